```python
import jax, jax.numpy as jnp
from jax import lax
import numpy as np

D_MODEL = 1024
BATCH = 8
SEQ = 4096
DEPTH = 1
DEC_BATCH = 32
DEC_SEQ = 32
PAST_LEN = 1024

CHUNK = 64
HEAD_DIM = 64
A_HEADS = 8
A_WIDTH = A_HEADS * HEAD_DIM
B_WIDTH = 256
CONV_W = 3
M_HEADS = 4
M_WIDTH = M_HEADS * HEAD_DIM
N_MEM = 256
MIX_WIDTH = A_WIDTH + B_WIDTH + M_WIDTH
IN_COLS = 3 * A_WIDTH + A_HEADS + 3 * B_WIDTH + M_WIDTH
Q_BLOCK = 128
PEER_HEADS = 8
PEER_KEYS = 128
PEER_N = PEER_KEYS * PEER_KEYS
PEER_QDIM = 128
PEER_HALF = PEER_QDIM // 2
PEER_TOPK = 16
PEER_BLOCK = 128
EPS = 1e-6

kernel_name = 'fox_shortconv_memory_peer_stream_step'


def rmsnorm(x, g):
    xf = x.astype(jnp.float32)
    y = xf * lax.rsqrt(jnp.mean(xf * xf, axis=-1, keepdims=True) + EPS)
    return (y * g.astype(jnp.float32)).astype(x.dtype)


def _in_splits():
    sizes = [A_WIDTH, A_WIDTH, A_WIDTH, A_HEADS, B_WIDTH, B_WIDTH, B_WIDTH, M_WIDTH]
    return [int(c) for c in np.cumsum(sizes)[:-1]]


def mixer_inputs(h, w_in, b_f, g_qa, g_ka, g_qm):
    b, t = h.shape[0], h.shape[1]
    z = h @ w_in
    qa, ka, va, fl, bg, cg, hv, qm = jnp.split(z, _in_splits(), axis=-1)
    qa = rmsnorm(qa.reshape(b, t, A_HEADS, HEAD_DIM), g_qa)
    ka = rmsnorm(ka.reshape(b, t, A_HEADS, HEAD_DIM), g_ka)
    va = va.reshape(b, t, A_HEADS, HEAD_DIM)
    logf = jax.nn.log_sigmoid(fl.astype(jnp.float32) + b_f.astype(jnp.float32))
    qm = rmsnorm(qm.reshape(b, t, M_HEADS, HEAD_DIM), g_qm)
    return qa, ka, va, logf, bg, cg * hv, qm


def fox_attend(q, k, v, fq, fk, q_pos, k_pos):
    s = jnp.einsum('bqhd,bkhd->bhqk', q, k).astype(jnp.float32) * (HEAD_DIM ** -0.5)
    s = s + jnp.transpose(fq, (0, 2, 1))[:, :, :, None] - jnp.transpose(fk, (0, 2, 1))[:, :, None, :]
    mask = k_pos[None, :] <= q_pos[:, None]
    s = jnp.where(mask[None, None], s, -jnp.inf)
    p = jax.nn.softmax(s, axis=-1).astype(v.dtype)
    return jnp.einsum('bhqk,bkhd->bqhd', p, v)


def fox_prompt(q, k, v, F):
    b, s = q.shape[0], q.shape[1]
    nb = s // Q_BLOCK
    qb = q.reshape(b, nb, Q_BLOCK, A_HEADS, HEAD_DIM).transpose(1, 0, 2, 3, 4)
    fb = F.reshape(b, nb, Q_BLOCK, A_HEADS).transpose(1, 0, 2, 3)
    k_pos = jnp.arange(s)

    def one(args):
        i, qi, fi = args
        q_pos = i * Q_BLOCK + jnp.arange(Q_BLOCK)
        return fox_attend(qi, k, v, fi, F, q_pos, k_pos)

    out = lax.map(one, (jnp.arange(nb), qb, fb))
    return out.transpose(1, 0, 2, 3, 4).reshape(b, s, A_WIDTH)


def short_conv(u, prev, w, bias):
    t = u.shape[1]
    up = jnp.concatenate([prev.astype(u.dtype), u], axis=1)
    y = bias + w[CONV_W - 1] * up[:, CONV_W - 1:CONV_W - 1 + t]
    for j in range(CONV_W - 1):
        y = y + w[j] * up[:, j:j + t]
    return y, up[:, -(CONV_W - 1):]


def memory_kv(mem, g_mem, w_mk, w_mv, g_km):
    bm = mem.shape[0]
    hm = rmsnorm(mem, g_mem)
    mk = rmsnorm((hm @ w_mk).reshape(bm, N_MEM, M_HEADS, HEAD_DIM), g_km)
    mv = (hm @ w_mv).reshape(bm, N_MEM, M_HEADS, HEAD_DIM)
    return mk, mv


def memory_attend(q, mk, mv):
    b, t = q.shape[0], q.shape[1]
    s = jnp.einsum('bqhd,bkhd->bhqk', q, mk).astype(jnp.float32) * (HEAD_DIM ** -0.5)
    p = jax.nn.softmax(s, axis=-1).astype(mv.dtype)
    return jnp.einsum('bhqk,bkhd->bqhd', p, mv).reshape(b, t, M_WIDTH)


def merge_out(oa, ob, om, w_out):
    return jnp.concatenate([oa, ob, om], axis=-1) @ w_out


def peer_tokens(xt, w_pq, keys, u_tab, v_tab):
    n = xt.shape[0]
    q = (xt @ w_pq).reshape(n, PEER_HEADS, 2, PEER_HALF)
    s = jnp.einsum('nhpd,hpkd->nhpk', q, keys).astype(jnp.float32)
    s1, i1 = lax.top_k(s[:, :, 0], PEER_TOPK)
    s2, i2 = lax.top_k(s[:, :, 1], PEER_TOPK)
    cand = (s1[..., :, None] + s2[..., None, :]).reshape(n, PEER_HEADS, PEER_TOPK * PEER_TOPK)
    sc, ci = lax.top_k(cand, PEER_TOPK)
    e = (jnp.take_along_axis(i1, ci // PEER_TOPK, axis=-1) * PEER_KEYS
         + jnp.take_along_axis(i2, ci % PEER_TOPK, axis=-1))
    g = jax.nn.softmax(sc, axis=-1)
    a = jax.nn.gelu(jnp.einsum('nhkd,nd->nhk', u_tab[e], xt).astype(jnp.float32), approximate=False)
    wgt = (g * a).astype(xt.dtype)
    return jnp.einsum('nhk,nhkd->nd', wgt, v_tab[e])


def peer(x, w_pq, keys, u_tab, v_tab):
    b, t, d = x.shape
    n = b * t
    nb = -(-n // PEER_BLOCK)
    xt = jnp.pad(x.reshape(n, d), ((0, nb * PEER_BLOCK - n), (0, 0)))
    out = lax.map(lambda xb: peer_tokens(xb, w_pq, keys, u_tab, v_tab), xt.reshape(nb, PEER_BLOCK, d))
    return out.reshape(nb * PEER_BLOCK, d)[:n].reshape(b, t, d)


def setup_inputs(seed: int = 0) -> dict:
    key = jax.random.key(seed)
    ks = iter(jax.random.split(key, 32))

    def nrm(shape, scale):
        return jax.random.normal(next(ks), shape, jnp.float32) * scale

    forget_bias = jnp.linspace(1.0, 5.0, A_HEADS, dtype=jnp.float32)
    return {
        'x_prompt': nrm((BATCH, SEQ, D_MODEL), 1.0),
        'x_sample': nrm((DEC_BATCH, DEC_SEQ, D_MODEL), 1.0),
        'cache_a_k': nrm((DEPTH, DEC_BATCH, PAST_LEN, A_HEADS, HEAD_DIM), 1.0),
        'cache_a_v': nrm((DEPTH, DEC_BATCH, PAST_LEN, A_HEADS, HEAD_DIM), 1.0),
        'cache_a_logf': jax.nn.log_sigmoid(nrm((DEPTH, DEC_BATCH, PAST_LEN, A_HEADS), 1.0) + forget_bias),
        'cache_b_conv': nrm((DEPTH, DEC_BATCH, CONV_W - 1, B_WIDTH), 0.5),
        'cache_m_k': nrm((DEPTH, DEC_BATCH, N_MEM, M_HEADS, HEAD_DIM), 1.0),
        'cache_m_v': nrm((DEPTH, DEC_BATCH, N_MEM, M_HEADS, HEAD_DIM), 1.0),
        'mem_prompt': nrm((BATCH, N_MEM, D_MODEL), 1.0),
        'g_attn': 1.0 + nrm((DEPTH, D_MODEL), 0.02),
        'w_in': nrm((DEPTH, D_MODEL, IN_COLS), D_MODEL ** -0.5),
        'b_f': forget_bias[None, :] + nrm((DEPTH, A_HEADS), 0.1),
        'g_q_a': 1.0 + nrm((DEPTH, HEAD_DIM), 0.02),
        'g_k_a': 1.0 + nrm((DEPTH, HEAD_DIM), 0.02),
        'conv_w': nrm((DEPTH, CONV_W, B_WIDTH), CONV_W ** -0.5),
        'conv_b': nrm((DEPTH, B_WIDTH), 0.02),
        'g_mem': 1.0 + nrm((DEPTH, D_MODEL), 0.02),
        'w_mem_k': nrm((DEPTH, D_MODEL, M_WIDTH), D_MODEL ** -0.5),
        'w_mem_v': nrm((DEPTH, D_MODEL, M_WIDTH), D_MODEL ** -0.5),
        'g_k_m': 1.0 + nrm((DEPTH, HEAD_DIM), 0.02),
        'g_q_m': 1.0 + nrm((DEPTH, HEAD_DIM), 0.02),
        'w_out': nrm((DEPTH, MIX_WIDTH, D_MODEL), 0.5 * MIX_WIDTH ** -0.5),
        'g_ffn': 1.0 + nrm((DEPTH, D_MODEL), 0.02),
        'w_peer_q': nrm((DEPTH, D_MODEL, PEER_HEADS * PEER_QDIM), D_MODEL ** -0.5),
        'peer_keys': nrm((DEPTH, PEER_HEADS, 2, PEER_KEYS, PEER_HALF), PEER_HALF ** -0.5),
        'peer_u': nrm((DEPTH, PEER_N, D_MODEL), D_MODEL ** -0.5),
        'peer_v': nrm((DEPTH, PEER_N, D_MODEL), PEER_HEADS ** -0.5),
    }


def reference(x_prompt, x_sample, cache_a_k, cache_a_v, cache_a_logf, cache_b_conv, cache_m_k, cache_m_v,
              mem_prompt, g_attn, w_in, b_f, g_q_a, g_k_a, conv_w, conv_b, g_mem, w_mem_k, w_mem_v,
              g_k_m, g_q_m, w_out, g_ffn, w_peer_q, peer_keys, peer_u, peer_v):
    xp, xs = x_prompt, x_sample
    akp, avp, alp, bcp, mkp, mvp = [], [], [], [], [], []
    aks, avs, als, bcs = [], [], [], []
    for l in range(DEPTH):
        bp = xp.shape[0]
        h = rmsnorm(xp, g_attn[l])
        qa, ka, va, logf, bg, u, qm = mixer_inputs(h, w_in[l], b_f[l], g_q_a[l], g_k_a[l], g_q_m[l])
        F = jnp.cumsum(logf, axis=1)
        oa = fox_prompt(qa, ka, va, F)
        cy, conv_state = short_conv(u, jnp.zeros((bp, CONV_W - 1, B_WIDTH), u.dtype), conv_w[l], conv_b[l])
        mk, mv = memory_kv(mem_prompt, g_mem[l], w_mem_k[l], w_mem_v[l], g_k_m[l])
        om = memory_attend(qm, mk, mv)
        xp = xp + merge_out(oa, bg * cy, om, w_out[l])
        xp = xp + peer(rmsnorm(xp, g_ffn[l]), w_peer_q[l], peer_keys[l], peer_u[l], peer_v[l])
        akp.append(ka); avp.append(va); alp.append(logf); bcp.append(conv_state)
        mkp.append(mk); mvp.append(mv)

        bs, ts = xs.shape[0], xs.shape[1]
        past = cache_a_k.shape[2]
        h = rmsnorm(xs, g_attn[l])
        qa, ka, va, logf, bg, u, qm = mixer_inputs(h, w_in[l], b_f[l], g_q_a[l], g_k_a[l], g_q_m[l])
        k_all = jnp.concatenate([cache_a_k[l].astype(ka.dtype), ka], axis=1)
        v_all = jnp.concatenate([cache_a_v[l].astype(va.dtype), va], axis=1)
        F = jnp.cumsum(jnp.concatenate([cache_a_logf[l].astype(jnp.float32), logf], axis=1), axis=1)
        oa = fox_attend(qa, k_all, v_all, F[:, past:], F,
                        past + jnp.arange(ts), jnp.arange(past + ts)).reshape(bs, ts, A_WIDTH)
        cy, conv_state = short_conv(u, cache_b_conv[l], conv_w[l], conv_b[l])
        om = memory_attend(qm, cache_m_k[l].astype(qm.dtype), cache_m_v[l].astype(qm.dtype))
        xs = xs + merge_out(oa, bg * cy, om, w_out[l])
        xs = xs + peer(rmsnorm(xs, g_ffn[l]), w_peer_q[l], peer_keys[l], peer_u[l], peer_v[l])
        aks.append(ka); avs.append(va); als.append(logf); bcs.append(conv_state)

    return (xp, xs,
            jnp.stack(akp), jnp.stack(avp), jnp.stack(alp), jnp.stack(bcp), jnp.stack(mkp), jnp.stack(mvp),
            jnp.stack(aks), jnp.stack(avs), jnp.stack(als), jnp.stack(bcs))
```

```python
import functools

import numpy as np
import jax
import jax.numpy as jnp
from jax import lax
from jax.experimental import pallas as pl
from jax.experimental.pallas import tpu as pltpu

F32, BF16, I32 = jnp.float32, jnp.bfloat16, jnp.int32
EPS = 1e-6
LANES = 128
SUBLANES = 8
HEAD_DIM = 64
A_HEADS = 8
A_WIDTH = A_HEADS * HEAD_DIM
B_WIDTH = 256
M_HEADS = 4
M_WIDTH = M_HEADS * HEAD_DIM
PEER_HEADS = 8
PEER_KEYS = 128
PEER_TOPK = 16
PEER_SLOTS = PEER_HEADS * PEER_TOPK
D_MODEL = 1024
ROW_WORDS = D_MODEL // 2
ROW_SUB = ROW_WORDS // LANES
NEG_BIG = -1e30
VMEM_LIMIT = 56 * 1024 * 1024

C_Q, C_K, C_V, C_BG, C_CG, C_HV, C_QM, C_FL, C_END = 0, 512, 1024, 1536, 1792, 2048, 2304, 2560, 2688


def _dot(a, b):
    return jnp.dot(a, b, preferred_element_type=F32)


def _dot_nt(a, b):
    return lax.dot_general(a, b, (((1,), (1,)), ((), ())), preferred_element_type=F32)


def _split3(x):
    hi = x.astype(BF16)
    r1 = x - hi.astype(F32)
    mid = r1.astype(BF16)
    lo = (r1 - mid.astype(F32)).astype(BF16)
    return hi, mid, lo


def _lane(shape):
    return lax.broadcasted_iota(I32, shape, len(shape) - 1)


def _memkv_body(mem_ref, g_ref, wk_ref, wv_ref, gk_ref, bd_ref, mk_ref, mv_ref):
    x = mem_ref[0]
    h = (x * lax.rsqrt(jnp.mean(x * x, axis=-1, keepdims=True) + EPS) * g_ref[...]).astype(BF16)
    zk = _dot(h, wk_ref[...])
    ms = _dot((zk * zk).astype(BF16), bd_ref[...])
    mk_ref[0] = zk * lax.rsqrt(ms + EPS) * gk_ref[...]
    mv_ref[0] = _dot(h, wv_ref[...])


def _memkv_call(mem, g_mem, w_mk, w_mv, gk_t, bd256):
    b, n_mem, d = mem.shape
    full = lambda shape: pl.BlockSpec(shape, lambda i: (0,) * len(shape))
    return pl.pallas_call(
        _memkv_body,
        grid=(b,),
        in_specs=[pl.BlockSpec((1, n_mem, d), lambda i: (i, 0, 0)), full((1, d)), full((d, M_WIDTH)),
                  full((d, M_WIDTH)), full((1, M_WIDTH)), full((M_WIDTH, M_WIDTH))],
        out_specs=[pl.BlockSpec((1, n_mem, M_WIDTH), lambda i: (i, 0, 0))] * 2,
        out_shape=[jax.ShapeDtypeStruct((b, n_mem, M_WIDTH), F32)] * 2,
        compiler_params=pltpu.CompilerParams(dimension_semantics=("arbitrary",), vmem_limit_bytes=VMEM_LIMIT),
        name="mem_kv",
    )(mem, g_mem, w_mk, w_mv, gk_t, bd256)


def _proj_body(x_ref, prev_ref, mk_ref, mv_ref, gat_ref, w_ref, bf_ref, gq_ref, gk_ref, gqm_ref,
               bd512_ref, bd256_ref, pfq_ref, pfk_ref, cq_ref, ck_ref, cw_ref, cb_ref,
               ka_ref, va_ref, logf_ref, qaug_ref, kaug_ref, vb_ref, ob_ref, om_ref, cst_ref,
               fcarry, ucarry):
    t = pl.program_id(1)
    tt = x_ref.shape[1]
    x = x_ref[0]
    h = (x * lax.rsqrt(jnp.mean(x * x, axis=-1, keepdims=True) + EPS) * gat_ref[...]).astype(BF16)

    def proj(c0, c1):
        return _dot(h, w_ref[:, c0:c1])

    @pl.when(t == 0)
    def _():
        fcarry[...] = jnp.zeros_like(fcarry)
        ucarry[...] = jnp.zeros_like(ucarry)
        ucarry[SUBLANES - 2:SUBLANES, :] = prev_ref[0]

    v = proj(C_FL, C_END) + bf_ref[...]
    logf = jnp.minimum(v, 0.0) - jnp.log1p(jnp.exp(-jnp.abs(v)))
    logf = jnp.where(_lane(logf.shape) < A_HEADS, logf, 0.0)
    logf_ref[0] = logf[:, :A_HEADS]
    row = lax.broadcasted_iota(I32, (tt, tt), 0)
    col = lax.broadcasted_iota(I32, (tt, tt), 1)
    tri = jnp.where(row >= col, 1.0, 0.0).astype(BF16)
    lh, lm, ll = _split3(logf)
    fcum = _dot(tri, lh) + _dot(tri, lm) + _dot(tri, ll) + fcarry[0:1, :]
    fcarry[...] = jnp.broadcast_to(fcum[tt - 1:tt, :], fcarry.shape)
    fparts = jnp.concatenate(_split3(fcum), axis=1)
    faq = _dot(fparts, pfq_ref[...]) + cq_ref[...]
    fak = _dot(fparts, pfk_ref[...]) + ck_ref[...]

    lane = _lane((tt, LANES))
    zq = proj(C_Q, C_K)
    qn = zq * lax.rsqrt(_dot((zq * zq).astype(BF16), bd512_ref[...]) + EPS) * gq_ref[...]
    zk = proj(C_K, C_V)
    kn = zk * lax.rsqrt(_dot((zk * zk).astype(BF16), bd512_ref[...]) + EPS) * gk_ref[...]
    ka_ref[0] = kn
    for hd in range(A_HEADS):
        c0 = LANES * (hd // 2)
        qt, kt = qn[:, c0:c0 + LANES], kn[:, c0:c0 + LANES]
        if hd % 2:
            qt, kt = pltpu.roll(qt, HEAD_DIM, 1), pltpu.roll(kt, HEAD_DIM, 1)
        qaug_ref[0, hd] = jnp.where(lane < HEAD_DIM, qt, faq[:, LANES * hd:LANES * (hd + 1)]).astype(BF16)
        kaug_ref[0, hd] = jnp.where(lane < HEAD_DIM, kt, fak[:, LANES * hd:LANES * (hd + 1)]).astype(BF16)
    zv = proj(C_V, C_BG)
    va_ref[0] = zv
    vb_ref[0] = zv.astype(BF16)

    u = proj(C_CG, C_HV) * proj(C_HV, C_QM)
    rows = lax.broadcasted_iota(I32, u.shape, 0)
    p1 = ucarry[SUBLANES - 1:SUBLANES, :]
    p2 = ucarry[SUBLANES - 2:SUBLANES - 1, :]
    u1 = jnp.where(rows == 0, p1, pltpu.roll(u, 1, 0))
    u2 = jnp.where(rows == 0, p2, jnp.where(rows == 1, p1, pltpu.roll(u, 2, 0)))
    cy = cb_ref[...] + cw_ref[2:3, :] * u + cw_ref[0:1, :] * u2 + cw_ref[1:2, :] * u1
    ob_ref[0] = (proj(C_BG, C_CG) * cy).astype(BF16)
    ucarry[...] = u[tt - SUBLANES:tt, :]
    cst_ref[0] = u[tt - 2:tt, :]

    zm = proj(C_QM, C_FL)
    qm = zm * lax.rsqrt(_dot((zm * zm).astype(BF16), bd256_ref[...]) + EPS) * gqm_ref[...]
    mkb = mk_ref[0].astype(BF16)
    mvb = mv_ref[0].astype(BF16)
    outs = []
    for pr in range(M_HEADS // 2):
        qp = qm[:, LANES * pr:LANES * (pr + 1)]
        kp = mkb[:, LANES * pr:LANES * (pr + 1)]
        vp = mvb[:, LANES * pr:LANES * (pr + 1)]
        o = []
        for sub in range(2):
            keep = (lane < HEAD_DIM) if sub == 0 else (lane >= HEAD_DIM)
            s = _dot_nt(jnp.where(keep, qp, 0.0).astype(BF16), kp)
            p = jnp.exp(s - jnp.max(s, axis=-1, keepdims=True))
            o.append(_dot(p.astype(BF16), vp) / jnp.sum(p, axis=-1, keepdims=True))
        outs.append(jnp.where(lane < HEAD_DIM, o[0], o[1]))
    om_ref[0] = jnp.concatenate(outs, axis=1).astype(BF16)


def _proj_call(x, prev, mk, mv, wts, tile):
    b, s, d = x.shape
    nt = s // tile
    n_mem = mk.shape[1]
    full = lambda a: pl.BlockSpec(a.shape, lambda i, j: (0,) * a.ndim)
    seq = lambda w: pl.BlockSpec((1, tile, w), lambda i, j: (i, j, 0))
    per_b = lambda r, w: pl.BlockSpec((1, r, w), lambda i, j: (i, 0, 0))
    heads = pl.BlockSpec((1, A_HEADS, tile, LANES), lambda i, j: (i, 0, j, 0))
    names = ["g_attn", "w_in", "b_f", "g_q", "g_k", "g_qm", "bd512", "bd256", "pfq", "pfk", "cq", "ck", "conv_w", "conv_b"]
    consts = [wts[k] for k in names]
    out_shape = [
        jax.ShapeDtypeStruct((b, s, A_WIDTH), F32),
        jax.ShapeDtypeStruct((b, s, A_WIDTH), F32),
        jax.ShapeDtypeStruct((b, s, A_HEADS), F32),
        jax.ShapeDtypeStruct((b, A_HEADS, s, LANES), BF16),
        jax.ShapeDtypeStruct((b, A_HEADS, s, LANES), BF16),
        jax.ShapeDtypeStruct((b, s, A_WIDTH), BF16),
        jax.ShapeDtypeStruct((b, s, B_WIDTH), BF16),
        jax.ShapeDtypeStruct((b, s, M_WIDTH), BF16),
        jax.ShapeDtypeStruct((b, 2, B_WIDTH), F32),
    ]
    out_specs = [seq(A_WIDTH), seq(A_WIDTH), seq(A_HEADS), heads, heads, seq(A_WIDTH), seq(B_WIDTH), seq(M_WIDTH),
                 per_b(2, B_WIDTH)]
    return pl.pallas_call(
        _proj_body,
        grid=(b, nt),
        in_specs=[seq(d), per_b(2, B_WIDTH), per_b(n_mem, M_WIDTH), per_b(n_mem, M_WIDTH)] + [full(c) for c in consts],
        out_specs=out_specs,
        out_shape=out_shape,
        scratch_shapes=[pltpu.VMEM((SUBLANES, LANES), F32), pltpu.VMEM((SUBLANES, B_WIDTH), F32)],
        compiler_params=pltpu.CompilerParams(dimension_semantics=("arbitrary", "arbitrary"), vmem_limit_bytes=VMEM_LIMIT),
        name="proj",
    )(x, prev, mk, mv, *consts)


def _fox_prompt_body(q_ref, k_ref, v_ref, o_ref, *, blk):
    qi = pl.program_id(2)
    ones = jnp.ones((blk, LANES), BF16)
    row = lax.broadcasted_iota(I32, (blk, blk), 0)
    col = lax.broadcasted_iota(I32, (blk, blk), 1)
    outs = []
    for sub in range(2):
        q = q_ref[0, sub]

        def step(kj, carry, masked):
            m, acc = carry
            off = pl.multiple_of(kj * blk, blk)
            s = _dot_nt(q, k_ref[0, sub, pl.ds(off, blk), :])
            if masked:
                s = jnp.where(col <= row, s, NEG_BIG)
            m_new = jnp.maximum(m, jnp.max(s, axis=-1, keepdims=True))
            p = jnp.exp(s - m_new).astype(BF16)
            vv = jnp.concatenate([v_ref[0, pl.ds(off, blk), :], ones], axis=1)
            return m_new, jnp.exp(m - m_new) * acc + _dot(p, vv)

        init = (jnp.full((blk, 1), NEG_BIG, F32), jnp.zeros((blk, 2 * LANES), F32))
        carry = lax.fori_loop(0, qi, functools.partial(step, masked=False), init)
        _, acc = step(qi, carry, True)
        outs.append(acc[:, :LANES] / acc[:, LANES:])
    o_ref[0] = jnp.where(_lane((blk, LANES)) < HEAD_DIM, outs[0], outs[1]).astype(BF16)


def _fox_prompt_call(qaug, kaug, vb, blk):
    b, _, s, _ = qaug.shape
    return pl.pallas_call(
        functools.partial(_fox_prompt_body, blk=blk),
        grid=(b, A_HEADS // 2, s // blk),
        in_specs=[pl.BlockSpec((1, 2, blk, LANES), lambda i, hp, j: (i, hp, j, 0)),
                  pl.BlockSpec((1, 2, s, LANES), lambda i, hp, j: (i, hp, 0, 0)),
                  pl.BlockSpec((1, s, LANES), lambda i, hp, j: (i, 0, hp))],
        out_specs=pl.BlockSpec((1, blk, LANES), lambda i, hp, j: (i, j, hp)),
        out_shape=jax.ShapeDtypeStruct((b, s, A_WIDTH), BF16),
        compiler_params=pltpu.CompilerParams(dimension_semantics=("arbitrary",) * 3, vmem_limit_bytes=VMEM_LIMIT),
        name="fox_prompt",
    )(qaug, kaug, vb)


def _fox_sample_body(ck_ref, cv_ref, clf_ref, q_ref, k_ref, v_ref, pfk_ref, ckc_ref, o_ref):
    past = ck_ref.shape[1]
    ts = q_ref.shape[2]
    row = lax.broadcasted_iota(I32, (past, past), 0)
    col = lax.broadcasted_iota(I32, (past, past), 1)
    tri = jnp.where(col > row, 1.0, 0.0).astype(BF16)
    lh, lm, ll = _split3(clf_ref[0])
    suffix = _dot(tri, lh) + _dot(tri, lm) + _dot(tri, ll)
    fak = _dot(jnp.concatenate(_split3(-suffix), axis=1), pfk_ref[...]) + ckc_ref[...]
    lane = _lane((past, LANES))
    lane_s = _lane((ts, LANES))
    causal = lax.broadcasted_iota(I32, (ts, ts), 1) <= lax.broadcasted_iota(I32, (ts, ts), 0)
    outs = []
    for pr in range(A_HEADS // 2):
        kc2 = ck_ref[0, :, LANES * pr:LANES * (pr + 1)]
        vc = cv_ref[0, :, LANES * pr:LANES * (pr + 1)].astype(BF16)
        vn = v_ref[0, :, LANES * pr:LANES * (pr + 1)]
        o = []
        for sub in range(2):
            hd = 2 * pr + sub
            kt = pltpu.roll(kc2, HEAD_DIM, 1) if sub else kc2
            kc = jnp.where(lane < HEAD_DIM, kt, fak[:, LANES * hd:LANES * (hd + 1)]).astype(BF16)
            q = q_ref[0, hd]
            s1 = _dot_nt(q, kc)
            s2 = jnp.where(causal, _dot_nt(q, k_ref[0, hd]), NEG_BIG)
            m = jnp.maximum(jnp.max(s1, axis=-1, keepdims=True), jnp.max(s2, axis=-1, keepdims=True))
            p1, p2 = jnp.exp(s1 - m), jnp.exp(s2 - m)
            den = jnp.sum(p1, axis=-1, keepdims=True) + jnp.sum(p2, axis=-1, keepdims=True)
            o.append((_dot(p1.astype(BF16), vc) + _dot(p2.astype(BF16), vn)) / den)
        outs.append(jnp.where(lane_s < HEAD_DIM, o[0], o[1]))
    o_ref[0] = jnp.concatenate(outs, axis=1).astype(BF16)


def _fox_sample_call(cache_k, cache_v, cache_lf, qaug, kaug, vb, pfk, ck):
    b, past, _ = cache_k.shape
    ts = qaug.shape[2]
    full = lambda a: pl.BlockSpec(a.shape, lambda i: (0,) * a.ndim)
    return pl.pallas_call(
        _fox_sample_body,
        grid=(b,),
        in_specs=[pl.BlockSpec((1, past, A_WIDTH), lambda i: (i, 0, 0)),
                  pl.BlockSpec((1, past, A_WIDTH), lambda i: (i, 0, 0)),
                  pl.BlockSpec((1, past, LANES), lambda i: (i, 0, 0)),
                  pl.BlockSpec((1, A_HEADS, ts, LANES), lambda i: (i, 0, 0, 0)),
                  pl.BlockSpec((1, A_HEADS, ts, LANES), lambda i: (i, 0, 0, 0)),
                  pl.BlockSpec((1, ts, A_WIDTH), lambda i: (i, 0, 0)),
                  full(pfk), full(ck)],
        out_specs=pl.BlockSpec((1, ts, A_WIDTH), lambda i: (i, 0, 0)),
        out_shape=jax.ShapeDtypeStruct((b, ts, A_WIDTH), BF16),
        compiler_params=pltpu.CompilerParams(dimension_semantics=("arbitrary",), vmem_limit_bytes=VMEM_LIMIT),
        name="fox_sample",
    )(cache_k, cache_v, cache_lf, qaug, kaug, vb, pfk, ck)


def _top_rows(vals, payload, k):
    r, t = vals.shape
    iota = lax.broadcasted_iota(I32, (r, t), 0)
    slot = lax.broadcasted_iota(I32, (k, t), 0)
    top_v = jnp.zeros((k, t), F32)
    top_p = jnp.zeros((k, t), I32)
    for j in range(k):
        m = jnp.max(vals, axis=0, keepdims=True)
        sel = jnp.min(jnp.where(vals == m, iota, r), axis=0, keepdims=True)
        hit = iota == sel
        pay = sel if payload is None else jnp.max(jnp.where(hit, payload, -1), axis=0, keepdims=True)
        top_v = jnp.where(slot == j, m, top_v)
        top_p = jnp.where(slot == j, pay, top_p)
        vals = jnp.where(hit, -jnp.inf, vals)
    return top_v, top_p


def _route_body(x_ref, oa_ref, ob_ref, om_ref, wo_ref, gf_ref, wq_ref, keys_ref,
                x1_ref, xn_ref, e_ref, g_ref, qp_scr, e_scr, g_scr):
    y = x_ref[...] + _dot(oa_ref[...], wo_ref[0:A_WIDTH, :]) \
        + _dot(ob_ref[...], wo_ref[A_WIDTH:A_WIDTH + B_WIDTH, :]) \
        + _dot(om_ref[...], wo_ref[A_WIDTH + B_WIDTH:, :])
    x1_ref[...] = y
    xn = y * lax.rsqrt(jnp.mean(y * y, axis=-1, keepdims=True) + EPS) * gf_ref[...]
    xn_ref[...] = xn
    qp = _dot(xn.astype(BF16), wq_ref[...])
    for hd in range(PEER_HEADS):
        qp_scr[hd] = qp[:, LANES * hd:LANES * (hd + 1)].astype(BF16)

    def head(hd, carry):
        sc = _dot_nt(keys_ref[hd], qp_scr[hd])
        s1, i1 = _top_rows(sc[:PEER_KEYS], None, PEER_TOPK)
        s2, i2 = _top_rows(sc[PEER_KEYS:], None, PEER_TOPK)
        cand = jnp.concatenate([s1[i:i + 1, :] + s2 for i in range(PEER_TOPK)], axis=0)
        expert = jnp.concatenate([i1[i:i + 1, :] * PEER_KEYS + i2 for i in range(PEER_TOPK)], axis=0)
        top, e = _top_rows(cand, expert, PEER_TOPK)
        p = jnp.exp(top - top[0:1, :])
        g_scr[hd] = p / jnp.sum(p, axis=0, keepdims=True)
        e_scr[hd] = e
        return carry

    lax.fori_loop(0, PEER_HEADS, head, 0)
    for hd in range(PEER_HEADS):
        e_ref[PEER_TOPK * hd:PEER_TOPK * (hd + 1), :] = e_scr[hd]
    g_ref[...] = jnp.concatenate([g_scr[hd] for hd in range(PEER_HEADS)], axis=0).T


def _route_call(x, oa, ob, om, wts, tile):
    n, d = x.shape
    full = lambda a: pl.BlockSpec(a.shape, lambda i: (0,) * a.ndim)
    rows = lambda w: pl.BlockSpec((tile, w), lambda i: (i, 0))
    consts = [wts[k] for k in ["w_out", "g_ffn", "w_pq", "keys"]]
    return pl.pallas_call(
        _route_body,
        grid=(n // tile,),
        in_specs=[rows(d), rows(A_WIDTH), rows(B_WIDTH), rows(M_WIDTH)] + [full(c) for c in consts],
        out_specs=[rows(d), rows(d), pl.BlockSpec((PEER_SLOTS, tile), lambda i: (0, i)), rows(PEER_SLOTS)],
        out_shape=[jax.ShapeDtypeStruct((n, d), F32), jax.ShapeDtypeStruct((n, d), F32),
                   jax.ShapeDtypeStruct((PEER_SLOTS, n), I32), jax.ShapeDtypeStruct((n, PEER_SLOTS), F32)],
        scratch_shapes=[pltpu.VMEM((PEER_HEADS, tile, LANES), BF16),
                        pltpu.VMEM((PEER_HEADS, PEER_TOPK, tile), I32),
                        pltpu.VMEM((PEER_HEADS, PEER_TOPK, tile), F32)],
        compiler_params=pltpu.CompilerParams(dimension_semantics=("arbitrary",), vmem_limit_bytes=VMEM_LIMIT),
        name="merge_route",
    )(x, oa, ob, om, *consts)


def _unpack_row(w):
    lo = pltpu.bitcast(w << 16, F32)
    hi = pltpu.bitcast(w & jnp.int32(-65536), F32)
    return lo, hi


def _fold8(ps, sub):
    cat = lambda a, b: jnp.concatenate([a, b], axis=0)
    y0, y1, y2, y3 = cat(ps[0], ps[4]), cat(ps[2], ps[6]), cat(ps[1], ps[5]), cat(ps[3], ps[7])

    def comb(a, b, m, d):
        return jnp.where(m, a, pltpu.roll(b, d, 0)) + jnp.where(m, pltpu.roll(a, SUBLANES - d, 0), b)

    m2 = (sub % 4) < 2
    return comb(comb(y0, y1, m2, 2), comb(y2, y3, m2, 2), (sub % 2) == 0, 1)


def _peer_u_body(e_ref, x_ref, g_ref, u_ref, w_ref, r_scr, a_scr, *, tile, batch):
    sub = lax.broadcasted_iota(I32, (SUBLANES, LANES), 0)
    ones = jnp.ones((SUBLANES, LANES), BF16)
    for sb in range(tile // batch):
        def token(tt, carry):
            t = sb * batch + tt
            x = x_ref[pl.ds(pl.multiple_of(t * SUBLANES, SUBLANES), SUBLANES), :]
            xlo, xhi = x[0:ROW_SUB], x[ROW_SUB:]
            for grp in range(PEER_SLOTS // SUBLANES):
                ps = []
                for j in range(SUBLANES):
                    idx = e_ref[grp * SUBLANES + j, t]
                    lo, hi = _unpack_row(u_ref[pl.ds(pl.multiple_of(idx * ROW_SUB, ROW_SUB), ROW_SUB), :])
                    ps.append(lo * xlo + hi * xhi)
                r_scr[pl.ds(pl.multiple_of(tt * PEER_SLOTS + grp * SUBLANES, SUBLANES), SUBLANES), :] = _fold8(ps, sub)
            return carry

        lax.fori_loop(0, batch, token, 0)
        r = r_scr[...]
        rhi = r.astype(BF16)
        rlo = (r - rhi.astype(F32)).astype(BF16)
        a = _dot_nt(ones, rhi) + _dot_nt(ones, rlo)
        for j in range(batch):
            a_scr[sb * batch + j:sb * batch + j + 1, :] = a[0:1, PEER_SLOTS * j:PEER_SLOTS * (j + 1)]
    a = a_scr[...]
    w = g_ref[...] * (0.5 * a * (1.0 + lax.erf(a * np.float32(np.sqrt(0.5)))))
    w_ref[...] = w.T


def _peer_u_call(e, xn8, gates, u_packed, tile, batch):
    n = e.shape[1]
    return pl.pallas_call(
        functools.partial(_peer_u_body, tile=tile, batch=batch),
        grid=(n // tile,),
        in_specs=[pl.BlockSpec((PEER_SLOTS, tile), lambda i: (0, i), memory_space=pltpu.SMEM),
                  pl.BlockSpec((tile * SUBLANES, LANES), lambda i: (i, 0)),
                  pl.BlockSpec((tile, PEER_SLOTS), lambda i: (i, 0)),
                  pl.BlockSpec(memory_space=pltpu.VMEM)],
        out_specs=pl.BlockSpec((PEER_SLOTS, tile), lambda i: (0, i)),
        out_shape=jax.ShapeDtypeStruct((PEER_SLOTS, n), F32),
        scratch_shapes=[pltpu.VMEM((batch * PEER_SLOTS, LANES), F32), pltpu.VMEM((tile, PEER_SLOTS), F32)],
        compiler_params=pltpu.CompilerParams(dimension_semantics=("arbitrary",), vmem_limit_bytes=VMEM_LIMIT),
        name="peer_u",
    )(e, xn8, gates, u_packed)


def _peer_v_body(e_ref, w_ref, x_ref, v_ref, y_ref, *, tile):
    def token(t, carry):
        acc = [jnp.zeros((ROW_SUB, LANES), F32) for _ in range(4)]
        for k in range(PEER_SLOTS):
            idx = e_ref[k, t]
            wk = w_ref[k, t]
            lo, hi = _unpack_row(v_ref[pl.ds(pl.multiple_of(idx * ROW_SUB, ROW_SUB), ROW_SUB), :])
            acc[2 * (k % 2)] = acc[2 * (k % 2)] + lo * wk
            acc[2 * (k % 2) + 1] = acc[2 * (k % 2) + 1] + hi * wk
        out = jnp.concatenate([acc[0] + acc[2], acc[1] + acc[3]], axis=0)
        rows = pl.ds(pl.multiple_of(t * SUBLANES, SUBLANES), SUBLANES)
        y_ref[rows, :] = x_ref[rows, :] + out
        return carry

    lax.fori_loop(0, tile, token, 0)


def _peer_v_call(e, w, x8, v_packed, tile):
    n = e.shape[1]
    return pl.pallas_call(
        functools.partial(_peer_v_body, tile=tile),
        grid=(n // tile,),
        in_specs=[pl.BlockSpec((PEER_SLOTS, tile), lambda i: (0, i), memory_space=pltpu.SMEM),
                  pl.BlockSpec((PEER_SLOTS, tile), lambda i: (0, i), memory_space=pltpu.SMEM),
                  pl.BlockSpec((tile * SUBLANES, LANES), lambda i: (i, 0)),
                  pl.BlockSpec(memory_space=pltpu.VMEM)],
        out_specs=pl.BlockSpec((tile * SUBLANES, LANES), lambda i: (i, 0)),
        out_shape=jax.ShapeDtypeStruct((n * SUBLANES, LANES), F32),
        compiler_params=pltpu.CompilerParams(dimension_semantics=("arbitrary",), vmem_limit_bytes=VMEM_LIMIT),
        name="peer_v",
    )(e, w, x8, v_packed)


def _pack_table(tab):
    n = tab.shape[0]
    b = lax.bitcast_convert_type(tab.astype(BF16), jnp.uint16).astype(jnp.uint32)
    packed = b[:, :ROW_WORDS] | (b[:, ROW_WORDS:] << 16)
    return lax.bitcast_convert_type(packed, I32).reshape(n * ROW_SUB, LANES)


def _block_mean(width):
    blk = np.arange(width) // HEAD_DIM
    return jnp.asarray((blk[:, None] == blk[None, :]) / HEAD_DIM, BF16)


def _forget_placement():
    pfq = np.zeros((3 * LANES, A_HEADS * LANES), np.float32)
    pfk = np.zeros((3 * LANES, A_HEADS * LANES), np.float32)
    cq = np.zeros((1, A_HEADS * LANES), np.float32)
    ck = np.zeros((1, A_HEADS * LANES), np.float32)
    for hd in range(A_HEADS):
        base = hd * LANES + HEAD_DIM
        for piece in range(3):
            pfq[piece * LANES + hd, base + piece] = 1.0
            pfk[piece * LANES + hd, base + 3 + piece] = -1.0
            cq[0, base + 3 + piece] = 1.0
            ck[0, base + piece] = 1.0
    return jnp.asarray(pfq, BF16), jnp.asarray(pfk, BF16), jnp.asarray(cq), jnp.asarray(ck)


def _layer_weights(l, g_attn, w_in, b_f, g_q_a, g_k_a, conv_w, conv_b, g_q_m, w_out, g_ffn, w_peer_q, peer_keys):
    scale = HEAD_DIM ** -0.5
    splits = np.cumsum([A_WIDTH, A_WIDTH, A_WIDTH, A_HEADS, B_WIDTH, B_WIDTH, B_WIDTH, M_WIDTH])
    wi = w_in[l]
    qa, ka, va, fl, bg, cg, hv, qm = [wi[:, a:b] for a, b in zip(np.r_[0, splits[:-1]], splits)]
    w_packed = jnp.concatenate([qa, ka, va, bg, cg, hv, qm, fl, jnp.zeros((wi.shape[0], LANES - A_HEADS), wi.dtype)],
                               axis=1).astype(BF16)
    pfq, pfk, cq, ck = _forget_placement()
    keys = peer_keys[l]
    zeros = jnp.zeros_like(keys[:, 0])
    keys2 = jnp.concatenate([jnp.concatenate([keys[:, 0], zeros], axis=-1),
                             jnp.concatenate([zeros, keys[:, 1]], axis=-1)], axis=1).astype(BF16)
    return {
        "g_attn": g_attn[l][None, :], "w_in": w_packed,
        "b_f": jnp.pad(b_f[l], (0, LANES - A_HEADS))[None, :],
        "g_q": (jnp.tile(g_q_a[l], A_HEADS) * scale)[None, :], "g_k": jnp.tile(g_k_a[l], A_HEADS)[None, :],
        "g_qm": (jnp.tile(g_q_m[l], M_HEADS) * scale)[None, :],
        "bd512": _block_mean(A_WIDTH), "bd256": _block_mean(M_WIDTH),
        "pfq": pfq, "pfk": pfk, "cq": cq, "ck": ck,
        "conv_w": conv_w[l], "conv_b": conv_b[l][None, :],
        "w_out": w_out[l].astype(BF16), "g_ffn": g_ffn[l][None, :], "w_pq": w_peer_q[l].astype(BF16), "keys": keys2,
    }


def _peer_and_merge(x, oa, ob, om, wts, u_packed, v_packed, route_tile, peer_tile, peer_batch):
    b, s, d = x.shape
    n = b * s
    x1, xn, e, gates = _route_call(x.reshape(n, d), oa.reshape(n, -1), ob.reshape(n, -1), om.reshape(n, -1), wts, route_tile)
    w = _peer_u_call(e, xn.reshape(n * SUBLANES, LANES), gates, u_packed, peer_tile, peer_batch)
    y8 = _peer_v_call(e, w, x1.reshape(n * SUBLANES, LANES), v_packed, peer_tile)
    return y8.reshape(b, s, d)


def kernel(x_prompt, x_sample, cache_a_k, cache_a_v, cache_a_logf, cache_b_conv, cache_m_k, cache_m_v, mem_prompt, g_attn, w_in, b_f, g_q_a, g_k_a, conv_w, conv_b, g_mem, w_mem_k, w_mem_v, g_k_m, g_q_m, w_out, g_ffn, w_peer_q, peer_keys, peer_u, peer_v):
    depth = w_in.shape[0]
    xp, xs = x_prompt, x_sample
    bp, sp, _ = xp.shape
    bs, ts, _ = xs.shape
    outs = [[] for _ in range(10)]
    for l in range(depth):
        wts = _layer_weights(l, g_attn, w_in, b_f, g_q_a, g_k_a, conv_w, conv_b, g_q_m, w_out, g_ffn, w_peer_q, peer_keys)
        u_packed, v_packed = _pack_table(peer_u[l]), _pack_table(peer_v[l])
        fox_blk = min(512, sp)
        route_tile = 256
        peer_tile, peer_batch = 256, 64

        mk, mv = _memkv_call(mem_prompt, g_mem[l][None, :], w_mem_k[l].astype(BF16), w_mem_v[l].astype(BF16),
                             jnp.tile(g_k_m[l], M_HEADS)[None, :], wts["bd256"])
        ka, va, logf, qaug, kaug, vb, ob, om, cst = _proj_call(
            xp, jnp.zeros((bp, 2, B_WIDTH), F32), mk, mv, wts, min(512, sp))
        oa = _fox_prompt_call(qaug, kaug, vb, fox_blk)
        xp = _peer_and_merge(xp, oa, ob, om, wts, u_packed, v_packed, route_tile, peer_tile, peer_batch)
        n_mem = mk.shape[1]
        for dst, val in zip(outs[:6], [ka.reshape(bp, sp, A_HEADS, HEAD_DIM), va.reshape(bp, sp, A_HEADS, HEAD_DIM), logf, cst,
                                       mk.reshape(bp, n_mem, M_HEADS, HEAD_DIM), mv.reshape(bp, n_mem, M_HEADS, HEAD_DIM)]):
            dst.append(val)

        past = cache_a_k.shape[2]
        ka, va, logf, qaug, kaug, vb, ob, om, cst = _proj_call(
            xs, cache_b_conv[l], cache_m_k[l].reshape(bs, -1, M_WIDTH), cache_m_v[l].reshape(bs, -1, M_WIDTH), wts, ts)
        clf = jnp.pad(cache_a_logf[l], ((0, 0), (0, 0), (0, LANES - A_HEADS)))
        oa = _fox_sample_call(cache_a_k[l].reshape(bs, past, A_WIDTH), cache_a_v[l].reshape(bs, past, A_WIDTH), clf,
                              qaug, kaug, vb, wts["pfk"], wts["ck"])
        xs = _peer_and_merge(xs, oa, ob, om, wts, u_packed, v_packed, route_tile, peer_tile, peer_batch)
        for dst, val in zip(outs[6:], [ka.reshape(bs, ts, A_HEADS, HEAD_DIM), va.reshape(bs, ts, A_HEADS, HEAD_DIM), logf, cst]):
            dst.append(val)

    return (xp, xs) + tuple(jnp.stack(o) for o in outs)
```

```python
import functools

import numpy as np
import jax
import jax.numpy as jnp
from jax import lax
from jax.experimental import pallas as pl
from jax.experimental.pallas import tpu as pltpu

F32, BF16, I32 = jnp.float32, jnp.bfloat16, jnp.int32
EPS = 1e-6
LANES = 128
SUBLANES = 8
HEAD_DIM = 64
A_HEADS = 8
A_WIDTH = A_HEADS * HEAD_DIM
B_WIDTH = 256
M_HEADS = 4
M_WIDTH = M_HEADS * HEAD_DIM
PEER_HEADS = 8
PEER_KEYS = 128
PEER_TOPK = 16
PEER_SLOTS = PEER_HEADS * PEER_TOPK
D_MODEL = 1024
ROW_WORDS = D_MODEL // 2
ROW_SUB = ROW_WORDS // LANES
NEG_BIG = -1e30
VMEM_LIMIT = 56 * 1024 * 1024

C_Q, C_K, C_V, C_BG, C_CG, C_HV, C_QM, C_FL, C_END = 0, 512, 1024, 1536, 1792, 2048, 2304, 2560, 2688


def _dot(a, b):
    return jnp.dot(a, b, preferred_element_type=F32)


def _dot_nt(a, b):
    return lax.dot_general(a, b, (((1,), (1,)), ((), ())), preferred_element_type=F32)


def _split3(x):
    hi = x.astype(BF16)
    r1 = x - hi.astype(F32)
    mid = r1.astype(BF16)
    lo = (r1 - mid.astype(F32)).astype(BF16)
    return hi, mid, lo


def _lane(shape):
    return lax.broadcasted_iota(I32, shape, len(shape) - 1)


def _memkv_body(mem_ref, g_ref, wk_ref, wv_ref, gk_ref, bd_ref, mk_ref, mv_ref):
    x = mem_ref[0]
    h = (x * lax.rsqrt(jnp.mean(x * x, axis=-1, keepdims=True) + EPS) * g_ref[...]).astype(BF16)
    zk = _dot(h, wk_ref[...])
    ms = _dot((zk * zk).astype(BF16), bd_ref[...])
    mk_ref[0] = zk * lax.rsqrt(ms + EPS) * gk_ref[...]
    mv_ref[0] = _dot(h, wv_ref[...])


def _memkv_call(mem, g_mem, w_mk, w_mv, gk_t, bd256):
    b, n_mem, d = mem.shape
    full = lambda shape: pl.BlockSpec(shape, lambda i: (0,) * len(shape))
    return pl.pallas_call(
        _memkv_body,
        grid=(b,),
        in_specs=[pl.BlockSpec((1, n_mem, d), lambda i: (i, 0, 0)), full((1, d)), full((d, M_WIDTH)),
                  full((d, M_WIDTH)), full((1, M_WIDTH)), full((M_WIDTH, M_WIDTH))],
        out_specs=[pl.BlockSpec((1, n_mem, M_WIDTH), lambda i: (i, 0, 0))] * 2,
        out_shape=[jax.ShapeDtypeStruct((b, n_mem, M_WIDTH), F32)] * 2,
        compiler_params=pltpu.CompilerParams(dimension_semantics=("arbitrary",), vmem_limit_bytes=VMEM_LIMIT),
        name="mem_kv",
    )(mem, g_mem, w_mk, w_mv, gk_t, bd256)


def _proj_body(x_ref, prev_ref, mk_ref, mv_ref, gat_ref, w_ref, bf_ref, gq_ref, gk_ref, gqm_ref,
               bd512_ref, bd256_ref, pfq_ref, pfk_ref, cq_ref, ck_ref, cw_ref, cb_ref,
               ka_ref, va_ref, logf_ref, qaug_ref, kaug_ref, vb_ref, ob_ref, om_ref, cst_ref,
               fcarry, ucarry):
    t = pl.program_id(1)
    tt = x_ref.shape[1]
    x = x_ref[0]
    h = (x * lax.rsqrt(jnp.mean(x * x, axis=-1, keepdims=True) + EPS) * gat_ref[...]).astype(BF16)

    def proj(c0, c1):
        return _dot(h, w_ref[:, c0:c1])

    @pl.when(t == 0)
    def _():
        fcarry[...] = jnp.zeros_like(fcarry)
        ucarry[...] = jnp.zeros_like(ucarry)
        ucarry[SUBLANES - 2:SUBLANES, :] = prev_ref[0]

    v = proj(C_FL, C_END) + bf_ref[...]
    logf = jnp.minimum(v, 0.0) - jnp.log1p(jnp.exp(-jnp.abs(v)))
    logf = jnp.where(_lane(logf.shape) < A_HEADS, logf, 0.0)
    logf_ref[0] = logf[:, :A_HEADS]
    row = lax.broadcasted_iota(I32, (tt, tt), 0)
    col = lax.broadcasted_iota(I32, (tt, tt), 1)
    tri = jnp.where(row >= col, 1.0, 0.0).astype(BF16)
    lh, lm, ll = _split3(logf)
    fcum = _dot(tri, lh) + _dot(tri, lm) + _dot(tri, ll) + fcarry[0:1, :]
    fcarry[...] = jnp.broadcast_to(fcum[tt - 1:tt, :], fcarry.shape)
    fparts = jnp.concatenate(_split3(fcum), axis=1)
    faq = _dot(fparts, pfq_ref[...]) + cq_ref[...]
    fak = _dot(fparts, pfk_ref[...]) + ck_ref[...]

    lane = _lane((tt, LANES))
    zq = proj(C_Q, C_K)
    qn = zq * lax.rsqrt(_dot((zq * zq).astype(BF16), bd512_ref[...]) + EPS) * gq_ref[...]
    zk = proj(C_K, C_V)
    kn = zk * lax.rsqrt(_dot((zk * zk).astype(BF16), bd512_ref[...]) + EPS) * gk_ref[...]
    ka_ref[0] = kn
    for hd in range(A_HEADS):
        c0 = LANES * (hd // 2)
        qt, kt = qn[:, c0:c0 + LANES], kn[:, c0:c0 + LANES]
        if hd % 2:
            qt, kt = pltpu.roll(qt, HEAD_DIM, 1), pltpu.roll(kt, HEAD_DIM, 1)
        qaug_ref[0, hd] = jnp.where(lane < HEAD_DIM, qt, faq[:, LANES * hd:LANES * (hd + 1)]).astype(BF16)
        kaug_ref[0, hd] = jnp.where(lane < HEAD_DIM, kt, fak[:, LANES * hd:LANES * (hd + 1)]).astype(BF16)
    zv = proj(C_V, C_BG)
    va_ref[0] = zv
    vb_ref[0] = zv.astype(BF16)

    u = proj(C_CG, C_HV) * proj(C_HV, C_QM)
    rows = lax.broadcasted_iota(I32, u.shape, 0)
    p1 = ucarry[SUBLANES - 1:SUBLANES, :]
    p2 = ucarry[SUBLANES - 2:SUBLANES - 1, :]
    u1 = jnp.where(rows == 0, p1, pltpu.roll(u, 1, 0))
    u2 = jnp.where(rows == 0, p2, jnp.where(rows == 1, p1, pltpu.roll(u, 2, 0)))
    cy = cb_ref[...] + cw_ref[2:3, :] * u + cw_ref[0:1, :] * u2 + cw_ref[1:2, :] * u1
    ob_ref[0] = (proj(C_BG, C_CG) * cy).astype(BF16)
    ucarry[...] = u[tt - SUBLANES:tt, :]
    cst_ref[0] = u[tt - 2:tt, :]

    zm = proj(C_QM, C_FL)
    qm = zm * lax.rsqrt(_dot((zm * zm).astype(BF16), bd256_ref[...]) + EPS) * gqm_ref[...]
    mkb = mk_ref[0].astype(BF16)
    mvb = mv_ref[0].astype(BF16)
    outs = []
    for pr in range(M_HEADS // 2):
        qp = qm[:, LANES * pr:LANES * (pr + 1)]
        kp = mkb[:, LANES * pr:LANES * (pr + 1)]
        vp = mvb[:, LANES * pr:LANES * (pr + 1)]
        o = []
        for sub in range(2):
            keep = (lane < HEAD_DIM) if sub == 0 else (lane >= HEAD_DIM)
            s = _dot_nt(jnp.where(keep, qp, 0.0).astype(BF16), kp)
            p = jnp.exp(s - jnp.max(s, axis=-1, keepdims=True))
            o.append(_dot(p.astype(BF16), vp) / jnp.sum(p, axis=-1, keepdims=True))
        outs.append(jnp.where(lane < HEAD_DIM, o[0], o[1]))
    om_ref[0] = jnp.concatenate(outs, axis=1).astype(BF16)


def _proj_call(x, prev, mk, mv, wts, tile):
    b, s, d = x.shape
    nt = s // tile
    n_mem = mk.shape[1]
    full = lambda a: pl.BlockSpec(a.shape, lambda i, j: (0,) * a.ndim)
    seq = lambda w: pl.BlockSpec((1, tile, w), lambda i, j: (i, j, 0))
    per_b = lambda r, w: pl.BlockSpec((1, r, w), lambda i, j: (i, 0, 0))
    heads = pl.BlockSpec((1, A_HEADS, tile, LANES), lambda i, j: (i, 0, j, 0))
    names = ["g_attn", "w_in", "b_f", "g_q", "g_k", "g_qm", "bd512", "bd256", "pfq", "pfk", "cq", "ck", "conv_w", "conv_b"]
    consts = [wts[k] for k in names]
    out_shape = [
        jax.ShapeDtypeStruct((b, s, A_WIDTH), F32),
        jax.ShapeDtypeStruct((b, s, A_WIDTH), F32),
        jax.ShapeDtypeStruct((b, s, A_HEADS), F32),
        jax.ShapeDtypeStruct((b, A_HEADS, s, LANES), BF16),
        jax.ShapeDtypeStruct((b, A_HEADS, s, LANES), BF16),
        jax.ShapeDtypeStruct((b, s, A_WIDTH), BF16),
        jax.ShapeDtypeStruct((b, s, B_WIDTH), BF16),
        jax.ShapeDtypeStruct((b, s, M_WIDTH), BF16),
        jax.ShapeDtypeStruct((b, 2, B_WIDTH), F32),
    ]
    out_specs = [seq(A_WIDTH), seq(A_WIDTH), seq(A_HEADS), heads, heads, seq(A_WIDTH), seq(B_WIDTH), seq(M_WIDTH),
                 per_b(2, B_WIDTH)]
    return pl.pallas_call(
        _proj_body,
        grid=(b, nt),
        in_specs=[seq(d), per_b(2, B_WIDTH), per_b(n_mem, M_WIDTH), per_b(n_mem, M_WIDTH)] + [full(c) for c in consts],
        out_specs=out_specs,
        out_shape=out_shape,
        scratch_shapes=[pltpu.VMEM((SUBLANES, LANES), F32), pltpu.VMEM((SUBLANES, B_WIDTH), F32)],
        compiler_params=pltpu.CompilerParams(dimension_semantics=("arbitrary", "arbitrary"), vmem_limit_bytes=VMEM_LIMIT),
        name="proj",
    )(x, prev, mk, mv, *consts)


def _fox_prompt_body(q_ref, k_ref, v_ref, o_ref, *, blk):
    qi = pl.program_id(2)
    ones = jnp.ones((blk, LANES), BF16)
    row = lax.broadcasted_iota(I32, (blk, blk), 0)
    col = lax.broadcasted_iota(I32, (blk, blk), 1)
    outs = []
    for sub in range(2):
        q = q_ref[0, sub]

        def step(kj, carry, masked):
            m, acc = carry
            off = pl.multiple_of(kj * blk, blk)
            s = _dot_nt(q, k_ref[0, sub, pl.ds(off, blk), :])
            if masked:
                s = jnp.where(col <= row, s, NEG_BIG)
            m_new = jnp.maximum(m, jnp.max(s, axis=-1, keepdims=True))
            p = jnp.exp(s - m_new).astype(BF16)
            vv = jnp.concatenate([v_ref[0, pl.ds(off, blk), :], ones], axis=1)
            return m_new, jnp.exp(m - m_new) * acc + _dot(p, vv)

        init = (jnp.full((blk, 1), NEG_BIG, F32), jnp.zeros((blk, 2 * LANES), F32))
        carry = lax.fori_loop(0, qi, functools.partial(step, masked=False), init)
        _, acc = step(qi, carry, True)
        outs.append(acc[:, :LANES] / acc[:, LANES:])
    o_ref[0] = jnp.where(_lane((blk, LANES)) < HEAD_DIM, outs[0], outs[1]).astype(BF16)


def _fox_prompt_call(qaug, kaug, vb, blk):
    b, _, s, _ = qaug.shape
    return pl.pallas_call(
        functools.partial(_fox_prompt_body, blk=blk),
        grid=(b, A_HEADS // 2, s // blk),
        in_specs=[pl.BlockSpec((1, 2, blk, LANES), lambda i, hp, j: (i, hp, j, 0)),
                  pl.BlockSpec((1, 2, s, LANES), lambda i, hp, j: (i, hp, 0, 0)),
                  pl.BlockSpec((1, s, LANES), lambda i, hp, j: (i, 0, hp))],
        out_specs=pl.BlockSpec((1, blk, LANES), lambda i, hp, j: (i, j, hp)),
        out_shape=jax.ShapeDtypeStruct((b, s, A_WIDTH), BF16),
        compiler_params=pltpu.CompilerParams(dimension_semantics=("arbitrary",) * 3, vmem_limit_bytes=VMEM_LIMIT),
        name="fox_prompt",
    )(qaug, kaug, vb)


def _fox_sample_body(ck_ref, cv_ref, clf_ref, q_ref, k_ref, v_ref, pfk_ref, ckc_ref, o_ref):
    past = ck_ref.shape[1]
    ts = q_ref.shape[2]
    row = lax.broadcasted_iota(I32, (past, past), 0)
    col = lax.broadcasted_iota(I32, (past, past), 1)
    tri = jnp.where(col > row, 1.0, 0.0).astype(BF16)
    lh, lm, ll = _split3(clf_ref[0])
    suffix = _dot(tri, lh) + _dot(tri, lm) + _dot(tri, ll)
    fak = _dot(jnp.concatenate(_split3(-suffix), axis=1), pfk_ref[...]) + ckc_ref[...]
    lane = _lane((past, LANES))
    lane_s = _lane((ts, LANES))
    causal = lax.broadcasted_iota(I32, (ts, ts), 1) <= lax.broadcasted_iota(I32, (ts, ts), 0)
    outs = []
    for pr in range(A_HEADS // 2):
        kc2 = ck_ref[0, :, LANES * pr:LANES * (pr + 1)]
        vc = cv_ref[0, :, LANES * pr:LANES * (pr + 1)].astype(BF16)
        vn = v_ref[0, :, LANES * pr:LANES * (pr + 1)]
        o = []
        for sub in range(2):
            hd = 2 * pr + sub
            kt = pltpu.roll(kc2, HEAD_DIM, 1) if sub else kc2
            kc = jnp.where(lane < HEAD_DIM, kt, fak[:, LANES * hd:LANES * (hd + 1)]).astype(BF16)
            q = q_ref[0, hd]
            s1 = _dot_nt(q, kc)
            s2 = jnp.where(causal, _dot_nt(q, k_ref[0, hd]), NEG_BIG)
            m = jnp.maximum(jnp.max(s1, axis=-1, keepdims=True), jnp.max(s2, axis=-1, keepdims=True))
            p1, p2 = jnp.exp(s1 - m), jnp.exp(s2 - m)
            den = jnp.sum(p1, axis=-1, keepdims=True) + jnp.sum(p2, axis=-1, keepdims=True)
            o.append((_dot(p1.astype(BF16), vc) + _dot(p2.astype(BF16), vn)) / den)
        outs.append(jnp.where(lane_s < HEAD_DIM, o[0], o[1]))
    o_ref[0] = jnp.concatenate(outs, axis=1).astype(BF16)


def _fox_sample_call(cache_k, cache_v, cache_lf, qaug, kaug, vb, pfk, ck):
    b, past, _ = cache_k.shape
    ts = qaug.shape[2]
    full = lambda a: pl.BlockSpec(a.shape, lambda i: (0,) * a.ndim)
    return pl.pallas_call(
        _fox_sample_body,
        grid=(b,),
        in_specs=[pl.BlockSpec((1, past, A_WIDTH), lambda i: (i, 0, 0)),
                  pl.BlockSpec((1, past, A_WIDTH), lambda i: (i, 0, 0)),
                  pl.BlockSpec((1, past, LANES), lambda i: (i, 0, 0)),
                  pl.BlockSpec((1, A_HEADS, ts, LANES), lambda i: (i, 0, 0, 0)),
                  pl.BlockSpec((1, A_HEADS, ts, LANES), lambda i: (i, 0, 0, 0)),
                  pl.BlockSpec((1, ts, A_WIDTH), lambda i: (i, 0, 0)),
                  full(pfk), full(ck)],
        out_specs=pl.BlockSpec((1, ts, A_WIDTH), lambda i: (i, 0, 0)),
        out_shape=jax.ShapeDtypeStruct((b, ts, A_WIDTH), BF16),
        compiler_params=pltpu.CompilerParams(dimension_semantics=("arbitrary",), vmem_limit_bytes=VMEM_LIMIT),
        name="fox_sample",
    )(cache_k, cache_v, cache_lf, qaug, kaug, vb, pfk, ck)


def _top_rows(vals, payload, k):
    r, t = vals.shape
    iota = lax.broadcasted_iota(I32, (r, t), 0)
    slot = lax.broadcasted_iota(I32, (k, t), 0)
    top_v = jnp.zeros((k, t), F32)
    top_p = jnp.zeros((k, t), I32)
    for j in range(k):
        m = jnp.max(vals, axis=0, keepdims=True)
        sel = jnp.min(jnp.where(vals == m, iota, r), axis=0, keepdims=True)
        hit = iota == sel
        pay = sel if payload is None else jnp.max(jnp.where(hit, payload, -1), axis=0, keepdims=True)
        top_v = jnp.where(slot == j, m, top_v)
        top_p = jnp.where(slot == j, pay, top_p)
        vals = jnp.where(hit, -jnp.inf, vals)
    return top_v, top_p


def _route_body(x_ref, oa_ref, ob_ref, om_ref, wo_ref, gf_ref, wq_ref, keys_ref,
                x1_ref, xn_ref, e_ref, g_ref, qp_scr, e_scr, g_scr):
    y = x_ref[...] + _dot(oa_ref[...], wo_ref[0:A_WIDTH, :]) \
        + _dot(ob_ref[...], wo_ref[A_WIDTH:A_WIDTH + B_WIDTH, :]) \
        + _dot(om_ref[...], wo_ref[A_WIDTH + B_WIDTH:, :])
    x1_ref[...] = y
    xn = y * lax.rsqrt(jnp.mean(y * y, axis=-1, keepdims=True) + EPS) * gf_ref[...]
    xn_ref[...] = xn
    qp = _dot(xn.astype(BF16), wq_ref[...])
    for hd in range(PEER_HEADS):
        qp_scr[hd] = qp[:, LANES * hd:LANES * (hd + 1)].astype(BF16)

    def head(hd, carry):
        sc = _dot_nt(keys_ref[hd], qp_scr[hd])
        s1, i1 = _top_rows(sc[:PEER_KEYS], None, PEER_TOPK)
        s2, i2 = _top_rows(sc[PEER_KEYS:], None, PEER_TOPK)
        cand = jnp.concatenate([s1[i:i + 1, :] + s2 for i in range(PEER_TOPK)], axis=0)
        expert = jnp.concatenate([i1[i:i + 1, :] * PEER_KEYS + i2 for i in range(PEER_TOPK)], axis=0)
        top, e = _top_rows(cand, expert, PEER_TOPK)
        p = jnp.exp(top - top[0:1, :])
        g_scr[hd] = p / jnp.sum(p, axis=0, keepdims=True)
        e_scr[hd] = e * ROW_SUB
        return carry

    lax.fori_loop(0, PEER_HEADS, head, 0)
    e_ref[...] = jnp.concatenate([e_scr[hd] for hd in range(PEER_HEADS)], axis=0).T
    g_ref[...] = jnp.concatenate([g_scr[hd] for hd in range(PEER_HEADS)], axis=0).T


def _route_call(x, oa, ob, om, wts, tile):
    n, d = x.shape
    full = lambda a: pl.BlockSpec(a.shape, lambda i: (0,) * a.ndim)
    rows = lambda w: pl.BlockSpec((tile, w), lambda i: (i, 0))
    consts = [wts[k] for k in ["w_out", "g_ffn", "w_pq", "keys"]]
    return pl.pallas_call(
        _route_body,
        grid=(n // tile,),
        in_specs=[rows(d), rows(A_WIDTH), rows(B_WIDTH), rows(M_WIDTH)] + [full(c) for c in consts],
        out_specs=[rows(d), rows(d), rows(PEER_SLOTS), rows(PEER_SLOTS)],
        out_shape=[jax.ShapeDtypeStruct((n, d), F32), jax.ShapeDtypeStruct((n, d), F32),
                   jax.ShapeDtypeStruct((n, PEER_SLOTS), I32), jax.ShapeDtypeStruct((n, PEER_SLOTS), F32)],
        scratch_shapes=[pltpu.VMEM((PEER_HEADS, tile, LANES), BF16),
                        pltpu.VMEM((PEER_HEADS, PEER_TOPK, tile), I32),
                        pltpu.VMEM((PEER_HEADS, PEER_TOPK, tile), F32)],
        compiler_params=pltpu.CompilerParams(dimension_semantics=("arbitrary",), vmem_limit_bytes=VMEM_LIMIT),
        name="merge_route",
    )(x, oa, ob, om, *consts)


GROUP_ROWS = PEER_SLOTS * ROW_SUB
WIDE = PEER_SLOTS * SUBLANES
PIPE_TOKENS = 16


def _gather_token(e_ref, tab_ref, buf, t):
    for k in range(PEER_SLOTS):
        buf[k * ROW_SUB:(k + 1) * ROW_SUB, :] = tab_ref[pl.ds(pl.multiple_of(e_ref[t, k], ROW_SUB), ROW_SUB), :]


def _pipelined_tokens(e_ref, tab_ref, bufs, compute, tile):
    last = tile - 1
    _gather_token(e_ref, tab_ref, bufs[0], 0)
    _gather_token(e_ref, tab_ref, bufs[1], 1)

    def step(i, carry):
        for half in range(PIPE_TOKENS // 2):
            cur = bufs[2 * (half % 2):2 * (half % 2) + 2]
            nxt = bufs[2 - 2 * (half % 2):4 - 2 * (half % 2)]
            base = PIPE_TOKENS * i + 2 * half
            for j in range(2):
                compute(base + j, cur[j])
            for j in range(2):
                _gather_token(e_ref, tab_ref, nxt[j], jnp.minimum(base + 2 + j, last))
        return carry

    lax.fori_loop(0, tile // PIPE_TOKENS, step, 0)


def _token_tile_rows(t):
    return pl.ds(pl.multiple_of(t * SUBLANES, SUBLANES), SUBLANES)


def _split2(x):
    hi = x.astype(BF16).astype(F32)
    return jnp.concatenate([hi, x - hi], axis=0).astype(BF16)


def _peer_u_body(e_ref, x_ref, g_ref, u_ref, col_ref, mask_ref, w_ref, b0, b1, b2, b3, z_scr, *, tile, chunk):
    mask = mask_ref[...]

    def compute(t, buf):
        z = _dot_nt(_split2(x_ref[_token_tile_rows(t), :]), pltpu.bitcast(buf[...], BF16))
        z_scr[_token_tile_rows(t), :] = (z[:SUBLANES] + z[SUBLANES:]) * mask

    _pipelined_tokens(e_ref, u_ref, (b0, b1, b2, b3), compute, tile)
    for c in range(tile // chunk):
        zs = z_scr[c * chunk * SUBLANES:(c + 1) * chunk * SUBLANES, :]
        zh = zs.astype(BF16)
        zl = (zs - zh.astype(F32)).astype(BF16)
        part = _dot(zh, col_ref[...]) + _dot(zl, col_ref[...])
        a = jnp.sum(part.reshape(chunk, SUBLANES, PEER_SLOTS), axis=1)
        rows = slice(c * chunk, (c + 1) * chunk)
        w_ref[rows, :] = g_ref[rows, :] * (0.5 * a * (1.0 + lax.erf(a * np.float32(np.sqrt(0.5)))))


def _peer_u_call(e, xn8, gates, u_packed, consts, tile):
    n = e.shape[0]
    full = lambda a: pl.BlockSpec(a.shape, lambda i: (0,) * a.ndim)
    return pl.pallas_call(
        functools.partial(_peer_u_body, tile=tile, chunk=min(tile, 32)),
        grid=(n // tile,),
        in_specs=[pl.BlockSpec((tile, PEER_SLOTS), lambda i: (i, 0), memory_space=pltpu.SMEM),
                  pl.BlockSpec((tile * SUBLANES, LANES), lambda i: (i, 0)),
                  pl.BlockSpec((tile, PEER_SLOTS), lambda i: (i, 0)),
                  pl.BlockSpec(memory_space=pltpu.VMEM),
                  full(consts["collapse"]), full(consts["mask8"])],
        out_specs=pl.BlockSpec((tile, PEER_SLOTS), lambda i: (i, 0)),
        out_shape=jax.ShapeDtypeStruct((n, PEER_SLOTS), F32),
        scratch_shapes=[pltpu.VMEM((GROUP_ROWS, LANES), I32)] * 4 + [
                        pltpu.VMEM((tile * SUBLANES, WIDE), F32)],
        compiler_params=pltpu.CompilerParams(dimension_semantics=("arbitrary",), vmem_limit_bytes=VMEM_LIMIT),
        name="peer_u",
    )(e, xn8, gates, u_packed, consts["collapse"], consts["mask8"])


def _peer_v_body(e_ref, w_ref, x_ref, v_ref, exp_ref, mask_ref, y_ref, b0, b1, b2, b3, eh_scr, el_scr, *, tile):
    mask = mask_ref[...]
    w = w_ref[...]
    wh = w.astype(BF16)
    eh_scr[...] = _dot(wh, exp_ref[...])
    el_scr[...] = _dot((w - wh.astype(F32)).astype(BF16), exp_ref[...])

    def compute(t, buf):
        row = pl.ds(t, 1)
        lhs = jnp.concatenate([jnp.broadcast_to(eh_scr[row, :], (SUBLANES, WIDE)) * mask,
                               jnp.broadcast_to(el_scr[row, :], (SUBLANES, WIDE)) * mask], axis=0).astype(BF16)
        o = _dot(lhs, pltpu.bitcast(buf[...], BF16))
        rows = _token_tile_rows(t)
        y_ref[rows, :] = x_ref[rows, :] + o[:SUBLANES] + o[SUBLANES:]

    _pipelined_tokens(e_ref, v_ref, (b0, b1, b2, b3), compute, tile)


def _peer_v_call(e, w, x8, v_packed, consts, tile):
    n = e.shape[0]
    full = lambda a: pl.BlockSpec(a.shape, lambda i: (0,) * a.ndim)
    return pl.pallas_call(
        functools.partial(_peer_v_body, tile=tile),
        grid=(n // tile,),
        in_specs=[pl.BlockSpec((tile, PEER_SLOTS), lambda i: (i, 0), memory_space=pltpu.SMEM),
                  pl.BlockSpec((tile, PEER_SLOTS), lambda i: (i, 0)),
                  pl.BlockSpec((tile * SUBLANES, LANES), lambda i: (i, 0)),
                  pl.BlockSpec(memory_space=pltpu.VMEM),
                  full(consts["expand"]), full(consts["mask8"])],
        out_specs=pl.BlockSpec((tile * SUBLANES, LANES), lambda i: (i, 0)),
        out_shape=jax.ShapeDtypeStruct((n * SUBLANES, LANES), F32),
        scratch_shapes=[pltpu.VMEM((GROUP_ROWS, LANES), I32)] * 4 + [
                        pltpu.VMEM((tile, WIDE), F32), pltpu.VMEM((tile, WIDE), F32)],
        compiler_params=pltpu.CompilerParams(dimension_semantics=("arbitrary",), vmem_limit_bytes=VMEM_LIMIT),
        name="peer_v",
    )(e, w, x8, v_packed, consts["expand"], consts["mask8"])


def _peer_constants():
    lane = np.arange(WIDE)
    expand = (lane[None, :] // SUBLANES == np.arange(PEER_SLOTS)[:, None]).astype(np.float32)
    mask8 = (lane[None, :] % SUBLANES == np.arange(SUBLANES)[:, None]).astype(np.float32)
    return {"expand": jnp.asarray(expand, BF16), "collapse": jnp.asarray(expand.T, BF16), "mask8": jnp.asarray(mask8)}


def _pack_table(tab):
    n = tab.shape[0]
    b = lax.bitcast_convert_type(tab.astype(BF16), jnp.uint16).astype(jnp.uint32).reshape(n, ROW_SUB, 2, LANES)
    packed = b[:, :, 0, :] | (b[:, :, 1, :] << 16)
    return lax.bitcast_convert_type(packed, I32).reshape(n * ROW_SUB, LANES)


def _block_mean(width):
    blk = np.arange(width) // HEAD_DIM
    return jnp.asarray((blk[:, None] == blk[None, :]) / HEAD_DIM, BF16)


def _forget_placement():
    pfq = np.zeros((3 * LANES, A_HEADS * LANES), np.float32)
    pfk = np.zeros((3 * LANES, A_HEADS * LANES), np.float32)
    cq = np.zeros((1, A_HEADS * LANES), np.float32)
    ck = np.zeros((1, A_HEADS * LANES), np.float32)
    for hd in range(A_HEADS):
        base = hd * LANES + HEAD_DIM
        for piece in range(3):
            pfq[piece * LANES + hd, base + piece] = 1.0
            pfk[piece * LANES + hd, base + 3 + piece] = -1.0
            cq[0, base + 3 + piece] = 1.0
            ck[0, base + piece] = 1.0
    return jnp.asarray(pfq, BF16), jnp.asarray(pfk, BF16), jnp.asarray(cq), jnp.asarray(ck)


def _layer_weights(l, g_attn, w_in, b_f, g_q_a, g_k_a, conv_w, conv_b, g_q_m, w_out, g_ffn, w_peer_q, peer_keys):
    scale = HEAD_DIM ** -0.5
    splits = np.cumsum([A_WIDTH, A_WIDTH, A_WIDTH, A_HEADS, B_WIDTH, B_WIDTH, B_WIDTH, M_WIDTH])
    wi = w_in[l]
    qa, ka, va, fl, bg, cg, hv, qm = [wi[:, a:b] for a, b in zip(np.r_[0, splits[:-1]], splits)]
    w_packed = jnp.concatenate([qa, ka, va, bg, cg, hv, qm, fl, jnp.zeros((wi.shape[0], LANES - A_HEADS), wi.dtype)],
                               axis=1).astype(BF16)
    pfq, pfk, cq, ck = _forget_placement()
    keys = peer_keys[l]
    zeros = jnp.zeros_like(keys[:, 0])
    keys2 = jnp.concatenate([jnp.concatenate([keys[:, 0], zeros], axis=-1),
                             jnp.concatenate([zeros, keys[:, 1]], axis=-1)], axis=1).astype(BF16)
    return {
        "g_attn": g_attn[l][None, :], "w_in": w_packed,
        "b_f": jnp.pad(b_f[l], (0, LANES - A_HEADS))[None, :],
        "g_q": (jnp.tile(g_q_a[l], A_HEADS) * scale)[None, :], "g_k": jnp.tile(g_k_a[l], A_HEADS)[None, :],
        "g_qm": (jnp.tile(g_q_m[l], M_HEADS) * scale)[None, :],
        "bd512": _block_mean(A_WIDTH), "bd256": _block_mean(M_WIDTH),
        "pfq": pfq, "pfk": pfk, "cq": cq, "ck": ck,
        "conv_w": conv_w[l], "conv_b": conv_b[l][None, :],
        "w_out": w_out[l].astype(BF16), "g_ffn": g_ffn[l][None, :], "w_pq": w_peer_q[l].astype(BF16), "keys": keys2,
    }


def _peer_and_merge(x, oa, ob, om, wts, u_packed, v_packed, route_tile, peer_tile):
    b, s, d = x.shape
    n = b * s
    x1, xn, e, gates = _route_call(x.reshape(n, d), oa.reshape(n, -1), ob.reshape(n, -1), om.reshape(n, -1), wts, route_tile)
    consts = _peer_constants()
    w = _peer_u_call(e, xn.reshape(n * SUBLANES, LANES), gates, u_packed, consts, peer_tile)
    y8 = _peer_v_call(e, w, x1.reshape(n * SUBLANES, LANES), v_packed, consts, peer_tile)
    return y8.reshape(b, s, d)


def kernel(x_prompt, x_sample, cache_a_k, cache_a_v, cache_a_logf, cache_b_conv, cache_m_k, cache_m_v, mem_prompt, g_attn, w_in, b_f, g_q_a, g_k_a, conv_w, conv_b, g_mem, w_mem_k, w_mem_v, g_k_m, g_q_m, w_out, g_ffn, w_peer_q, peer_keys, peer_u, peer_v):
    depth = w_in.shape[0]
    xp, xs = x_prompt, x_sample
    bp, sp, _ = xp.shape
    bs, ts, _ = xs.shape
    outs = [[] for _ in range(10)]
    for l in range(depth):
        wts = _layer_weights(l, g_attn, w_in, b_f, g_q_a, g_k_a, conv_w, conv_b, g_q_m, w_out, g_ffn, w_peer_q, peer_keys)
        u_packed, v_packed = _pack_table(peer_u[l]), _pack_table(peer_v[l])
        fox_blk = min(512, sp)
        route_tile = 256
        peer_tile = 128

        mk, mv = _memkv_call(mem_prompt, g_mem[l][None, :], w_mem_k[l].astype(BF16), w_mem_v[l].astype(BF16),
                             jnp.tile(g_k_m[l], M_HEADS)[None, :], wts["bd256"])
        ka, va, logf, qaug, kaug, vb, ob, om, cst = _proj_call(
            xp, jnp.zeros((bp, 2, B_WIDTH), F32), mk, mv, wts, min(512, sp))
        oa = _fox_prompt_call(qaug, kaug, vb, fox_blk)
        xp = _peer_and_merge(xp, oa, ob, om, wts, u_packed, v_packed, route_tile, peer_tile)
        n_mem = mk.shape[1]
        for dst, val in zip(outs[:6], [ka.reshape(bp, sp, A_HEADS, HEAD_DIM), va.reshape(bp, sp, A_HEADS, HEAD_DIM), logf, cst,
                                       mk.reshape(bp, n_mem, M_HEADS, HEAD_DIM), mv.reshape(bp, n_mem, M_HEADS, HEAD_DIM)]):
            dst.append(val)

        past = cache_a_k.shape[2]
        ka, va, logf, qaug, kaug, vb, ob, om, cst = _proj_call(
            xs, cache_b_conv[l], cache_m_k[l].reshape(bs, -1, M_WIDTH), cache_m_v[l].reshape(bs, -1, M_WIDTH), wts, ts)
        clf = jnp.pad(cache_a_logf[l], ((0, 0), (0, 0), (0, LANES - A_HEADS)))
        oa = _fox_sample_call(cache_a_k[l].reshape(bs, past, A_WIDTH), cache_a_v[l].reshape(bs, past, A_WIDTH), clf,
                              qaug, kaug, vb, wts["pfk"], wts["ck"])
        xs = _peer_and_merge(xs, oa, ob, om, wts, u_packed, v_packed, route_tile, peer_tile)
        for dst, val in zip(outs[6:], [ka.reshape(bs, ts, A_HEADS, HEAD_DIM), va.reshape(bs, ts, A_HEADS, HEAD_DIM), logf, cst]):
            dst.append(val)

    return (xp, xs) + tuple(jnp.stack(o) for o in outs)
```

```python
import functools

import numpy as np
import jax
import jax.numpy as jnp
from jax import lax
from jax.experimental import pallas as pl
from jax.experimental.pallas import tpu as pltpu

F32, BF16, I32 = jnp.float32, jnp.bfloat16, jnp.int32
EPS = 1e-6
LANES = 128
SUBLANES = 8
HEAD_DIM = 64
A_HEADS = 8
A_WIDTH = A_HEADS * HEAD_DIM
B_WIDTH = 256
M_HEADS = 4
M_WIDTH = M_HEADS * HEAD_DIM
PEER_HEADS = 8
PEER_KEYS = 128
PEER_TOPK = 16
PEER_SLOTS = PEER_HEADS * PEER_TOPK
D_MODEL = 1024
ROW_WORDS = D_MODEL // 2
ROW_SUB = ROW_WORDS // LANES
NEG_BIG = -1e30
VMEM_LIMIT = 56 * 1024 * 1024

C_Q, C_K, C_V, C_BG, C_CG, C_HV, C_QM, C_FL, C_END = 0, 512, 1024, 1536, 1792, 2048, 2304, 2560, 2688


def _dot(a, b):
    return jnp.dot(a, b, preferred_element_type=F32)


def _dot_nt(a, b):
    return lax.dot_general(a, b, (((1,), (1,)), ((), ())), preferred_element_type=F32)


def _split3(x):
    hi = x.astype(BF16)
    r1 = x - hi.astype(F32)
    mid = r1.astype(BF16)
    lo = (r1 - mid.astype(F32)).astype(BF16)
    return hi, mid, lo


def _lane(shape):
    return lax.broadcasted_iota(I32, shape, len(shape) - 1)


def _memkv_body(mem_ref, g_ref, wk_ref, wv_ref, gk_ref, bd_ref, mk_ref, mv_ref):
    x = mem_ref[0]
    h = (x * lax.rsqrt(jnp.mean(x * x, axis=-1, keepdims=True) + EPS) * g_ref[...]).astype(BF16)
    zk = _dot(h, wk_ref[...])
    ms = _dot((zk * zk).astype(BF16), bd_ref[...])
    mk_ref[0] = zk * lax.rsqrt(ms + EPS) * gk_ref[...]
    mv_ref[0] = _dot(h, wv_ref[...])


def _memkv_call(mem, g_mem, w_mk, w_mv, gk_t, bd256):
    b, n_mem, d = mem.shape
    full = lambda shape: pl.BlockSpec(shape, lambda i: (0,) * len(shape))
    return pl.pallas_call(
        _memkv_body,
        grid=(b,),
        in_specs=[pl.BlockSpec((1, n_mem, d), lambda i: (i, 0, 0)), full((1, d)), full((d, M_WIDTH)),
                  full((d, M_WIDTH)), full((1, M_WIDTH)), full((M_WIDTH, M_WIDTH))],
        out_specs=[pl.BlockSpec((1, n_mem, M_WIDTH), lambda i: (i, 0, 0))] * 2,
        out_shape=[jax.ShapeDtypeStruct((b, n_mem, M_WIDTH), F32)] * 2,
        compiler_params=pltpu.CompilerParams(dimension_semantics=("arbitrary",), vmem_limit_bytes=VMEM_LIMIT),
        name="mem_kv",
    )(mem, g_mem, w_mk, w_mv, gk_t, bd256)


def _proj_body(x_ref, prev_ref, mk_ref, mv_ref, gat_ref, w_ref, bf_ref, gq_ref, gk_ref, gqm_ref,
               bd512_ref, bd256_ref, pfq_ref, pfk_ref, cq_ref, ck_ref, cw_ref, cb_ref,
               ka_ref, va_ref, logf_ref, qaug_ref, kaug_ref, vb_ref, ob_ref, om_ref, cst_ref,
               fcarry, ucarry):
    t = pl.program_id(1)
    tt = x_ref.shape[1]
    x = x_ref[0]
    h = (x * lax.rsqrt(jnp.mean(x * x, axis=-1, keepdims=True) + EPS) * gat_ref[...]).astype(BF16)

    def proj(c0, c1):
        return _dot(h, w_ref[:, c0:c1])

    @pl.when(t == 0)
    def _():
        fcarry[...] = jnp.zeros_like(fcarry)
        ucarry[...] = jnp.zeros_like(ucarry)
        ucarry[SUBLANES - 2:SUBLANES, :] = prev_ref[0]

    v = proj(C_FL, C_END) + bf_ref[...]
    logf = jnp.minimum(v, 0.0) - jnp.log1p(jnp.exp(-jnp.abs(v)))
    logf = jnp.where(_lane(logf.shape) < A_HEADS, logf, 0.0)
    logf_ref[0] = logf[:, :A_HEADS]
    row = lax.broadcasted_iota(I32, (tt, tt), 0)
    col = lax.broadcasted_iota(I32, (tt, tt), 1)
    tri = jnp.where(row >= col, 1.0, 0.0).astype(BF16)
    lh, lm, ll = _split3(logf)
    fcum = _dot(tri, lh) + _dot(tri, lm) + _dot(tri, ll) + fcarry[0:1, :]
    fcarry[...] = jnp.broadcast_to(fcum[tt - 1:tt, :], fcarry.shape)
    fparts = jnp.concatenate(_split3(fcum), axis=1)
    faq = _dot(fparts, pfq_ref[...]) + cq_ref[...]
    fak = _dot(fparts, pfk_ref[...]) + ck_ref[...]

    lane = _lane((tt, LANES))
    zq = proj(C_Q, C_K)
    qn = zq * lax.rsqrt(_dot((zq * zq).astype(BF16), bd512_ref[...]) + EPS) * gq_ref[...]
    zk = proj(C_K, C_V)
    kn = zk * lax.rsqrt(_dot((zk * zk).astype(BF16), bd512_ref[...]) + EPS) * gk_ref[...]
    ka_ref[0] = kn
    for hd in range(A_HEADS):
        c0 = LANES * (hd // 2)
        qt, kt = qn[:, c0:c0 + LANES], kn[:, c0:c0 + LANES]
        if hd % 2:
            qt, kt = pltpu.roll(qt, HEAD_DIM, 1), pltpu.roll(kt, HEAD_DIM, 1)
        qaug_ref[0, hd] = jnp.where(lane < HEAD_DIM, qt, faq[:, LANES * hd:LANES * (hd + 1)]).astype(BF16)
        kaug_ref[0, hd] = jnp.where(lane < HEAD_DIM, kt, fak[:, LANES * hd:LANES * (hd + 1)]).astype(BF16)
    zv = proj(C_V, C_BG)
    va_ref[0] = zv
    vb_ref[0] = zv.astype(BF16)

    u = proj(C_CG, C_HV) * proj(C_HV, C_QM)
    rows = lax.broadcasted_iota(I32, u.shape, 0)
    p1 = ucarry[SUBLANES - 1:SUBLANES, :]
    p2 = ucarry[SUBLANES - 2:SUBLANES - 1, :]
    u1 = jnp.where(rows == 0, p1, pltpu.roll(u, 1, 0))
    u2 = jnp.where(rows == 0, p2, jnp.where(rows == 1, p1, pltpu.roll(u, 2, 0)))
    cy = cb_ref[...] + cw_ref[2:3, :] * u + cw_ref[0:1, :] * u2 + cw_ref[1:2, :] * u1
    ob_ref[0] = (proj(C_BG, C_CG) * cy).astype(BF16)
    ucarry[...] = u[tt - SUBLANES:tt, :]
    cst_ref[0] = u[tt - 2:tt, :]

    zm = proj(C_QM, C_FL)
    qm = zm * lax.rsqrt(_dot((zm * zm).astype(BF16), bd256_ref[...]) + EPS) * gqm_ref[...]
    mkb = mk_ref[0].astype(BF16)
    mvb = mv_ref[0].astype(BF16)
    outs = []
    for pr in range(M_HEADS // 2):
        qp = qm[:, LANES * pr:LANES * (pr + 1)]
        kp = mkb[:, LANES * pr:LANES * (pr + 1)]
        vp = mvb[:, LANES * pr:LANES * (pr + 1)]
        o = []
        for sub in range(2):
            keep = (lane < HEAD_DIM) if sub == 0 else (lane >= HEAD_DIM)
            s = _dot_nt(jnp.where(keep, qp, 0.0).astype(BF16), kp)
            p = jnp.exp(s - jnp.max(s, axis=-1, keepdims=True))
            o.append(_dot(p.astype(BF16), vp) / jnp.sum(p, axis=-1, keepdims=True))
        outs.append(jnp.where(lane < HEAD_DIM, o[0], o[1]))
    om_ref[0] = jnp.concatenate(outs, axis=1).astype(BF16)


def _proj_call(x, prev, mk, mv, wts, tile):
    b, s, d = x.shape
    nt = s // tile
    n_mem = mk.shape[1]
    full = lambda a: pl.BlockSpec(a.shape, lambda i, j: (0,) * a.ndim)
    seq = lambda w: pl.BlockSpec((1, tile, w), lambda i, j: (i, j, 0))
    per_b = lambda r, w: pl.BlockSpec((1, r, w), lambda i, j: (i, 0, 0))
    heads = pl.BlockSpec((1, A_HEADS, tile, LANES), lambda i, j: (i, 0, j, 0))
    names = ["g_attn", "w_in", "b_f", "g_q", "g_k", "g_qm", "bd512", "bd256", "pfq", "pfk", "cq", "ck", "conv_w", "conv_b"]
    consts = [wts[k] for k in names]
    out_shape = [
        jax.ShapeDtypeStruct((b, s, A_WIDTH), F32),
        jax.ShapeDtypeStruct((b, s, A_WIDTH), F32),
        jax.ShapeDtypeStruct((b, s, A_HEADS), F32),
        jax.ShapeDtypeStruct((b, A_HEADS, s, LANES), BF16),
        jax.ShapeDtypeStruct((b, A_HEADS, s, LANES), BF16),
        jax.ShapeDtypeStruct((b, s, A_WIDTH), BF16),
        jax.ShapeDtypeStruct((b, s, B_WIDTH), BF16),
        jax.ShapeDtypeStruct((b, s, M_WIDTH), BF16),
        jax.ShapeDtypeStruct((b, 2, B_WIDTH), F32),
    ]
    out_specs = [seq(A_WIDTH), seq(A_WIDTH), seq(A_HEADS), heads, heads, seq(A_WIDTH), seq(B_WIDTH), seq(M_WIDTH),
                 per_b(2, B_WIDTH)]
    return pl.pallas_call(
        _proj_body,
        grid=(b, nt),
        in_specs=[seq(d), per_b(2, B_WIDTH), per_b(n_mem, M_WIDTH), per_b(n_mem, M_WIDTH)] + [full(c) for c in consts],
        out_specs=out_specs,
        out_shape=out_shape,
        scratch_shapes=[pltpu.VMEM((SUBLANES, LANES), F32), pltpu.VMEM((SUBLANES, B_WIDTH), F32)],
        compiler_params=pltpu.CompilerParams(dimension_semantics=("arbitrary", "arbitrary"), vmem_limit_bytes=VMEM_LIMIT),
        name="proj",
    )(x, prev, mk, mv, *consts)


def _fox_prompt_body(q_ref, k_ref, v_ref, o_ref, *, blk):
    qi = pl.program_id(2)
    ones = jnp.ones((blk, LANES), BF16)
    row = lax.broadcasted_iota(I32, (blk, blk), 0)
    col = lax.broadcasted_iota(I32, (blk, blk), 1)
    outs = []
    for sub in range(2):
        q = q_ref[0, sub]

        def step(kj, carry, masked):
            m, acc = carry
            off = pl.multiple_of(kj * blk, blk)
            s = _dot_nt(q, k_ref[0, sub, pl.ds(off, blk), :])
            if masked:
                s = jnp.where(col <= row, s, NEG_BIG)
            m_new = jnp.maximum(m, jnp.max(s, axis=-1, keepdims=True))
            p = jnp.exp(s - m_new).astype(BF16)
            vv = jnp.concatenate([v_ref[0, pl.ds(off, blk), :], ones], axis=1)
            return m_new, jnp.exp(m - m_new) * acc + _dot(p, vv)

        init = (jnp.full((blk, 1), NEG_BIG, F32), jnp.zeros((blk, 2 * LANES), F32))
        carry = lax.fori_loop(0, qi, functools.partial(step, masked=False), init)
        _, acc = step(qi, carry, True)
        outs.append(acc[:, :LANES] / acc[:, LANES:])
    o_ref[0] = jnp.where(_lane((blk, LANES)) < HEAD_DIM, outs[0], outs[1]).astype(BF16)


def _fox_prompt_call(qaug, kaug, vb, blk):
    b, _, s, _ = qaug.shape
    return pl.pallas_call(
        functools.partial(_fox_prompt_body, blk=blk),
        grid=(b, A_HEADS // 2, s // blk),
        in_specs=[pl.BlockSpec((1, 2, blk, LANES), lambda i, hp, j: (i, hp, j, 0)),
                  pl.BlockSpec((1, 2, s, LANES), lambda i, hp, j: (i, hp, 0, 0)),
                  pl.BlockSpec((1, s, LANES), lambda i, hp, j: (i, 0, hp))],
        out_specs=pl.BlockSpec((1, blk, LANES), lambda i, hp, j: (i, j, hp)),
        out_shape=jax.ShapeDtypeStruct((b, s, A_WIDTH), BF16),
        compiler_params=pltpu.CompilerParams(dimension_semantics=("arbitrary",) * 3, vmem_limit_bytes=VMEM_LIMIT),
        name="fox_prompt",
    )(qaug, kaug, vb)


def _fox_sample_body(ck_ref, cv_ref, clf_ref, q_ref, k_ref, v_ref, pfk_ref, ckc_ref, o_ref):
    past = ck_ref.shape[1]
    ts = q_ref.shape[2]
    row = lax.broadcasted_iota(I32, (past, past), 0)
    col = lax.broadcasted_iota(I32, (past, past), 1)
    tri = jnp.where(col > row, 1.0, 0.0).astype(BF16)
    lh, lm, ll = _split3(clf_ref[0])
    suffix = _dot(tri, lh) + _dot(tri, lm) + _dot(tri, ll)
    fak = _dot(jnp.concatenate(_split3(-suffix), axis=1), pfk_ref[...]) + ckc_ref[...]
    lane = _lane((past, LANES))
    lane_s = _lane((ts, LANES))
    causal = lax.broadcasted_iota(I32, (ts, ts), 1) <= lax.broadcasted_iota(I32, (ts, ts), 0)
    outs = []
    for pr in range(A_HEADS // 2):
        kc2 = ck_ref[0, :, LANES * pr:LANES * (pr + 1)]
        vc = cv_ref[0, :, LANES * pr:LANES * (pr + 1)].astype(BF16)
        vn = v_ref[0, :, LANES * pr:LANES * (pr + 1)]
        o = []
        for sub in range(2):
            hd = 2 * pr + sub
            kt = pltpu.roll(kc2, HEAD_DIM, 1) if sub else kc2
            kc = jnp.where(lane < HEAD_DIM, kt, fak[:, LANES * hd:LANES * (hd + 1)]).astype(BF16)
            q = q_ref[0, hd]
            s1 = _dot_nt(q, kc)
            s2 = jnp.where(causal, _dot_nt(q, k_ref[0, hd]), NEG_BIG)
            m = jnp.maximum(jnp.max(s1, axis=-1, keepdims=True), jnp.max(s2, axis=-1, keepdims=True))
            p1, p2 = jnp.exp(s1 - m), jnp.exp(s2 - m)
            den = jnp.sum(p1, axis=-1, keepdims=True) + jnp.sum(p2, axis=-1, keepdims=True)
            o.append((_dot(p1.astype(BF16), vc) + _dot(p2.astype(BF16), vn)) / den)
        outs.append(jnp.where(lane_s < HEAD_DIM, o[0], o[1]))
    o_ref[0] = jnp.concatenate(outs, axis=1).astype(BF16)


def _fox_sample_call(cache_k, cache_v, cache_lf, qaug, kaug, vb, pfk, ck):
    b, past, _ = cache_k.shape
    ts = qaug.shape[2]
    full = lambda a: pl.BlockSpec(a.shape, lambda i: (0,) * a.ndim)
    return pl.pallas_call(
        _fox_sample_body,
        grid=(b,),
        in_specs=[pl.BlockSpec((1, past, A_WIDTH), lambda i: (i, 0, 0)),
                  pl.BlockSpec((1, past, A_WIDTH), lambda i: (i, 0, 0)),
                  pl.BlockSpec((1, past, LANES), lambda i: (i, 0, 0)),
                  pl.BlockSpec((1, A_HEADS, ts, LANES), lambda i: (i, 0, 0, 0)),
                  pl.BlockSpec((1, A_HEADS, ts, LANES), lambda i: (i, 0, 0, 0)),
                  pl.BlockSpec((1, ts, A_WIDTH), lambda i: (i, 0, 0)),
                  full(pfk), full(ck)],
        out_specs=pl.BlockSpec((1, ts, A_WIDTH), lambda i: (i, 0, 0)),
        out_shape=jax.ShapeDtypeStruct((b, ts, A_WIDTH), BF16),
        compiler_params=pltpu.CompilerParams(dimension_semantics=("arbitrary",), vmem_limit_bytes=VMEM_LIMIT),
        name="fox_sample",
    )(cache_k, cache_v, cache_lf, qaug, kaug, vb, pfk, ck)


def _top_rows(vals, payload, k):
    r, t = vals.shape
    iota = lax.broadcasted_iota(I32, (r, t), 0)
    slot = lax.broadcasted_iota(I32, (k, t), 0)
    top_v = jnp.zeros((k, t), F32)
    top_p = jnp.zeros((k, t), I32)
    for j in range(k):
        m = jnp.max(vals, axis=0, keepdims=True)
        sel = jnp.min(jnp.where(vals == m, iota, r), axis=0, keepdims=True)
        hit = iota == sel
        pay = sel if payload is None else jnp.max(jnp.where(hit, payload, -1), axis=0, keepdims=True)
        top_v = jnp.where(slot == j, m, top_v)
        top_p = jnp.where(slot == j, pay, top_p)
        vals = jnp.where(hit, -jnp.inf, vals)
    return top_v, top_p


def _route_body(x_ref, oa_ref, ob_ref, om_ref, wo_ref, gf_ref, wq_ref, keys_ref,
                x1_ref, xn_ref, e_ref, g_ref, qp_scr, e_scr, g_scr):
    y = x_ref[...] + _dot(oa_ref[...], wo_ref[0:A_WIDTH, :]) \
        + _dot(ob_ref[...], wo_ref[A_WIDTH:A_WIDTH + B_WIDTH, :]) \
        + _dot(om_ref[...], wo_ref[A_WIDTH + B_WIDTH:, :])
    x1_ref[...] = y
    xn = y * lax.rsqrt(jnp.mean(y * y, axis=-1, keepdims=True) + EPS) * gf_ref[...]
    xn_ref[...] = xn
    qp = _dot(xn.astype(BF16), wq_ref[...])
    for hd in range(PEER_HEADS):
        qp_scr[hd] = qp[:, LANES * hd:LANES * (hd + 1)].astype(BF16)

    def head(hd, carry):
        sc = _dot_nt(keys_ref[hd], qp_scr[hd])
        s1, i1 = _top_rows(sc[:PEER_KEYS], None, PEER_TOPK)
        s2, i2 = _top_rows(sc[PEER_KEYS:], None, PEER_TOPK)
        cand = jnp.concatenate([s1[i:i + 1, :] + s2 for i in range(PEER_TOPK)], axis=0)
        expert = jnp.concatenate([i1[i:i + 1, :] * PEER_KEYS + i2 for i in range(PEER_TOPK)], axis=0)
        top, e = _top_rows(cand, expert, PEER_TOPK)
        p = jnp.exp(top - top[0:1, :])
        g_scr[hd] = p / jnp.sum(p, axis=0, keepdims=True)
        e_scr[hd] = e * ROW_SUB
        return carry

    lax.fori_loop(0, PEER_HEADS, head, 0)
    e_ref[...] = jnp.concatenate([e_scr[hd] for hd in range(PEER_HEADS)], axis=0).T
    g_ref[...] = jnp.concatenate([g_scr[hd] for hd in range(PEER_HEADS)], axis=0).T


def _route_call(x, oa, ob, om, wts, tile):
    n, d = x.shape
    full = lambda a: pl.BlockSpec(a.shape, lambda i: (0,) * a.ndim)
    rows = lambda w: pl.BlockSpec((tile, w), lambda i: (i, 0))
    consts = [wts[k] for k in ["w_out", "g_ffn", "w_pq", "keys"]]
    return pl.pallas_call(
        _route_body,
        grid=(n // tile,),
        in_specs=[rows(d), rows(A_WIDTH), rows(B_WIDTH), rows(M_WIDTH)] + [full(c) for c in consts],
        out_specs=[rows(d), rows(d), rows(PEER_SLOTS), rows(PEER_SLOTS)],
        out_shape=[jax.ShapeDtypeStruct((n, d), F32), jax.ShapeDtypeStruct((n, d), F32),
                   jax.ShapeDtypeStruct((n, PEER_SLOTS), I32), jax.ShapeDtypeStruct((n, PEER_SLOTS), F32)],
        scratch_shapes=[pltpu.VMEM((PEER_HEADS, tile, LANES), BF16),
                        pltpu.VMEM((PEER_HEADS, PEER_TOPK, tile), I32),
                        pltpu.VMEM((PEER_HEADS, PEER_TOPK, tile), F32)],
        compiler_params=pltpu.CompilerParams(dimension_semantics=("arbitrary",), vmem_limit_bytes=VMEM_LIMIT),
        name="merge_route",
    )(x, oa, ob, om, *consts)


GROUP_ROWS = PEER_SLOTS * ROW_SUB
WIDE = PEER_SLOTS * SUBLANES
PIPE_TOKENS = 32


def _gather_token(idx_ref, row, tab_ref, buf):
    for k in range(PEER_SLOTS):
        buf[k * ROW_SUB:(k + 1) * ROW_SUB, :] = tab_ref[pl.ds(pl.multiple_of(idx_ref[row, k], ROW_SUB), ROW_SUB), :]


def _pipelined_tokens(e_hbm, tab_ref, bufs, idx, sems, compute, tile):
    blocks = tile // PIPE_TOKENS
    tile_row = pl.program_id(0) * tile

    def idx_copy(block, slot):
        rows = pl.ds(pl.multiple_of(tile_row + block * PIPE_TOKENS, PIPE_TOKENS), PIPE_TOKENS)
        return pltpu.make_async_copy(e_hbm.at[rows, :], idx[slot], sems.at[slot])

    idx_copy(0, 0).start()
    idx_copy(1, 1).start()
    idx_copy(0, 0).wait()
    _gather_token(idx[0], 0, tab_ref, bufs[0])
    _gather_token(idx[0], 1, tab_ref, bufs[1])

    def block_pair(p, carry):
        for slot in range(2):
            block = 2 * p + slot
            for half in range(PIPE_TOKENS // 2):
                cur = bufs[2 * (half % 2):2 * (half % 2) + 2]
                nxt = bufs[2 - 2 * (half % 2):4 - 2 * (half % 2)]
                for j in range(2):
                    compute(block * PIPE_TOKENS + 2 * half + j, cur[j])
                if half + 1 < PIPE_TOKENS // 2:
                    for j in range(2):
                        _gather_token(idx[slot], 2 * half + 2 + j, tab_ref, nxt[j])
                else:
                    idx_copy(jnp.minimum(block + 1, blocks - 1), 1 - slot).wait()
                    for j in range(2):
                        _gather_token(idx[1 - slot], j, tab_ref, nxt[j])
            idx_copy(jnp.minimum(block + 2, blocks - 1), slot).start()
        return carry

    lax.fori_loop(0, blocks // 2, block_pair, 0)
    idx_copy(blocks - 1, 1).wait()


def _pipeline_scratch():
    return ([pltpu.VMEM((GROUP_ROWS, LANES), I32)] * 4 + [pltpu.SMEM((PIPE_TOKENS, PEER_SLOTS), I32)] * 2
            + [pltpu.SemaphoreType.DMA((2,))])


def _token_tile_rows(t):
    return pl.ds(pl.multiple_of(t * SUBLANES, SUBLANES), SUBLANES)


def _split2(x):
    hi = x.astype(BF16).astype(F32)
    return jnp.concatenate([hi, x - hi], axis=0).astype(BF16)


def _peer_u_body(e_hbm, x_ref, g_ref, u_ref, col_ref, mask_ref, w_ref, b0, b1, b2, b3, i0, i1, sems, z_scr, *, tile, chunk):
    mask = mask_ref[...]

    def compute(t, buf):
        z = _dot_nt(_split2(x_ref[_token_tile_rows(t), :]), pltpu.bitcast(buf[...], BF16))
        z_scr[_token_tile_rows(t), :] = (z[:SUBLANES] + z[SUBLANES:]) * mask

    _pipelined_tokens(e_hbm, u_ref, (b0, b1, b2, b3), (i0, i1), sems, compute, tile)
    for c in range(tile // chunk):
        zs = z_scr[c * chunk * SUBLANES:(c + 1) * chunk * SUBLANES, :]
        zh = zs.astype(BF16)
        zl = (zs - zh.astype(F32)).astype(BF16)
        part = _dot(zh, col_ref[...]) + _dot(zl, col_ref[...])
        a = jnp.sum(part.reshape(chunk, SUBLANES, PEER_SLOTS), axis=1)
        rows = slice(c * chunk, (c + 1) * chunk)
        w_ref[rows, :] = g_ref[rows, :] * (0.5 * a * (1.0 + lax.erf(a * np.float32(np.sqrt(0.5)))))


def _peer_u_call(e, xn8, gates, u_packed, consts, tile):
    n = e.shape[0]
    full = lambda a: pl.BlockSpec(a.shape, lambda i: (0,) * a.ndim)
    return pl.pallas_call(
        functools.partial(_peer_u_body, tile=tile, chunk=min(tile, 32)),
        grid=(n // tile,),
        in_specs=[pl.BlockSpec(memory_space=pl.ANY),
                  pl.BlockSpec((tile * SUBLANES, LANES), lambda i: (i, 0)),
                  pl.BlockSpec((tile, PEER_SLOTS), lambda i: (i, 0)),
                  pl.BlockSpec(memory_space=pltpu.VMEM),
                  full(consts["collapse"]), full(consts["mask8"])],
        out_specs=pl.BlockSpec((tile, PEER_SLOTS), lambda i: (i, 0)),
        out_shape=jax.ShapeDtypeStruct((n, PEER_SLOTS), F32),
        scratch_shapes=_pipeline_scratch() + [pltpu.VMEM((tile * SUBLANES, WIDE), F32)],
        compiler_params=pltpu.CompilerParams(dimension_semantics=("arbitrary",), vmem_limit_bytes=VMEM_LIMIT),
        name="peer_u",
    )(e, xn8, gates, u_packed, consts["collapse"], consts["mask8"])


def _peer_v_body(e_hbm, w_ref, x_ref, v_ref, exp_ref, mask_ref, y_ref, b0, b1, b2, b3, i0, i1, sems, eh_scr, el_scr, *, tile):
    mask = mask_ref[...]
    w = w_ref[...]
    wh = w.astype(BF16)
    eh_scr[...] = _dot(wh, exp_ref[...])
    el_scr[...] = _dot((w - wh.astype(F32)).astype(BF16), exp_ref[...])

    def compute(t, buf):
        row = pl.ds(t, 1)
        lhs = jnp.concatenate([jnp.broadcast_to(eh_scr[row, :], (SUBLANES, WIDE)) * mask,
                               jnp.broadcast_to(el_scr[row, :], (SUBLANES, WIDE)) * mask], axis=0).astype(BF16)
        o = _dot(lhs, pltpu.bitcast(buf[...], BF16))
        rows = _token_tile_rows(t)
        y_ref[rows, :] = x_ref[rows, :] + o[:SUBLANES] + o[SUBLANES:]

    _pipelined_tokens(e_hbm, v_ref, (b0, b1, b2, b3), (i0, i1), sems, compute, tile)


def _peer_v_call(e, w, x8, v_packed, consts, tile):
    n = e.shape[0]
    full = lambda a: pl.BlockSpec(a.shape, lambda i: (0,) * a.ndim)
    return pl.pallas_call(
        functools.partial(_peer_v_body, tile=tile),
        grid=(n // tile,),
        in_specs=[pl.BlockSpec(memory_space=pl.ANY),
                  pl.BlockSpec((tile, PEER_SLOTS), lambda i: (i, 0)),
                  pl.BlockSpec((tile * SUBLANES, LANES), lambda i: (i, 0)),
                  pl.BlockSpec(memory_space=pltpu.VMEM),
                  full(consts["expand"]), full(consts["mask8"])],
        out_specs=pl.BlockSpec((tile * SUBLANES, LANES), lambda i: (i, 0)),
        out_shape=jax.ShapeDtypeStruct((n * SUBLANES, LANES), F32),
        scratch_shapes=_pipeline_scratch() + [pltpu.VMEM((tile, WIDE), F32), pltpu.VMEM((tile, WIDE), F32)],
        compiler_params=pltpu.CompilerParams(dimension_semantics=("arbitrary",), vmem_limit_bytes=VMEM_LIMIT),
        name="peer_v",
    )(e, w, x8, v_packed, consts["expand"], consts["mask8"])


def _peer_constants():
    lane = np.arange(WIDE)
    expand = (lane[None, :] // SUBLANES == np.arange(PEER_SLOTS)[:, None]).astype(np.float32)
    mask8 = (lane[None, :] % SUBLANES == np.arange(SUBLANES)[:, None]).astype(np.float32)
    return {"expand": jnp.asarray(expand, BF16), "collapse": jnp.asarray(expand.T, BF16), "mask8": jnp.asarray(mask8)}


def _pack_table(tab):
    n = tab.shape[0]
    b = lax.bitcast_convert_type(tab.astype(BF16), jnp.uint16).astype(jnp.uint32).reshape(n, ROW_SUB, 2, LANES)
    packed = b[:, :, 0, :] | (b[:, :, 1, :] << 16)
    return lax.bitcast_convert_type(packed, I32).reshape(n * ROW_SUB, LANES)


def _block_mean(width):
    blk = np.arange(width) // HEAD_DIM
    return jnp.asarray((blk[:, None] == blk[None, :]) / HEAD_DIM, BF16)


def _forget_placement():
    pfq = np.zeros((3 * LANES, A_HEADS * LANES), np.float32)
    pfk = np.zeros((3 * LANES, A_HEADS * LANES), np.float32)
    cq = np.zeros((1, A_HEADS * LANES), np.float32)
    ck = np.zeros((1, A_HEADS * LANES), np.float32)
    for hd in range(A_HEADS):
        base = hd * LANES + HEAD_DIM
        for piece in range(3):
            pfq[piece * LANES + hd, base + piece] = 1.0
            pfk[piece * LANES + hd, base + 3 + piece] = -1.0
            cq[0, base + 3 + piece] = 1.0
            ck[0, base + piece] = 1.0
    return jnp.asarray(pfq, BF16), jnp.asarray(pfk, BF16), jnp.asarray(cq), jnp.asarray(ck)


def _layer_weights(l, g_attn, w_in, b_f, g_q_a, g_k_a, conv_w, conv_b, g_q_m, w_out, g_ffn, w_peer_q, peer_keys):
    scale = HEAD_DIM ** -0.5
    splits = np.cumsum([A_WIDTH, A_WIDTH, A_WIDTH, A_HEADS, B_WIDTH, B_WIDTH, B_WIDTH, M_WIDTH])
    wi = w_in[l]
    qa, ka, va, fl, bg, cg, hv, qm = [wi[:, a:b] for a, b in zip(np.r_[0, splits[:-1]], splits)]
    w_packed = jnp.concatenate([qa, ka, va, bg, cg, hv, qm, fl, jnp.zeros((wi.shape[0], LANES - A_HEADS), wi.dtype)],
                               axis=1).astype(BF16)
    pfq, pfk, cq, ck = _forget_placement()
    keys = peer_keys[l]
    zeros = jnp.zeros_like(keys[:, 0])
    keys2 = jnp.concatenate([jnp.concatenate([keys[:, 0], zeros], axis=-1),
                             jnp.concatenate([zeros, keys[:, 1]], axis=-1)], axis=1).astype(BF16)
    return {
        "g_attn": g_attn[l][None, :], "w_in": w_packed,
        "b_f": jnp.pad(b_f[l], (0, LANES - A_HEADS))[None, :],
        "g_q": (jnp.tile(g_q_a[l], A_HEADS) * scale)[None, :], "g_k": jnp.tile(g_k_a[l], A_HEADS)[None, :],
        "g_qm": (jnp.tile(g_q_m[l], M_HEADS) * scale)[None, :],
        "bd512": _block_mean(A_WIDTH), "bd256": _block_mean(M_WIDTH),
        "pfq": pfq, "pfk": pfk, "cq": cq, "ck": ck,
        "conv_w": conv_w[l], "conv_b": conv_b[l][None, :],
        "w_out": w_out[l].astype(BF16), "g_ffn": g_ffn[l][None, :], "w_pq": w_peer_q[l].astype(BF16), "keys": keys2,
    }


def _peer_and_merge(x, oa, ob, om, wts, u_packed, v_packed, route_tile, peer_tile):
    b, s, d = x.shape
    n = b * s
    x1, xn, e, gates = _route_call(x.reshape(n, d), oa.reshape(n, -1), ob.reshape(n, -1), om.reshape(n, -1), wts, route_tile)
    consts = _peer_constants()
    w = _peer_u_call(e, xn.reshape(n * SUBLANES, LANES), gates, u_packed, consts, peer_tile)
    y8 = _peer_v_call(e, w, x1.reshape(n * SUBLANES, LANES), v_packed, consts, peer_tile)
    return y8.reshape(b, s, d)


def kernel(x_prompt, x_sample, cache_a_k, cache_a_v, cache_a_logf, cache_b_conv, cache_m_k, cache_m_v, mem_prompt, g_attn, w_in, b_f, g_q_a, g_k_a, conv_w, conv_b, g_mem, w_mem_k, w_mem_v, g_k_m, g_q_m, w_out, g_ffn, w_peer_q, peer_keys, peer_u, peer_v):
    depth = w_in.shape[0]
    xp, xs = x_prompt, x_sample
    bp, sp, _ = xp.shape
    bs, ts, _ = xs.shape
    outs = [[] for _ in range(10)]
    for l in range(depth):
        wts = _layer_weights(l, g_attn, w_in, b_f, g_q_a, g_k_a, conv_w, conv_b, g_q_m, w_out, g_ffn, w_peer_q, peer_keys)
        u_packed, v_packed = _pack_table(peer_u[l]), _pack_table(peer_v[l])
        fox_blk = min(512, sp)
        route_tile = 256
        peer_tile = 128

        mk, mv = _memkv_call(mem_prompt, g_mem[l][None, :], w_mem_k[l].astype(BF16), w_mem_v[l].astype(BF16),
                             jnp.tile(g_k_m[l], M_HEADS)[None, :], wts["bd256"])
        ka, va, logf, qaug, kaug, vb, ob, om, cst = _proj_call(
            xp, jnp.zeros((bp, 2, B_WIDTH), F32), mk, mv, wts, min(512, sp))
        oa = _fox_prompt_call(qaug, kaug, vb, fox_blk)
        xp = _peer_and_merge(xp, oa, ob, om, wts, u_packed, v_packed, route_tile, peer_tile)
        n_mem = mk.shape[1]
        for dst, val in zip(outs[:6], [ka.reshape(bp, sp, A_HEADS, HEAD_DIM), va.reshape(bp, sp, A_HEADS, HEAD_DIM), logf, cst,
                                       mk.reshape(bp, n_mem, M_HEADS, HEAD_DIM), mv.reshape(bp, n_mem, M_HEADS, HEAD_DIM)]):
            dst.append(val)

        past = cache_a_k.shape[2]
        ka, va, logf, qaug, kaug, vb, ob, om, cst = _proj_call(
            xs, cache_b_conv[l], cache_m_k[l].reshape(bs, -1, M_WIDTH), cache_m_v[l].reshape(bs, -1, M_WIDTH), wts, ts)
        clf = jnp.pad(cache_a_logf[l], ((0, 0), (0, 0), (0, LANES - A_HEADS)))
        oa = _fox_sample_call(cache_a_k[l].reshape(bs, past, A_WIDTH), cache_a_v[l].reshape(bs, past, A_WIDTH), clf,
                              qaug, kaug, vb, wts["pfk"], wts["ck"])
        xs = _peer_and_merge(xs, oa, ob, om, wts, u_packed, v_packed, route_tile, peer_tile)
        for dst, val in zip(outs[6:], [ka.reshape(bs, ts, A_HEADS, HEAD_DIM), va.reshape(bs, ts, A_HEADS, HEAD_DIM), logf, cst]):
            dst.append(val)

    return (xp, xs) + tuple(jnp.stack(o) for o in outs)
```

```python
import functools

import numpy as np
import jax
import jax.numpy as jnp
from jax import lax
from jax.experimental import pallas as pl
from jax.experimental.pallas import tpu as pltpu

F32, BF16, I32 = jnp.float32, jnp.bfloat16, jnp.int32
EPS = 1e-6
LANES = 128
SUBLANES = 8
HEAD_DIM = 64
A_HEADS = 8
A_WIDTH = A_HEADS * HEAD_DIM
B_WIDTH = 256
M_HEADS = 4
M_WIDTH = M_HEADS * HEAD_DIM
PEER_HEADS = 8
PEER_KEYS = 128
PEER_TOPK = 16
PEER_SLOTS = PEER_HEADS * PEER_TOPK
D_MODEL = 1024
ROW_WORDS = D_MODEL // 2
ROW_SUB = ROW_WORDS // LANES
NEG_BIG = -1e30
VMEM_LIMIT = 56 * 1024 * 1024

C_Q, C_K, C_V, C_BG, C_CG, C_HV, C_QM, C_FL, C_END = 0, 512, 1024, 1536, 1792, 2048, 2304, 2560, 2688


def _dot(a, b):
    return jnp.dot(a, b, preferred_element_type=F32)


def _dot_nt(a, b):
    return lax.dot_general(a, b, (((1,), (1,)), ((), ())), preferred_element_type=F32)


def _split3(x):
    hi = x.astype(BF16)
    r1 = x - hi.astype(F32)
    mid = r1.astype(BF16)
    lo = (r1 - mid.astype(F32)).astype(BF16)
    return hi, mid, lo


def _lane(shape):
    return lax.broadcasted_iota(I32, shape, len(shape) - 1)


def _memkv_body(mem_ref, g_ref, wk_ref, wv_ref, gk_ref, bd_ref, mk_ref, mv_ref):
    x = mem_ref[0]
    h = (x * lax.rsqrt(jnp.mean(x * x, axis=-1, keepdims=True) + EPS) * g_ref[...]).astype(BF16)
    zk = _dot(h, wk_ref[...])
    ms = _dot((zk * zk).astype(BF16), bd_ref[...])
    mk_ref[0] = zk * lax.rsqrt(ms + EPS) * gk_ref[...]
    mv_ref[0] = _dot(h, wv_ref[...])


def _memkv_call(mem, g_mem, w_mk, w_mv, gk_t, bd256):
    b, n_mem, d = mem.shape
    full = lambda shape: pl.BlockSpec(shape, lambda i: (0,) * len(shape))
    return pl.pallas_call(
        _memkv_body,
        grid=(b,),
        in_specs=[pl.BlockSpec((1, n_mem, d), lambda i: (i, 0, 0)), full((1, d)), full((d, M_WIDTH)),
                  full((d, M_WIDTH)), full((1, M_WIDTH)), full((M_WIDTH, M_WIDTH))],
        out_specs=[pl.BlockSpec((1, n_mem, M_WIDTH), lambda i: (i, 0, 0))] * 2,
        out_shape=[jax.ShapeDtypeStruct((b, n_mem, M_WIDTH), F32)] * 2,
        compiler_params=pltpu.CompilerParams(dimension_semantics=("arbitrary",), vmem_limit_bytes=VMEM_LIMIT),
        name="mem_kv",
    )(mem, g_mem, w_mk, w_mv, gk_t, bd256)


def _proj_body(x_ref, prev_ref, mk_ref, mv_ref, gat_ref, w_ref, bf_ref, gq_ref, gk_ref, gqm_ref,
               bd512_ref, bd256_ref, pfq_ref, pfk_ref, cq_ref, ck_ref, cw_ref, cb_ref,
               ka_ref, va_ref, logf_ref, qaug_ref, kaug_ref, vb_ref, ob_ref, om_ref, cst_ref,
               fcarry, ucarry):
    t = pl.program_id(1)
    tt = x_ref.shape[1]
    x = x_ref[0]
    h = (x * lax.rsqrt(jnp.mean(x * x, axis=-1, keepdims=True) + EPS) * gat_ref[...]).astype(BF16)

    def proj(c0, c1):
        return _dot(h, w_ref[:, c0:c1])

    @pl.when(t == 0)
    def _():
        fcarry[...] = jnp.zeros_like(fcarry)
        ucarry[...] = jnp.zeros_like(ucarry)
        ucarry[SUBLANES - 2:SUBLANES, :] = prev_ref[0]

    v = proj(C_FL, C_END) + bf_ref[...]
    logf = jnp.minimum(v, 0.0) - jnp.log1p(jnp.exp(-jnp.abs(v)))
    logf = jnp.where(_lane(logf.shape) < A_HEADS, logf, 0.0)
    logf_ref[0] = logf[:, :A_HEADS]
    row = lax.broadcasted_iota(I32, (tt, tt), 0)
    col = lax.broadcasted_iota(I32, (tt, tt), 1)
    tri = jnp.where(row >= col, 1.0, 0.0).astype(BF16)
    lh, lm, ll = _split3(logf)
    fcum = _dot(tri, lh) + _dot(tri, lm) + _dot(tri, ll) + fcarry[0:1, :]
    fcarry[...] = jnp.broadcast_to(fcum[tt - 1:tt, :], fcarry.shape)
    fparts = jnp.concatenate(_split3(fcum), axis=1)
    faq = _dot(fparts, pfq_ref[...]) + cq_ref[...]
    fak = _dot(fparts, pfk_ref[...]) + ck_ref[...]

    lane = _lane((tt, LANES))
    zq = proj(C_Q, C_K)
    qn = zq * lax.rsqrt(_dot((zq * zq).astype(BF16), bd512_ref[...]) + EPS) * gq_ref[...]
    zk = proj(C_K, C_V)
    kn = zk * lax.rsqrt(_dot((zk * zk).astype(BF16), bd512_ref[...]) + EPS) * gk_ref[...]
    ka_ref[0] = kn
    for hd in range(A_HEADS):
        c0 = LANES * (hd // 2)
        qt, kt = qn[:, c0:c0 + LANES], kn[:, c0:c0 + LANES]
        if hd % 2:
            qt, kt = pltpu.roll(qt, HEAD_DIM, 1), pltpu.roll(kt, HEAD_DIM, 1)
        qaug_ref[0, hd] = jnp.where(lane < HEAD_DIM, qt, faq[:, LANES * hd:LANES * (hd + 1)]).astype(BF16)
        kaug_ref[0, hd] = jnp.where(lane < HEAD_DIM, kt, fak[:, LANES * hd:LANES * (hd + 1)]).astype(BF16)
    zv = proj(C_V, C_BG)
    va_ref[0] = zv
    vb_ref[0] = zv.astype(BF16)

    u = proj(C_CG, C_HV) * proj(C_HV, C_QM)
    rows = lax.broadcasted_iota(I32, u.shape, 0)
    p1 = ucarry[SUBLANES - 1:SUBLANES, :]
    p2 = ucarry[SUBLANES - 2:SUBLANES - 1, :]
    u1 = jnp.where(rows == 0, p1, pltpu.roll(u, 1, 0))
    u2 = jnp.where(rows == 0, p2, jnp.where(rows == 1, p1, pltpu.roll(u, 2, 0)))
    cy = cb_ref[...] + cw_ref[2:3, :] * u + cw_ref[0:1, :] * u2 + cw_ref[1:2, :] * u1
    ob_ref[0] = (proj(C_BG, C_CG) * cy).astype(BF16)
    ucarry[...] = u[tt - SUBLANES:tt, :]
    cst_ref[0] = u[tt - 2:tt, :]

    zm = proj(C_QM, C_FL)
    qm = zm * lax.rsqrt(_dot((zm * zm).astype(BF16), bd256_ref[...]) + EPS) * gqm_ref[...]
    mkb = mk_ref[0].astype(BF16)
    mvb = mv_ref[0].astype(BF16)
    outs = []
    for pr in range(M_HEADS // 2):
        qp = qm[:, LANES * pr:LANES * (pr + 1)]
        kp = mkb[:, LANES * pr:LANES * (pr + 1)]
        vp = mvb[:, LANES * pr:LANES * (pr + 1)]
        o = []
        for sub in range(2):
            keep = (lane < HEAD_DIM) if sub == 0 else (lane >= HEAD_DIM)
            s = _dot_nt(jnp.where(keep, qp, 0.0).astype(BF16), kp)
            p = jnp.exp(s - jnp.max(s, axis=-1, keepdims=True))
            o.append(_dot(p.astype(BF16), vp) / jnp.sum(p, axis=-1, keepdims=True))
        outs.append(jnp.where(lane < HEAD_DIM, o[0], o[1]))
    om_ref[0] = jnp.concatenate(outs, axis=1).astype(BF16)


def _proj_call(x, prev, mk, mv, wts, tile):
    b, s, d = x.shape
    nt = s // tile
    n_mem = mk.shape[1]
    full = lambda a: pl.BlockSpec(a.shape, lambda i, j: (0,) * a.ndim)
    seq = lambda w: pl.BlockSpec((1, tile, w), lambda i, j: (i, j, 0))
    per_b = lambda r, w: pl.BlockSpec((1, r, w), lambda i, j: (i, 0, 0))
    heads = pl.BlockSpec((1, A_HEADS, tile, LANES), lambda i, j: (i, 0, j, 0))
    names = ["g_attn", "w_in", "b_f", "g_q", "g_k", "g_qm", "bd512", "bd256", "pfq", "pfk", "cq", "ck", "conv_w", "conv_b"]
    consts = [wts[k] for k in names]
    out_shape = [
        jax.ShapeDtypeStruct((b, s, A_WIDTH), F32),
        jax.ShapeDtypeStruct((b, s, A_WIDTH), F32),
        jax.ShapeDtypeStruct((b, s, A_HEADS), F32),
        jax.ShapeDtypeStruct((b, A_HEADS, s, LANES), BF16),
        jax.ShapeDtypeStruct((b, A_HEADS, s, LANES), BF16),
        jax.ShapeDtypeStruct((b, s, A_WIDTH), BF16),
        jax.ShapeDtypeStruct((b, s, B_WIDTH), BF16),
        jax.ShapeDtypeStruct((b, s, M_WIDTH), BF16),
        jax.ShapeDtypeStruct((b, 2, B_WIDTH), F32),
    ]
    out_specs = [seq(A_WIDTH), seq(A_WIDTH), seq(A_HEADS), heads, heads, seq(A_WIDTH), seq(B_WIDTH), seq(M_WIDTH),
                 per_b(2, B_WIDTH)]
    return pl.pallas_call(
        _proj_body,
        grid=(b, nt),
        in_specs=[seq(d), per_b(2, B_WIDTH), per_b(n_mem, M_WIDTH), per_b(n_mem, M_WIDTH)] + [full(c) for c in consts],
        out_specs=out_specs,
        out_shape=out_shape,
        scratch_shapes=[pltpu.VMEM((SUBLANES, LANES), F32), pltpu.VMEM((SUBLANES, B_WIDTH), F32)],
        compiler_params=pltpu.CompilerParams(dimension_semantics=("arbitrary", "arbitrary"), vmem_limit_bytes=VMEM_LIMIT),
        name="proj",
    )(x, prev, mk, mv, *consts)


def _fox_prompt_body(q_ref, k_ref, v_ref, o_ref, *, blk):
    qi = pl.program_id(2)
    ones = jnp.ones((blk, LANES), BF16)
    row = lax.broadcasted_iota(I32, (blk, blk), 0)
    col = lax.broadcasted_iota(I32, (blk, blk), 1)
    outs = []
    for sub in range(2):
        q = q_ref[0, sub]

        def step(kj, carry, masked):
            m, acc = carry
            off = pl.multiple_of(kj * blk, blk)
            s = _dot_nt(q, k_ref[0, sub, pl.ds(off, blk), :])
            if masked:
                s = jnp.where(col <= row, s, NEG_BIG)
            m_new = jnp.maximum(m, jnp.max(s, axis=-1, keepdims=True))
            p = jnp.exp(s - m_new).astype(BF16)
            vv = jnp.concatenate([v_ref[0, pl.ds(off, blk), :], ones], axis=1)
            return m_new, jnp.exp(m - m_new) * acc + _dot(p, vv)

        init = (jnp.full((blk, 1), NEG_BIG, F32), jnp.zeros((blk, 2 * LANES), F32))
        carry = lax.fori_loop(0, qi, functools.partial(step, masked=False), init)
        _, acc = step(qi, carry, True)
        outs.append(acc[:, :LANES] / acc[:, LANES:])
    o_ref[0] = jnp.where(_lane((blk, LANES)) < HEAD_DIM, outs[0], outs[1]).astype(BF16)


def _fox_prompt_call(qaug, kaug, vb, blk):
    b, _, s, _ = qaug.shape
    return pl.pallas_call(
        functools.partial(_fox_prompt_body, blk=blk),
        grid=(b, A_HEADS // 2, s // blk),
        in_specs=[pl.BlockSpec((1, 2, blk, LANES), lambda i, hp, j: (i, hp, j, 0)),
                  pl.BlockSpec((1, 2, s, LANES), lambda i, hp, j: (i, hp, 0, 0)),
                  pl.BlockSpec((1, s, LANES), lambda i, hp, j: (i, 0, hp))],
        out_specs=pl.BlockSpec((1, blk, LANES), lambda i, hp, j: (i, j, hp)),
        out_shape=jax.ShapeDtypeStruct((b, s, A_WIDTH), BF16),
        compiler_params=pltpu.CompilerParams(dimension_semantics=("arbitrary",) * 3, vmem_limit_bytes=VMEM_LIMIT),
        name="fox_prompt",
    )(qaug, kaug, vb)


def _fox_sample_body(ck_ref, cv_ref, clf_ref, q_ref, k_ref, v_ref, pfk_ref, ckc_ref, o_ref):
    past = ck_ref.shape[1]
    ts = q_ref.shape[2]
    row = lax.broadcasted_iota(I32, (past, past), 0)
    col = lax.broadcasted_iota(I32, (past, past), 1)
    tri = jnp.where(col > row, 1.0, 0.0).astype(BF16)
    lh, lm, ll = _split3(clf_ref[0])
    suffix = _dot(tri, lh) + _dot(tri, lm) + _dot(tri, ll)
    fak = _dot(jnp.concatenate(_split3(-suffix), axis=1), pfk_ref[...]) + ckc_ref[...]
    lane = _lane((past, LANES))
    lane_s = _lane((ts, LANES))
    causal = lax.broadcasted_iota(I32, (ts, ts), 1) <= lax.broadcasted_iota(I32, (ts, ts), 0)
    outs = []
    for pr in range(A_HEADS // 2):
        kc2 = ck_ref[0, :, LANES * pr:LANES * (pr + 1)]
        vc = cv_ref[0, :, LANES * pr:LANES * (pr + 1)].astype(BF16)
        vn = v_ref[0, :, LANES * pr:LANES * (pr + 1)]
        o = []
        for sub in range(2):
            hd = 2 * pr + sub
            kt = pltpu.roll(kc2, HEAD_DIM, 1) if sub else kc2
            kc = jnp.where(lane < HEAD_DIM, kt, fak[:, LANES * hd:LANES * (hd + 1)]).astype(BF16)
            q = q_ref[0, hd]
            s1 = _dot_nt(q, kc)
            s2 = jnp.where(causal, _dot_nt(q, k_ref[0, hd]), NEG_BIG)
            m = jnp.maximum(jnp.max(s1, axis=-1, keepdims=True), jnp.max(s2, axis=-1, keepdims=True))
            p1, p2 = jnp.exp(s1 - m), jnp.exp(s2 - m)
            den = jnp.sum(p1, axis=-1, keepdims=True) + jnp.sum(p2, axis=-1, keepdims=True)
            o.append((_dot(p1.astype(BF16), vc) + _dot(p2.astype(BF16), vn)) / den)
        outs.append(jnp.where(lane_s < HEAD_DIM, o[0], o[1]))
    o_ref[0] = jnp.concatenate(outs, axis=1).astype(BF16)


def _fox_sample_call(cache_k, cache_v, cache_lf, qaug, kaug, vb, pfk, ck):
    b, past, _ = cache_k.shape
    ts = qaug.shape[2]
    full = lambda a: pl.BlockSpec(a.shape, lambda i: (0,) * a.ndim)
    return pl.pallas_call(
        _fox_sample_body,
        grid=(b,),
        in_specs=[pl.BlockSpec((1, past, A_WIDTH), lambda i: (i, 0, 0)),
                  pl.BlockSpec((1, past, A_WIDTH), lambda i: (i, 0, 0)),
                  pl.BlockSpec((1, past, LANES), lambda i: (i, 0, 0)),
                  pl.BlockSpec((1, A_HEADS, ts, LANES), lambda i: (i, 0, 0, 0)),
                  pl.BlockSpec((1, A_HEADS, ts, LANES), lambda i: (i, 0, 0, 0)),
                  pl.BlockSpec((1, ts, A_WIDTH), lambda i: (i, 0, 0)),
                  full(pfk), full(ck)],
        out_specs=pl.BlockSpec((1, ts, A_WIDTH), lambda i: (i, 0, 0)),
        out_shape=jax.ShapeDtypeStruct((b, ts, A_WIDTH), BF16),
        compiler_params=pltpu.CompilerParams(dimension_semantics=("arbitrary",), vmem_limit_bytes=VMEM_LIMIT),
        name="fox_sample",
    )(cache_k, cache_v, cache_lf, qaug, kaug, vb, pfk, ck)


def _top_rows(vals, payload, k):
    r, t = vals.shape
    iota = lax.broadcasted_iota(I32, (r, t), 0)
    slot = lax.broadcasted_iota(I32, (k, t), 0)
    top_v = jnp.zeros((k, t), F32)
    top_p = jnp.zeros((k, t), I32)
    for j in range(k):
        m = jnp.max(vals, axis=0, keepdims=True)
        sel = jnp.min(jnp.where(vals == m, iota, r), axis=0, keepdims=True)
        hit = iota == sel
        pay = sel if payload is None else jnp.max(jnp.where(hit, payload, -1), axis=0, keepdims=True)
        top_v = jnp.where(slot == j, m, top_v)
        top_p = jnp.where(slot == j, pay, top_p)
        vals = jnp.where(hit, -jnp.inf, vals)
    return top_v, top_p


def _route_body(x_ref, oa_ref, ob_ref, om_ref, wo_ref, gf_ref, wq_ref, keys_ref,
                x1_ref, xn_ref, e_ref, g_ref, qp_scr, e_scr, g_scr):
    y = x_ref[...] + _dot(oa_ref[...], wo_ref[0:A_WIDTH, :]) \
        + _dot(ob_ref[...], wo_ref[A_WIDTH:A_WIDTH + B_WIDTH, :]) \
        + _dot(om_ref[...], wo_ref[A_WIDTH + B_WIDTH:, :])
    x1_ref[...] = y
    xn = y * lax.rsqrt(jnp.mean(y * y, axis=-1, keepdims=True) + EPS) * gf_ref[...]
    xn_ref[...] = xn
    qp = _dot(xn.astype(BF16), wq_ref[...])
    for hd in range(PEER_HEADS):
        qp_scr[hd] = qp[:, LANES * hd:LANES * (hd + 1)].astype(BF16)

    def head(hd, carry):
        sc = _dot_nt(keys_ref[hd], qp_scr[hd])
        s1, i1 = _top_rows(sc[:PEER_KEYS], None, PEER_TOPK)
        s2, i2 = _top_rows(sc[PEER_KEYS:], None, PEER_TOPK)
        cand = jnp.concatenate([s1[i:i + 1, :] + s2 for i in range(PEER_TOPK)], axis=0)
        expert = jnp.concatenate([i1[i:i + 1, :] * PEER_KEYS + i2 for i in range(PEER_TOPK)], axis=0)
        top, e = _top_rows(cand, expert, PEER_TOPK)
        p = jnp.exp(top - top[0:1, :])
        g_scr[hd] = p / jnp.sum(p, axis=0, keepdims=True)
        e_scr[hd] = e * ROW_SUB
        return carry

    lax.fori_loop(0, PEER_HEADS, head, 0)
    e_ref[...] = jnp.concatenate([e_scr[hd] for hd in range(PEER_HEADS)], axis=0).T
    g_ref[...] = jnp.concatenate([g_scr[hd] for hd in range(PEER_HEADS)], axis=0).T


def _route_call(x, oa, ob, om, wts, tile):
    n, d = x.shape
    full = lambda a: pl.BlockSpec(a.shape, lambda i: (0,) * a.ndim)
    rows = lambda w: pl.BlockSpec((tile, w), lambda i: (i, 0))
    consts = [wts[k] for k in ["w_out", "g_ffn", "w_pq", "keys"]]
    return pl.pallas_call(
        _route_body,
        grid=(n // tile,),
        in_specs=[rows(d), rows(A_WIDTH), rows(B_WIDTH), rows(M_WIDTH)] + [full(c) for c in consts],
        out_specs=[rows(d), rows(d), rows(PEER_SLOTS), rows(PEER_SLOTS)],
        out_shape=[jax.ShapeDtypeStruct((n, d), F32), jax.ShapeDtypeStruct((n, d), F32),
                   jax.ShapeDtypeStruct((n, PEER_SLOTS), I32), jax.ShapeDtypeStruct((n, PEER_SLOTS), F32)],
        scratch_shapes=[pltpu.VMEM((PEER_HEADS, tile, LANES), BF16),
                        pltpu.VMEM((PEER_HEADS, PEER_TOPK, tile), I32),
                        pltpu.VMEM((PEER_HEADS, PEER_TOPK, tile), F32)],
        compiler_params=pltpu.CompilerParams(dimension_semantics=("arbitrary",), vmem_limit_bytes=VMEM_LIMIT),
        name="merge_route",
    )(x, oa, ob, om, *consts)


GROUP_ROWS = PEER_SLOTS * ROW_SUB
WIDE = PEER_SLOTS * SUBLANES
PIPE_TOKENS = 32


def _gather_token(idx_ref, row, tab_ref, buf):
    for k in range(PEER_SLOTS):
        buf[k * ROW_SUB:(k + 1) * ROW_SUB, :] = tab_ref[pl.ds(pl.multiple_of(idx_ref[row, k], ROW_SUB), ROW_SUB), :]


def _pipelined_tokens(e_hbm, tab_ref, bufs, idx, sems, compute, tile):
    blocks = tile // PIPE_TOKENS
    first_block = pl.program_id(0) * blocks
    all_blocks = pl.num_programs(0) * blocks

    def idx_copy(block, slot):
        rows = pl.ds(pl.multiple_of(block * PIPE_TOKENS, PIPE_TOKENS), PIPE_TOKENS)
        return pltpu.make_async_copy(e_hbm.at[rows, :], idx[slot], sems.at[slot])

    @pl.when(pl.program_id(0) == 0)
    def _():
        idx_copy(0, 0).start()
        idx_copy(1, 1).start()
        idx_copy(0, 0).wait()
        _gather_token(idx[0], 0, tab_ref, bufs[0])
        _gather_token(idx[0], 1, tab_ref, bufs[1])

    def block_pair(p, carry):
        for slot in range(2):
            local = 2 * p + slot
            block = first_block + local
            for half in range(PIPE_TOKENS // 2):
                cur = bufs[2 * (half % 2):2 * (half % 2) + 2]
                nxt = bufs[2 - 2 * (half % 2):4 - 2 * (half % 2)]
                for j in range(2):
                    compute(local * PIPE_TOKENS + 2 * half + j, cur[j])
                if half + 1 < PIPE_TOKENS // 2:
                    for j in range(2):
                        _gather_token(idx[slot], 2 * half + 2 + j, tab_ref, nxt[j])
                else:
                    @pl.when(block + 1 < all_blocks)
                    def _():
                        idx_copy(block + 1, 1 - slot).wait()
                        for j in range(2):
                            _gather_token(idx[1 - slot], j, tab_ref, nxt[j])

            @pl.when(block + 2 < all_blocks)
            def _():
                idx_copy(block + 2, slot).start()
        return carry

    lax.fori_loop(0, blocks // 2, block_pair, 0)


def _pipeline_scratch():
    return ([pltpu.VMEM((GROUP_ROWS, LANES), I32)] * 4 + [pltpu.SMEM((PIPE_TOKENS, PEER_SLOTS), I32)] * 2
            + [pltpu.SemaphoreType.DMA((2,))])


def _token_tile_rows(t):
    return pl.ds(pl.multiple_of(t * SUBLANES, SUBLANES), SUBLANES)


def _split2(x):
    hi = x.astype(BF16).astype(F32)
    return jnp.concatenate([hi, x - hi], axis=0).astype(BF16)


def _peer_u_body(e_hbm, x_ref, g_ref, u_ref, col_ref, mask_ref, w_ref, b0, b1, b2, b3, i0, i1, sems, z_scr, *, tile, chunk):
    mask = mask_ref[...]

    def compute(t, buf):
        z = _dot_nt(_split2(x_ref[_token_tile_rows(t), :]), pltpu.bitcast(buf[...], BF16))
        z_scr[_token_tile_rows(t), :] = (z[:SUBLANES] + z[SUBLANES:]) * mask

    _pipelined_tokens(e_hbm, u_ref, (b0, b1, b2, b3), (i0, i1), sems, compute, tile)
    for c in range(tile // chunk):
        zs = z_scr[c * chunk * SUBLANES:(c + 1) * chunk * SUBLANES, :]
        zh = zs.astype(BF16)
        zl = (zs - zh.astype(F32)).astype(BF16)
        part = _dot(zh, col_ref[...]) + _dot(zl, col_ref[...])
        a = jnp.sum(part.reshape(chunk, SUBLANES, PEER_SLOTS), axis=1)
        rows = slice(c * chunk, (c + 1) * chunk)
        w_ref[rows, :] = g_ref[rows, :] * (0.5 * a * (1.0 + lax.erf(a * np.float32(np.sqrt(0.5)))))


def _peer_u_call(e, xn8, gates, u_packed, consts, tile):
    n = e.shape[0]
    full = lambda a: pl.BlockSpec(a.shape, lambda i: (0,) * a.ndim)
    return pl.pallas_call(
        functools.partial(_peer_u_body, tile=tile, chunk=min(tile, 32)),
        grid=(n // tile,),
        in_specs=[pl.BlockSpec(memory_space=pl.ANY),
                  pl.BlockSpec((tile * SUBLANES, LANES), lambda i: (i, 0)),
                  pl.BlockSpec((tile, PEER_SLOTS), lambda i: (i, 0)),
                  pl.BlockSpec(memory_space=pltpu.VMEM),
                  full(consts["collapse"]), full(consts["mask8"])],
        out_specs=pl.BlockSpec((tile, PEER_SLOTS), lambda i: (i, 0)),
        out_shape=jax.ShapeDtypeStruct((n, PEER_SLOTS), F32),
        scratch_shapes=_pipeline_scratch() + [pltpu.VMEM((tile * SUBLANES, WIDE), F32)],
        compiler_params=pltpu.CompilerParams(dimension_semantics=("arbitrary",), vmem_limit_bytes=VMEM_LIMIT),
        name="peer_u",
    )(e, xn8, gates, u_packed, consts["collapse"], consts["mask8"])


def _peer_v_body(e_hbm, w_ref, x_ref, v_ref, exp_ref, mask_ref, y_ref, b0, b1, b2, b3, i0, i1, sems, eh_scr, el_scr, *, tile):
    mask = mask_ref[...]
    w = w_ref[...]
    wh = w.astype(BF16)
    eh_scr[...] = _dot(wh, exp_ref[...])
    el_scr[...] = _dot((w - wh.astype(F32)).astype(BF16), exp_ref[...])

    def compute(t, buf):
        row = pl.ds(t, 1)
        lhs = jnp.concatenate([jnp.broadcast_to(eh_scr[row, :], (SUBLANES, WIDE)) * mask,
                               jnp.broadcast_to(el_scr[row, :], (SUBLANES, WIDE)) * mask], axis=0).astype(BF16)
        o = _dot(lhs, pltpu.bitcast(buf[...], BF16))
        rows = _token_tile_rows(t)
        y_ref[rows, :] = x_ref[rows, :] + o[:SUBLANES] + o[SUBLANES:]

    _pipelined_tokens(e_hbm, v_ref, (b0, b1, b2, b3), (i0, i1), sems, compute, tile)


def _peer_v_call(e, w, x8, v_packed, consts, tile):
    n = e.shape[0]
    full = lambda a: pl.BlockSpec(a.shape, lambda i: (0,) * a.ndim)
    return pl.pallas_call(
        functools.partial(_peer_v_body, tile=tile),
        grid=(n // tile,),
        in_specs=[pl.BlockSpec(memory_space=pl.ANY),
                  pl.BlockSpec((tile, PEER_SLOTS), lambda i: (i, 0)),
                  pl.BlockSpec((tile * SUBLANES, LANES), lambda i: (i, 0)),
                  pl.BlockSpec(memory_space=pltpu.VMEM),
                  full(consts["expand"]), full(consts["mask8"])],
        out_specs=pl.BlockSpec((tile * SUBLANES, LANES), lambda i: (i, 0)),
        out_shape=jax.ShapeDtypeStruct((n * SUBLANES, LANES), F32),
        scratch_shapes=_pipeline_scratch() + [pltpu.VMEM((tile, WIDE), F32), pltpu.VMEM((tile, WIDE), F32)],
        compiler_params=pltpu.CompilerParams(dimension_semantics=("arbitrary",), vmem_limit_bytes=VMEM_LIMIT),
        name="peer_v",
    )(e, w, x8, v_packed, consts["expand"], consts["mask8"])


def _peer_constants():
    lane = np.arange(WIDE)
    expand = (lane[None, :] // SUBLANES == np.arange(PEER_SLOTS)[:, None]).astype(np.float32)
    mask8 = (lane[None, :] % SUBLANES == np.arange(SUBLANES)[:, None]).astype(np.float32)
    return {"expand": jnp.asarray(expand, BF16), "collapse": jnp.asarray(expand.T, BF16), "mask8": jnp.asarray(mask8)}


def _pack_table(tab):
    n = tab.shape[0]
    b = lax.bitcast_convert_type(tab.astype(BF16), jnp.uint16).astype(jnp.uint32).reshape(n, ROW_SUB, 2, LANES)
    packed = b[:, :, 0, :] | (b[:, :, 1, :] << 16)
    return lax.bitcast_convert_type(packed, I32).reshape(n * ROW_SUB, LANES)


def _block_mean(width):
    blk = np.arange(width) // HEAD_DIM
    return jnp.asarray((blk[:, None] == blk[None, :]) / HEAD_DIM, BF16)


def _forget_placement():
    pfq = np.zeros((3 * LANES, A_HEADS * LANES), np.float32)
    pfk = np.zeros((3 * LANES, A_HEADS * LANES), np.float32)
    cq = np.zeros((1, A_HEADS * LANES), np.float32)
    ck = np.zeros((1, A_HEADS * LANES), np.float32)
    for hd in range(A_HEADS):
        base = hd * LANES + HEAD_DIM
        for piece in range(3):
            pfq[piece * LANES + hd, base + piece] = 1.0
            pfk[piece * LANES + hd, base + 3 + piece] = -1.0
            cq[0, base + 3 + piece] = 1.0
            ck[0, base + piece] = 1.0
    return jnp.asarray(pfq, BF16), jnp.asarray(pfk, BF16), jnp.asarray(cq), jnp.asarray(ck)


def _layer_weights(l, g_attn, w_in, b_f, g_q_a, g_k_a, conv_w, conv_b, g_q_m, w_out, g_ffn, w_peer_q, peer_keys):
    scale = HEAD_DIM ** -0.5
    splits = np.cumsum([A_WIDTH, A_WIDTH, A_WIDTH, A_HEADS, B_WIDTH, B_WIDTH, B_WIDTH, M_WIDTH])
    wi = w_in[l]
    qa, ka, va, fl, bg, cg, hv, qm = [wi[:, a:b] for a, b in zip(np.r_[0, splits[:-1]], splits)]
    w_packed = jnp.concatenate([qa, ka, va, bg, cg, hv, qm, fl, jnp.zeros((wi.shape[0], LANES - A_HEADS), wi.dtype)],
                               axis=1).astype(BF16)
    pfq, pfk, cq, ck = _forget_placement()
    keys = peer_keys[l]
    zeros = jnp.zeros_like(keys[:, 0])
    keys2 = jnp.concatenate([jnp.concatenate([keys[:, 0], zeros], axis=-1),
                             jnp.concatenate([zeros, keys[:, 1]], axis=-1)], axis=1).astype(BF16)
    return {
        "g_attn": g_attn[l][None, :], "w_in": w_packed,
        "b_f": jnp.pad(b_f[l], (0, LANES - A_HEADS))[None, :],
        "g_q": (jnp.tile(g_q_a[l], A_HEADS) * scale)[None, :], "g_k": jnp.tile(g_k_a[l], A_HEADS)[None, :],
        "g_qm": (jnp.tile(g_q_m[l], M_HEADS) * scale)[None, :],
        "bd512": _block_mean(A_WIDTH), "bd256": _block_mean(M_WIDTH),
        "pfq": pfq, "pfk": pfk, "cq": cq, "ck": ck,
        "conv_w": conv_w[l], "conv_b": conv_b[l][None, :],
        "w_out": w_out[l].astype(BF16), "g_ffn": g_ffn[l][None, :], "w_pq": w_peer_q[l].astype(BF16), "keys": keys2,
    }


def _peer_and_merge(x, oa, ob, om, wts, u_packed, v_packed, route_tile, peer_tile):
    b, s, d = x.shape
    n = b * s
    x1, xn, e, gates = _route_call(x.reshape(n, d), oa.reshape(n, -1), ob.reshape(n, -1), om.reshape(n, -1), wts, route_tile)
    consts = _peer_constants()
    w = _peer_u_call(e, xn.reshape(n * SUBLANES, LANES), gates, u_packed, consts, peer_tile)
    y8 = _peer_v_call(e, w, x1.reshape(n * SUBLANES, LANES), v_packed, consts, peer_tile)
    return y8.reshape(b, s, d)


def kernel(x_prompt, x_sample, cache_a_k, cache_a_v, cache_a_logf, cache_b_conv, cache_m_k, cache_m_v, mem_prompt, g_attn, w_in, b_f, g_q_a, g_k_a, conv_w, conv_b, g_mem, w_mem_k, w_mem_v, g_k_m, g_q_m, w_out, g_ffn, w_peer_q, peer_keys, peer_u, peer_v):
    depth = w_in.shape[0]
    xp, xs = x_prompt, x_sample
    bp, sp, _ = xp.shape
    bs, ts, _ = xs.shape
    outs = [[] for _ in range(10)]
    for l in range(depth):
        wts = _layer_weights(l, g_attn, w_in, b_f, g_q_a, g_k_a, conv_w, conv_b, g_q_m, w_out, g_ffn, w_peer_q, peer_keys)
        u_packed, v_packed = _pack_table(peer_u[l]), _pack_table(peer_v[l])
        fox_blk = min(512, sp)
        route_tile = 256
        peer_tile = 128

        mk, mv = _memkv_call(mem_prompt, g_mem[l][None, :], w_mem_k[l].astype(BF16), w_mem_v[l].astype(BF16),
                             jnp.tile(g_k_m[l], M_HEADS)[None, :], wts["bd256"])
        ka, va, logf, qaug, kaug, vb, ob, om, cst = _proj_call(
            xp, jnp.zeros((bp, 2, B_WIDTH), F32), mk, mv, wts, min(512, sp))
        oa = _fox_prompt_call(qaug, kaug, vb, fox_blk)
        xp = _peer_and_merge(xp, oa, ob, om, wts, u_packed, v_packed, route_tile, peer_tile)
        n_mem = mk.shape[1]
        for dst, val in zip(outs[:6], [ka.reshape(bp, sp, A_HEADS, HEAD_DIM), va.reshape(bp, sp, A_HEADS, HEAD_DIM), logf, cst,
                                       mk.reshape(bp, n_mem, M_HEADS, HEAD_DIM), mv.reshape(bp, n_mem, M_HEADS, HEAD_DIM)]):
            dst.append(val)

        past = cache_a_k.shape[2]
        ka, va, logf, qaug, kaug, vb, ob, om, cst = _proj_call(
            xs, cache_b_conv[l], cache_m_k[l].reshape(bs, -1, M_WIDTH), cache_m_v[l].reshape(bs, -1, M_WIDTH), wts, ts)
        clf = jnp.pad(cache_a_logf[l], ((0, 0), (0, 0), (0, LANES - A_HEADS)))
        oa = _fox_sample_call(cache_a_k[l].reshape(bs, past, A_WIDTH), cache_a_v[l].reshape(bs, past, A_WIDTH), clf,
                              qaug, kaug, vb, wts["pfk"], wts["ck"])
        xs = _peer_and_merge(xs, oa, ob, om, wts, u_packed, v_packed, route_tile, peer_tile)
        for dst, val in zip(outs[6:], [ka.reshape(bs, ts, A_HEADS, HEAD_DIM), va.reshape(bs, ts, A_HEADS, HEAD_DIM), logf, cst]):
            dst.append(val)

    return (xp, xs) + tuple(jnp.stack(o) for o in outs)
```

```python
import functools

import numpy as np
import jax
import jax.numpy as jnp
from jax import lax
from jax.experimental import pallas as pl
from jax.experimental.pallas import tpu as pltpu

F32, BF16, I32 = jnp.float32, jnp.bfloat16, jnp.int32
EPS = 1e-6
LANES = 128
SUBLANES = 8
HEAD_DIM = 64
A_HEADS = 8
A_WIDTH = A_HEADS * HEAD_DIM
B_WIDTH = 256
M_HEADS = 4
M_WIDTH = M_HEADS * HEAD_DIM
PEER_HEADS = 8
PEER_KEYS = 128
PEER_TOPK = 16
PEER_SLOTS = PEER_HEADS * PEER_TOPK
D_MODEL = 1024
ROW_WORDS = D_MODEL // 2
ROW_SUB = ROW_WORDS // LANES
NEG_BIG = -1e30
VMEM_LIMIT = 56 * 1024 * 1024

C_Q, C_K, C_V, C_BG, C_CG, C_HV, C_QM, C_FL, C_END = 0, 512, 1024, 1536, 1792, 2048, 2304, 2560, 2688


def _dot(a, b):
    return jnp.dot(a, b, preferred_element_type=F32)


def _dot_nt(a, b):
    return lax.dot_general(a, b, (((1,), (1,)), ((), ())), preferred_element_type=F32)


def _split3(x):
    hi = x.astype(BF16)
    r1 = x - hi.astype(F32)
    mid = r1.astype(BF16)
    lo = (r1 - mid.astype(F32)).astype(BF16)
    return hi, mid, lo


def _lane(shape):
    return lax.broadcasted_iota(I32, shape, len(shape) - 1)


def _memkv_body(mem_ref, g_ref, wk_ref, wv_ref, gk_ref, bd_ref, mk_ref, mv_ref):
    x = mem_ref[0]
    h = (x * lax.rsqrt(jnp.mean(x * x, axis=-1, keepdims=True) + EPS) * g_ref[...]).astype(BF16)
    zk = _dot(h, wk_ref[...])
    ms = _dot((zk * zk).astype(BF16), bd_ref[...])
    mk_ref[0] = zk * lax.rsqrt(ms + EPS) * gk_ref[...]
    mv_ref[0] = _dot(h, wv_ref[...])


def _memkv_call(mem, g_mem, w_mk, w_mv, gk_t, bd256):
    b, n_mem, d = mem.shape
    full = lambda shape: pl.BlockSpec(shape, lambda i: (0,) * len(shape))
    return pl.pallas_call(
        _memkv_body,
        grid=(b,),
        in_specs=[pl.BlockSpec((1, n_mem, d), lambda i: (i, 0, 0)), full((1, d)), full((d, M_WIDTH)),
                  full((d, M_WIDTH)), full((1, M_WIDTH)), full((M_WIDTH, M_WIDTH))],
        out_specs=[pl.BlockSpec((1, n_mem, M_WIDTH), lambda i: (i, 0, 0))] * 2,
        out_shape=[jax.ShapeDtypeStruct((b, n_mem, M_WIDTH), F32)] * 2,
        compiler_params=pltpu.CompilerParams(dimension_semantics=("arbitrary",), vmem_limit_bytes=VMEM_LIMIT),
        name="mem_kv",
    )(mem, g_mem, w_mk, w_mv, gk_t, bd256)


def _proj_body(x_ref, prev_ref, mk_ref, mv_ref, gat_ref, w_ref, bf_ref, gq_ref, gk_ref, gqm_ref,
               bd512_ref, bd256_ref, pfq_ref, pfk_ref, cq_ref, ck_ref, cw_ref, cb_ref,
               ka_ref, va_ref, logf_ref, qaug_ref, kaug_ref, vb_ref, ob_ref, om_ref, cst_ref,
               fcarry, ucarry):
    t = pl.program_id(1)
    tt = x_ref.shape[1]
    x = x_ref[0]
    h = (x * lax.rsqrt(jnp.mean(x * x, axis=-1, keepdims=True) + EPS) * gat_ref[...]).astype(BF16)

    def proj(c0, c1):
        return _dot(h, w_ref[:, c0:c1])

    @pl.when(t == 0)
    def _():
        fcarry[...] = jnp.zeros_like(fcarry)
        ucarry[...] = jnp.zeros_like(ucarry)
        ucarry[SUBLANES - 2:SUBLANES, :] = prev_ref[0]

    v = proj(C_FL, C_END) + bf_ref[...]
    logf = jnp.minimum(v, 0.0) - jnp.log1p(jnp.exp(-jnp.abs(v)))
    logf = jnp.where(_lane(logf.shape) < A_HEADS, logf, 0.0)
    logf_ref[0] = logf[:, :A_HEADS]
    row = lax.broadcasted_iota(I32, (tt, tt), 0)
    col = lax.broadcasted_iota(I32, (tt, tt), 1)
    tri = jnp.where(row >= col, 1.0, 0.0).astype(BF16)
    lh, lm, ll = _split3(logf)
    fcum = _dot(tri, lh) + _dot(tri, lm) + _dot(tri, ll) + fcarry[0:1, :]
    fcarry[...] = jnp.broadcast_to(fcum[tt - 1:tt, :], fcarry.shape)
    fparts = jnp.concatenate(_split3(fcum), axis=1)
    faq = _dot(fparts, pfq_ref[...]) + cq_ref[...]
    fak = _dot(fparts, pfk_ref[...]) + ck_ref[...]

    lane = _lane((tt, LANES))
    zq = proj(C_Q, C_K)
    qn = zq * lax.rsqrt(_dot((zq * zq).astype(BF16), bd512_ref[...]) + EPS) * gq_ref[...]
    zk = proj(C_K, C_V)
    kn = zk * lax.rsqrt(_dot((zk * zk).astype(BF16), bd512_ref[...]) + EPS) * gk_ref[...]
    ka_ref[0] = kn
    for hd in range(A_HEADS):
        c0 = LANES * (hd // 2)
        qt, kt = qn[:, c0:c0 + LANES], kn[:, c0:c0 + LANES]
        if hd % 2:
            qt, kt = pltpu.roll(qt, HEAD_DIM, 1), pltpu.roll(kt, HEAD_DIM, 1)
        qaug_ref[0, hd] = jnp.where(lane < HEAD_DIM, qt, faq[:, LANES * hd:LANES * (hd + 1)]).astype(BF16)
        kaug_ref[0, hd] = jnp.where(lane < HEAD_DIM, kt, fak[:, LANES * hd:LANES * (hd + 1)]).astype(BF16)
    zv = proj(C_V, C_BG)
    va_ref[0] = zv
    vb_ref[0] = zv.astype(BF16)

    u = proj(C_CG, C_HV) * proj(C_HV, C_QM)
    rows = lax.broadcasted_iota(I32, u.shape, 0)
    p1 = ucarry[SUBLANES - 1:SUBLANES, :]
    p2 = ucarry[SUBLANES - 2:SUBLANES - 1, :]
    u1 = jnp.where(rows == 0, p1, pltpu.roll(u, 1, 0))
    u2 = jnp.where(rows == 0, p2, jnp.where(rows == 1, p1, pltpu.roll(u, 2, 0)))
    cy = cb_ref[...] + cw_ref[2:3, :] * u + cw_ref[0:1, :] * u2 + cw_ref[1:2, :] * u1
    ob_ref[0] = (proj(C_BG, C_CG) * cy).astype(BF16)
    ucarry[...] = u[tt - SUBLANES:tt, :]
    cst_ref[0] = u[tt - 2:tt, :]

    zm = proj(C_QM, C_FL)
    qm = zm * lax.rsqrt(_dot((zm * zm).astype(BF16), bd256_ref[...]) + EPS) * gqm_ref[...]
    mkb = mk_ref[0].astype(BF16)
    mvb = mv_ref[0].astype(BF16)
    outs = []
    for pr in range(M_HEADS // 2):
        qp = qm[:, LANES * pr:LANES * (pr + 1)]
        kp = mkb[:, LANES * pr:LANES * (pr + 1)]
        vp = mvb[:, LANES * pr:LANES * (pr + 1)]
        o = []
        for sub in range(2):
            keep = (lane < HEAD_DIM) if sub == 0 else (lane >= HEAD_DIM)
            s = _dot_nt(jnp.where(keep, qp, 0.0).astype(BF16), kp)
            p = jnp.exp(s - jnp.max(s, axis=-1, keepdims=True))
            o.append(_dot(p.astype(BF16), vp) / jnp.sum(p, axis=-1, keepdims=True))
        outs.append(jnp.where(lane < HEAD_DIM, o[0], o[1]))
    om_ref[0] = jnp.concatenate(outs, axis=1).astype(BF16)


def _proj_call(x, prev, mk, mv, wts, tile):
    b, s, d = x.shape
    nt = s // tile
    n_mem = mk.shape[1]
    full = lambda a: pl.BlockSpec(a.shape, lambda i, j: (0,) * a.ndim)
    seq = lambda w: pl.BlockSpec((1, tile, w), lambda i, j: (i, j, 0))
    per_b = lambda r, w: pl.BlockSpec((1, r, w), lambda i, j: (i, 0, 0))
    heads = pl.BlockSpec((1, A_HEADS, tile, LANES), lambda i, j: (i, 0, j, 0))
    names = ["g_attn", "w_in", "b_f", "g_q", "g_k", "g_qm", "bd512", "bd256", "pfq", "pfk", "cq", "ck", "conv_w", "conv_b"]
    consts = [wts[k] for k in names]
    out_shape = [
        jax.ShapeDtypeStruct((b, s, A_WIDTH), F32),
        jax.ShapeDtypeStruct((b, s, A_WIDTH), F32),
        jax.ShapeDtypeStruct((b, s, A_HEADS), F32),
        jax.ShapeDtypeStruct((b, A_HEADS, s, LANES), BF16),
        jax.ShapeDtypeStruct((b, A_HEADS, s, LANES), BF16),
        jax.ShapeDtypeStruct((b, s, A_WIDTH), BF16),
        jax.ShapeDtypeStruct((b, s, B_WIDTH), BF16),
        jax.ShapeDtypeStruct((b, s, M_WIDTH), BF16),
        jax.ShapeDtypeStruct((b, 2, B_WIDTH), F32),
    ]
    out_specs = [seq(A_WIDTH), seq(A_WIDTH), seq(A_HEADS), heads, heads, seq(A_WIDTH), seq(B_WIDTH), seq(M_WIDTH),
                 per_b(2, B_WIDTH)]
    return pl.pallas_call(
        _proj_body,
        grid=(b, nt),
        in_specs=[seq(d), per_b(2, B_WIDTH), per_b(n_mem, M_WIDTH), per_b(n_mem, M_WIDTH)] + [full(c) for c in consts],
        out_specs=out_specs,
        out_shape=out_shape,
        scratch_shapes=[pltpu.VMEM((SUBLANES, LANES), F32), pltpu.VMEM((SUBLANES, B_WIDTH), F32)],
        compiler_params=pltpu.CompilerParams(dimension_semantics=("arbitrary", "arbitrary"), vmem_limit_bytes=VMEM_LIMIT),
        name="proj",
    )(x, prev, mk, mv, *consts)


def _fox_prompt_body(q_ref, k_ref, v_ref, o_ref, *, blk):
    qi = pl.program_id(2)
    ones = jnp.ones((blk, LANES), BF16)
    row = lax.broadcasted_iota(I32, (blk, blk), 0)
    col = lax.broadcasted_iota(I32, (blk, blk), 1)
    outs = []
    for sub in range(2):
        q = q_ref[0, sub]

        def step(kj, carry, masked):
            m, acc = carry
            off = pl.multiple_of(kj * blk, blk)
            s = _dot_nt(q, k_ref[0, sub, pl.ds(off, blk), :])
            if masked:
                s = jnp.where(col <= row, s, NEG_BIG)
            m_new = jnp.maximum(m, jnp.max(s, axis=-1, keepdims=True))
            p = jnp.exp(s - m_new).astype(BF16)
            vv = jnp.concatenate([v_ref[0, pl.ds(off, blk), :], ones], axis=1)
            return m_new, jnp.exp(m - m_new) * acc + _dot(p, vv)

        init = (jnp.full((blk, 1), NEG_BIG, F32), jnp.zeros((blk, 2 * LANES), F32))
        carry = lax.fori_loop(0, qi, functools.partial(step, masked=False), init)
        _, acc = step(qi, carry, True)
        outs.append(acc[:, :LANES] / acc[:, LANES:])
    o_ref[0] = jnp.where(_lane((blk, LANES)) < HEAD_DIM, outs[0], outs[1]).astype(BF16)


def _fox_prompt_call(qaug, kaug, vb, blk):
    b, _, s, _ = qaug.shape
    return pl.pallas_call(
        functools.partial(_fox_prompt_body, blk=blk),
        grid=(b, A_HEADS // 2, s // blk),
        in_specs=[pl.BlockSpec((1, 2, blk, LANES), lambda i, hp, j: (i, hp, j, 0)),
                  pl.BlockSpec((1, 2, s, LANES), lambda i, hp, j: (i, hp, 0, 0)),
                  pl.BlockSpec((1, s, LANES), lambda i, hp, j: (i, 0, hp))],
        out_specs=pl.BlockSpec((1, blk, LANES), lambda i, hp, j: (i, j, hp)),
        out_shape=jax.ShapeDtypeStruct((b, s, A_WIDTH), BF16),
        compiler_params=pltpu.CompilerParams(dimension_semantics=("arbitrary",) * 3, vmem_limit_bytes=VMEM_LIMIT),
        name="fox_prompt",
    )(qaug, kaug, vb)


def _fox_sample_body(ck_ref, cv_ref, clf_ref, q_ref, k_ref, v_ref, pfk_ref, ckc_ref, o_ref):
    past = ck_ref.shape[1]
    ts = q_ref.shape[2]
    row = lax.broadcasted_iota(I32, (past, past), 0)
    col = lax.broadcasted_iota(I32, (past, past), 1)
    tri = jnp.where(col > row, 1.0, 0.0).astype(BF16)
    lh, lm, ll = _split3(clf_ref[0])
    suffix = _dot(tri, lh) + _dot(tri, lm) + _dot(tri, ll)
    fak = _dot(jnp.concatenate(_split3(-suffix), axis=1), pfk_ref[...]) + ckc_ref[...]
    lane = _lane((past, LANES))
    lane_s = _lane((ts, LANES))
    causal = lax.broadcasted_iota(I32, (ts, ts), 1) <= lax.broadcasted_iota(I32, (ts, ts), 0)
    outs = []
    for pr in range(A_HEADS // 2):
        kc2 = ck_ref[0, :, LANES * pr:LANES * (pr + 1)]
        vc = cv_ref[0, :, LANES * pr:LANES * (pr + 1)].astype(BF16)
        vn = v_ref[0, :, LANES * pr:LANES * (pr + 1)]
        o = []
        for sub in range(2):
            hd = 2 * pr + sub
            kt = pltpu.roll(kc2, HEAD_DIM, 1) if sub else kc2
            kc = jnp.where(lane < HEAD_DIM, kt, fak[:, LANES * hd:LANES * (hd + 1)]).astype(BF16)
            q = q_ref[0, hd]
            s1 = _dot_nt(q, kc)
            s2 = jnp.where(causal, _dot_nt(q, k_ref[0, hd]), NEG_BIG)
            m = jnp.maximum(jnp.max(s1, axis=-1, keepdims=True), jnp.max(s2, axis=-1, keepdims=True))
            p1, p2 = jnp.exp(s1 - m), jnp.exp(s2 - m)
            den = jnp.sum(p1, axis=-1, keepdims=True) + jnp.sum(p2, axis=-1, keepdims=True)
            o.append((_dot(p1.astype(BF16), vc) + _dot(p2.astype(BF16), vn)) / den)
        outs.append(jnp.where(lane_s < HEAD_DIM, o[0], o[1]))
    o_ref[0] = jnp.concatenate(outs, axis=1).astype(BF16)


def _fox_sample_call(cache_k, cache_v, cache_lf, qaug, kaug, vb, pfk, ck):
    b, past, _ = cache_k.shape
    ts = qaug.shape[2]
    full = lambda a: pl.BlockSpec(a.shape, lambda i: (0,) * a.ndim)
    return pl.pallas_call(
        _fox_sample_body,
        grid=(b,),
        in_specs=[pl.BlockSpec((1, past, A_WIDTH), lambda i: (i, 0, 0)),
                  pl.BlockSpec((1, past, A_WIDTH), lambda i: (i, 0, 0)),
                  pl.BlockSpec((1, past, LANES), lambda i: (i, 0, 0)),
                  pl.BlockSpec((1, A_HEADS, ts, LANES), lambda i: (i, 0, 0, 0)),
                  pl.BlockSpec((1, A_HEADS, ts, LANES), lambda i: (i, 0, 0, 0)),
                  pl.BlockSpec((1, ts, A_WIDTH), lambda i: (i, 0, 0)),
                  full(pfk), full(ck)],
        out_specs=pl.BlockSpec((1, ts, A_WIDTH), lambda i: (i, 0, 0)),
        out_shape=jax.ShapeDtypeStruct((b, ts, A_WIDTH), BF16),
        compiler_params=pltpu.CompilerParams(dimension_semantics=("arbitrary",), vmem_limit_bytes=VMEM_LIMIT),
        name="fox_sample",
    )(cache_k, cache_v, cache_lf, qaug, kaug, vb, pfk, ck)


def _all_sublanes(x, op):
    for shift in (4, 2, 1):
        x = op(x, pltpu.roll(x, shift, 0))
    return x


def _top_groups(groups, payload, k, sub):
    groups = list(groups)
    big = SUBLANES * len(groups)
    out = []
    for _ in range(k):
        level = [(v, g) for g, v in enumerate(groups)]
        while len(level) > 1:
            merged = []
            for (va, ga), (vb, gb) in zip(level[0::2], level[1::2]):
                take = vb > va
                merged.append((jnp.where(take, vb, va), jnp.where(take, gb, ga)))
            level = merged + ([level[-1]] if len(level) % 2 else [])
        best, where_g = level[0]
        top = _all_sublanes(best, jnp.maximum)
        row = _all_sublanes(jnp.where(best == top, where_g * SUBLANES + sub, big), jnp.minimum)
        rel = row - sub
        hits = [rel == SUBLANES * g for g in range(len(groups))]
        if payload is None:
            out.append((top, row))
        else:
            pay = jnp.where(hits[0], payload[0], -1)
            for g in range(1, len(groups)):
                pay = jnp.where(hits[g], payload[g], pay)
            out.append((top, _all_sublanes(pay, jnp.maximum)))
        groups = [jnp.where(h, -jnp.inf, v) for h, v in zip(hits, groups)]
    return out


def _pack_rows(rows, sub2):
    acc = jnp.zeros(sub2.shape, rows[0].dtype)
    for j, r in enumerate(rows):
        acc = jnp.where(sub2 == j, jnp.concatenate([r, r], axis=0), acc)
    return acc


def _route_head(sc, sub, sub2):
    groups = lambda a: [a[SUBLANES * g:SUBLANES * (g + 1)] for g in range(a.shape[0] // SUBLANES)]
    t1 = _top_groups(groups(sc[:PEER_KEYS]), None, PEER_TOPK, sub)
    t2 = _top_groups(groups(sc[PEER_KEYS:]), None, PEER_TOPK, sub)
    s1, i1 = [v for v, _ in t1], [i * PEER_KEYS for _, i in t1]
    s2p, i2p = _pack_rows([v for v, _ in t2], sub2), _pack_rows([i for _, i in t2], sub2)
    s1p, i1p = _pack_rows(s1, sub2), _pack_rows(i1, sub2)
    lo, hi = slice(0, SUBLANES), slice(SUBLANES, 2 * SUBLANES)
    half = sub < 4
    rep4 = lambda a: jnp.where(half, a[lo], pltpu.roll(a[lo], 4, 0))
    pick = lambda a, b: jnp.where(half, a, b)

    def pairs(first, first_packed, second_packed, second0):
        return [first[0] + second_packed[lo], first[0] + second_packed[hi], first[1] + second_packed[lo],
                first[2] + second_packed[lo], first[3] + second_packed[lo],
                pick(first[4], first[5]) + rep4(second_packed), pick(first[6], first[7]) + rep4(second_packed),
                first_packed[hi] + second0]

    cand = pairs(s1, s1p, s2p, t2[0][0])
    expert = pairs(i1, i1p, i2p, t2[0][1])
    best = _top_groups(cand, expert, PEER_TOPK, sub)
    top = _pack_rows([v for v, _ in best], sub2)
    p = jnp.exp(top - best[0][0][0:1, :])
    gates = p / jnp.sum(p, axis=0, keepdims=True)
    return _pack_rows([e for _, e in best], sub2) * ROW_SUB, gates


def _route_body(x_ref, oa_ref, ob_ref, om_ref, wo_ref, gf_ref, wq_ref, keys_ref,
                x1_ref, xn_ref, e_ref, g_ref, qp_scr, e_scr, g_scr):
    tile = x_ref.shape[0]
    lane_tiles = tile // LANES
    y = x_ref[...] + _dot(oa_ref[...], wo_ref[0:A_WIDTH, :]) \
        + _dot(ob_ref[...], wo_ref[A_WIDTH:A_WIDTH + B_WIDTH, :]) \
        + _dot(om_ref[...], wo_ref[A_WIDTH + B_WIDTH:, :])
    x1_ref[...] = y
    xn = y * lax.rsqrt(jnp.mean(y * y, axis=-1, keepdims=True) + EPS) * gf_ref[...]
    xn_ref[...] = xn
    qp = _dot(xn.astype(BF16), wq_ref[...])
    for hd in range(PEER_HEADS):
        qp_scr[hd] = qp[:, LANES * hd:LANES * (hd + 1)].astype(BF16)
    sub = lax.broadcasted_iota(I32, (SUBLANES, LANES), 0)
    sub2 = lax.broadcasted_iota(I32, (2 * SUBLANES, LANES), 0)

    def head(hd, carry):
        for lt in range(lane_tiles):
            q = qp_scr[hd, LANES * lt:LANES * (lt + 1), :]
            e, gates = _route_head(_dot_nt(keys_ref[hd], q), sub, sub2)
            e_scr[lt, hd] = e
            g_scr[lt, hd] = gates
        return carry

    lax.fori_loop(0, PEER_HEADS, head, 0)
    for lt in range(lane_tiles):
        rows = slice(LANES * lt, LANES * (lt + 1))
        e_ref[rows, :] = jnp.concatenate([e_scr[lt, hd] for hd in range(PEER_HEADS)], axis=0).T
        g_ref[rows, :] = jnp.concatenate([g_scr[lt, hd] for hd in range(PEER_HEADS)], axis=0).T


def _route_call(x, oa, ob, om, wts, tile):
    n, d = x.shape
    full = lambda a: pl.BlockSpec(a.shape, lambda i: (0,) * a.ndim)
    rows = lambda w: pl.BlockSpec((tile, w), lambda i: (i, 0))
    consts = [wts[k] for k in ["w_out", "g_ffn", "w_pq", "keys"]]
    return pl.pallas_call(
        _route_body,
        grid=(n // tile,),
        in_specs=[rows(d), rows(A_WIDTH), rows(B_WIDTH), rows(M_WIDTH)] + [full(c) for c in consts],
        out_specs=[rows(d), rows(d), rows(PEER_SLOTS), rows(PEER_SLOTS)],
        out_shape=[jax.ShapeDtypeStruct((n, d), F32), jax.ShapeDtypeStruct((n, d), F32),
                   jax.ShapeDtypeStruct((n, PEER_SLOTS), I32), jax.ShapeDtypeStruct((n, PEER_SLOTS), F32)],
        scratch_shapes=[pltpu.VMEM((PEER_HEADS, tile, LANES), BF16),
                        pltpu.VMEM((tile // LANES, PEER_HEADS, PEER_TOPK, LANES), I32),
                        pltpu.VMEM((tile // LANES, PEER_HEADS, PEER_TOPK, LANES), F32)],
        compiler_params=pltpu.CompilerParams(dimension_semantics=("arbitrary",), vmem_limit_bytes=VMEM_LIMIT),
        name="merge_route",
    )(x, oa, ob, om, *consts)


GROUP_ROWS = PEER_SLOTS * ROW_SUB
WIDE = PEER_SLOTS * SUBLANES
PIPE_TOKENS = 32


def _gather_token(idx_ref, row, tab_ref, buf):
    for k in range(PEER_SLOTS):
        buf[k * ROW_SUB:(k + 1) * ROW_SUB, :] = tab_ref[pl.ds(pl.multiple_of(idx_ref[row, k], ROW_SUB), ROW_SUB), :]


def _pipelined_tokens(e_hbm, tab_ref, bufs, idx, sems, compute, tile):
    blocks = tile // PIPE_TOKENS
    first_block = pl.program_id(0) * blocks
    all_blocks = pl.num_programs(0) * blocks

    def idx_copy(block, slot):
        rows = pl.ds(pl.multiple_of(block * PIPE_TOKENS, PIPE_TOKENS), PIPE_TOKENS)
        return pltpu.make_async_copy(e_hbm.at[rows, :], idx[slot], sems.at[slot])

    @pl.when(pl.program_id(0) == 0)
    def _():
        idx_copy(0, 0).start()
        idx_copy(1, 1).start()
        idx_copy(0, 0).wait()
        _gather_token(idx[0], 0, tab_ref, bufs[0])
        _gather_token(idx[0], 1, tab_ref, bufs[1])

    def block_pair(p, carry):
        for slot in range(2):
            local = 2 * p + slot
            block = first_block + local
            for half in range(PIPE_TOKENS // 2):
                cur = bufs[2 * (half % 2):2 * (half % 2) + 2]
                nxt = bufs[2 - 2 * (half % 2):4 - 2 * (half % 2)]
                for j in range(2):
                    compute(local * PIPE_TOKENS + 2 * half + j, cur[j])
                if half + 1 < PIPE_TOKENS // 2:
                    for j in range(2):
                        _gather_token(idx[slot], 2 * half + 2 + j, tab_ref, nxt[j])
                else:
                    @pl.when(block + 1 < all_blocks)
                    def _():
                        idx_copy(block + 1, 1 - slot).wait()
                        for j in range(2):
                            _gather_token(idx[1 - slot], j, tab_ref, nxt[j])

            @pl.when(block + 2 < all_blocks)
            def _():
                idx_copy(block + 2, slot).start()
        return carry

    lax.fori_loop(0, blocks // 2, block_pair, 0)


def _pipeline_scratch():
    return ([pltpu.VMEM((GROUP_ROWS, LANES), I32)] * 4 + [pltpu.SMEM((PIPE_TOKENS, PEER_SLOTS), I32)] * 2
            + [pltpu.SemaphoreType.DMA((2,))])


def _token_tile_rows(t):
    return pl.ds(pl.multiple_of(t * SUBLANES, SUBLANES), SUBLANES)


def _split2(x):
    hi = x.astype(BF16).astype(F32)
    return jnp.concatenate([hi, x - hi], axis=0).astype(BF16)


def _peer_u_body(e_hbm, x_ref, g_ref, u_ref, col_ref, mask_ref, w_ref, b0, b1, b2, b3, i0, i1, sems, z_scr, *, tile, chunk):
    mask = mask_ref[...]

    def compute(t, buf):
        z = _dot_nt(_split2(x_ref[_token_tile_rows(t), :]), pltpu.bitcast(buf[...], BF16))
        z_scr[_token_tile_rows(t), :] = (z[:SUBLANES] + z[SUBLANES:]) * mask

    _pipelined_tokens(e_hbm, u_ref, (b0, b1, b2, b3), (i0, i1), sems, compute, tile)
    for c in range(tile // chunk):
        zs = z_scr[c * chunk * SUBLANES:(c + 1) * chunk * SUBLANES, :]
        zh = zs.astype(BF16)
        zl = (zs - zh.astype(F32)).astype(BF16)
        part = _dot(zh, col_ref[...]) + _dot(zl, col_ref[...])
        a = jnp.sum(part.reshape(chunk, SUBLANES, PEER_SLOTS), axis=1)
        rows = slice(c * chunk, (c + 1) * chunk)
        w_ref[rows, :] = g_ref[rows, :] * (0.5 * a * (1.0 + lax.erf(a * np.float32(np.sqrt(0.5)))))


def _peer_u_call(e, xn8, gates, u_packed, consts, tile):
    n = e.shape[0]
    full = lambda a: pl.BlockSpec(a.shape, lambda i: (0,) * a.ndim)
    return pl.pallas_call(
        functools.partial(_peer_u_body, tile=tile, chunk=min(tile, 32)),
        grid=(n // tile,),
        in_specs=[pl.BlockSpec(memory_space=pl.ANY),
                  pl.BlockSpec((tile * SUBLANES, LANES), lambda i: (i, 0)),
                  pl.BlockSpec((tile, PEER_SLOTS), lambda i: (i, 0)),
                  pl.BlockSpec(memory_space=pltpu.VMEM),
                  full(consts["collapse"]), full(consts["mask8"])],
        out_specs=pl.BlockSpec((tile, PEER_SLOTS), lambda i: (i, 0)),
        out_shape=jax.ShapeDtypeStruct((n, PEER_SLOTS), F32),
        scratch_shapes=_pipeline_scratch() + [pltpu.VMEM((tile * SUBLANES, WIDE), F32)],
        compiler_params=pltpu.CompilerParams(dimension_semantics=("arbitrary",), vmem_limit_bytes=VMEM_LIMIT),
        name="peer_u",
    )(e, xn8, gates, u_packed, consts["collapse"], consts["mask8"])


def _peer_v_body(e_hbm, w_ref, x_ref, v_ref, exp_ref, mask_ref, y_ref, b0, b1, b2, b3, i0, i1, sems, eh_scr, el_scr, *, tile):
    mask = mask_ref[...]
    w = w_ref[...]
    wh = w.astype(BF16)
    eh_scr[...] = _dot(wh, exp_ref[...])
    el_scr[...] = _dot((w - wh.astype(F32)).astype(BF16), exp_ref[...])

    def compute(t, buf):
        row = pl.ds(t, 1)
        lhs = jnp.concatenate([jnp.broadcast_to(eh_scr[row, :], (SUBLANES, WIDE)) * mask,
                               jnp.broadcast_to(el_scr[row, :], (SUBLANES, WIDE)) * mask], axis=0).astype(BF16)
        o = _dot(lhs, pltpu.bitcast(buf[...], BF16))
        rows = _token_tile_rows(t)
        y_ref[rows, :] = x_ref[rows, :] + o[:SUBLANES] + o[SUBLANES:]

    _pipelined_tokens(e_hbm, v_ref, (b0, b1, b2, b3), (i0, i1), sems, compute, tile)


def _peer_v_call(e, w, x8, v_packed, consts, tile):
    n = e.shape[0]
    full = lambda a: pl.BlockSpec(a.shape, lambda i: (0,) * a.ndim)
    return pl.pallas_call(
        functools.partial(_peer_v_body, tile=tile),
        grid=(n // tile,),
        in_specs=[pl.BlockSpec(memory_space=pl.ANY),
                  pl.BlockSpec((tile, PEER_SLOTS), lambda i: (i, 0)),
                  pl.BlockSpec((tile * SUBLANES, LANES), lambda i: (i, 0)),
                  pl.BlockSpec(memory_space=pltpu.VMEM),
                  full(consts["expand"]), full(consts["mask8"])],
        out_specs=pl.BlockSpec((tile * SUBLANES, LANES), lambda i: (i, 0)),
        out_shape=jax.ShapeDtypeStruct((n * SUBLANES, LANES), F32),
        scratch_shapes=_pipeline_scratch() + [pltpu.VMEM((tile, WIDE), F32), pltpu.VMEM((tile, WIDE), F32)],
        compiler_params=pltpu.CompilerParams(dimension_semantics=("arbitrary",), vmem_limit_bytes=VMEM_LIMIT),
        name="peer_v",
    )(e, w, x8, v_packed, consts["expand"], consts["mask8"])


def _peer_constants():
    lane = np.arange(WIDE)
    expand = (lane[None, :] // SUBLANES == np.arange(PEER_SLOTS)[:, None]).astype(np.float32)
    mask8 = (lane[None, :] % SUBLANES == np.arange(SUBLANES)[:, None]).astype(np.float32)
    return {"expand": jnp.asarray(expand, BF16), "collapse": jnp.asarray(expand.T, BF16), "mask8": jnp.asarray(mask8)}


def _pack_table(tab):
    n = tab.shape[0]
    b = lax.bitcast_convert_type(tab.astype(BF16), jnp.uint16).astype(jnp.uint32).reshape(n, ROW_SUB, 2, LANES)
    packed = b[:, :, 0, :] | (b[:, :, 1, :] << 16)
    return lax.bitcast_convert_type(packed, I32).reshape(n * ROW_SUB, LANES)


def _block_mean(width):
    blk = np.arange(width) // HEAD_DIM
    return jnp.asarray((blk[:, None] == blk[None, :]) / HEAD_DIM, BF16)


def _forget_placement():
    pfq = np.zeros((3 * LANES, A_HEADS * LANES), np.float32)
    pfk = np.zeros((3 * LANES, A_HEADS * LANES), np.float32)
    cq = np.zeros((1, A_HEADS * LANES), np.float32)
    ck = np.zeros((1, A_HEADS * LANES), np.float32)
    for hd in range(A_HEADS):
        base = hd * LANES + HEAD_DIM
        for piece in range(3):
            pfq[piece * LANES + hd, base + piece] = 1.0
            pfk[piece * LANES + hd, base + 3 + piece] = -1.0
            cq[0, base + 3 + piece] = 1.0
            ck[0, base + piece] = 1.0
    return jnp.asarray(pfq, BF16), jnp.asarray(pfk, BF16), jnp.asarray(cq), jnp.asarray(ck)


def _layer_weights(l, g_attn, w_in, b_f, g_q_a, g_k_a, conv_w, conv_b, g_q_m, w_out, g_ffn, w_peer_q, peer_keys):
    scale = HEAD_DIM ** -0.5
    splits = np.cumsum([A_WIDTH, A_WIDTH, A_WIDTH, A_HEADS, B_WIDTH, B_WIDTH, B_WIDTH, M_WIDTH])
    wi = w_in[l]
    qa, ka, va, fl, bg, cg, hv, qm = [wi[:, a:b] for a, b in zip(np.r_[0, splits[:-1]], splits)]
    w_packed = jnp.concatenate([qa, ka, va, bg, cg, hv, qm, fl, jnp.zeros((wi.shape[0], LANES - A_HEADS), wi.dtype)],
                               axis=1).astype(BF16)
    pfq, pfk, cq, ck = _forget_placement()
    keys = peer_keys[l]
    zeros = jnp.zeros_like(keys[:, 0])
    keys2 = jnp.concatenate([jnp.concatenate([keys[:, 0], zeros], axis=-1),
                             jnp.concatenate([zeros, keys[:, 1]], axis=-1)], axis=1).astype(BF16)
    return {
        "g_attn": g_attn[l][None, :], "w_in": w_packed,
        "b_f": jnp.pad(b_f[l], (0, LANES - A_HEADS))[None, :],
        "g_q": (jnp.tile(g_q_a[l], A_HEADS) * scale)[None, :], "g_k": jnp.tile(g_k_a[l], A_HEADS)[None, :],
        "g_qm": (jnp.tile(g_q_m[l], M_HEADS) * scale)[None, :],
        "bd512": _block_mean(A_WIDTH), "bd256": _block_mean(M_WIDTH),
        "pfq": pfq, "pfk": pfk, "cq": cq, "ck": ck,
        "conv_w": conv_w[l], "conv_b": conv_b[l][None, :],
        "w_out": w_out[l].astype(BF16), "g_ffn": g_ffn[l][None, :], "w_pq": w_peer_q[l].astype(BF16), "keys": keys2,
    }


def _peer_and_merge(x, oa, ob, om, wts, u_packed, v_packed, route_tile, peer_tile):
    b, s, d = x.shape
    n = b * s
    x1, xn, e, gates = _route_call(x.reshape(n, d), oa.reshape(n, -1), ob.reshape(n, -1), om.reshape(n, -1), wts, route_tile)
    consts = _peer_constants()
    w = _peer_u_call(e, xn.reshape(n * SUBLANES, LANES), gates, u_packed, consts, peer_tile)
    y8 = _peer_v_call(e, w, x1.reshape(n * SUBLANES, LANES), v_packed, consts, peer_tile)
    return y8.reshape(b, s, d)


def kernel(x_prompt, x_sample, cache_a_k, cache_a_v, cache_a_logf, cache_b_conv, cache_m_k, cache_m_v, mem_prompt, g_attn, w_in, b_f, g_q_a, g_k_a, conv_w, conv_b, g_mem, w_mem_k, w_mem_v, g_k_m, g_q_m, w_out, g_ffn, w_peer_q, peer_keys, peer_u, peer_v):
    depth = w_in.shape[0]
    xp, xs = x_prompt, x_sample
    bp, sp, _ = xp.shape
    bs, ts, _ = xs.shape
    outs = [[] for _ in range(10)]
    for l in range(depth):
        wts = _layer_weights(l, g_attn, w_in, b_f, g_q_a, g_k_a, conv_w, conv_b, g_q_m, w_out, g_ffn, w_peer_q, peer_keys)
        u_packed, v_packed = _pack_table(peer_u[l]), _pack_table(peer_v[l])
        fox_blk = min(512, sp)
        route_tile = 256
        peer_tile = 128

        mk, mv = _memkv_call(mem_prompt, g_mem[l][None, :], w_mem_k[l].astype(BF16), w_mem_v[l].astype(BF16),
                             jnp.tile(g_k_m[l], M_HEADS)[None, :], wts["bd256"])
        ka, va, logf, qaug, kaug, vb, ob, om, cst = _proj_call(
            xp, jnp.zeros((bp, 2, B_WIDTH), F32), mk, mv, wts, min(512, sp))
        oa = _fox_prompt_call(qaug, kaug, vb, fox_blk)
        xp = _peer_and_merge(xp, oa, ob, om, wts, u_packed, v_packed, route_tile, peer_tile)
        n_mem = mk.shape[1]
        for dst, val in zip(outs[:6], [ka.reshape(bp, sp, A_HEADS, HEAD_DIM), va.reshape(bp, sp, A_HEADS, HEAD_DIM), logf, cst,
                                       mk.reshape(bp, n_mem, M_HEADS, HEAD_DIM), mv.reshape(bp, n_mem, M_HEADS, HEAD_DIM)]):
            dst.append(val)

        past = cache_a_k.shape[2]
        ka, va, logf, qaug, kaug, vb, ob, om, cst = _proj_call(
            xs, cache_b_conv[l], cache_m_k[l].reshape(bs, -1, M_WIDTH), cache_m_v[l].reshape(bs, -1, M_WIDTH), wts, ts)
        clf = jnp.pad(cache_a_logf[l], ((0, 0), (0, 0), (0, LANES - A_HEADS)))
        oa = _fox_sample_call(cache_a_k[l].reshape(bs, past, A_WIDTH), cache_a_v[l].reshape(bs, past, A_WIDTH), clf,
                              qaug, kaug, vb, wts["pfk"], wts["ck"])
        xs = _peer_and_merge(xs, oa, ob, om, wts, u_packed, v_packed, route_tile, peer_tile)
        for dst, val in zip(outs[6:], [ka.reshape(bs, ts, A_HEADS, HEAD_DIM), va.reshape(bs, ts, A_HEADS, HEAD_DIM), logf, cst]):
            dst.append(val)

    return (xp, xs) + tuple(jnp.stack(o) for o in outs)
```

```python
import functools

import numpy as np
import jax
import jax.numpy as jnp
from jax import lax
from jax.experimental import pallas as pl
from jax.experimental.pallas import tpu as pltpu

F32, BF16, I32 = jnp.float32, jnp.bfloat16, jnp.int32
EPS = 1e-6
LANES = 128
SUBLANES = 8
HEAD_DIM = 64
A_HEADS = 8
A_WIDTH = A_HEADS * HEAD_DIM
B_WIDTH = 256
M_HEADS = 4
M_WIDTH = M_HEADS * HEAD_DIM
PEER_HEADS = 8
PEER_KEYS = 128
PEER_TOPK = 16
PEER_SLOTS = PEER_HEADS * PEER_TOPK
D_MODEL = 1024
ROW_WORDS = D_MODEL // 2
ROW_SUB = ROW_WORDS // LANES
NEG_BIG = -1e30
VMEM_LIMIT = 56 * 1024 * 1024

C_Q, C_K, C_V, C_BG, C_CG, C_HV, C_QM, C_FL, C_END = 0, 512, 1024, 1536, 1792, 2048, 2304, 2560, 2688


def _dot(a, b):
    return jnp.dot(a, b, preferred_element_type=F32)


def _dot_nt(a, b):
    return lax.dot_general(a, b, (((1,), (1,)), ((), ())), preferred_element_type=F32)


def _split3(x):
    hi = x.astype(BF16)
    r1 = x - hi.astype(F32)
    mid = r1.astype(BF16)
    lo = (r1 - mid.astype(F32)).astype(BF16)
    return hi, mid, lo


def _lane(shape):
    return lax.broadcasted_iota(I32, shape, len(shape) - 1)


def _memkv_body(mem_ref, g_ref, wk_ref, wv_ref, gk_ref, bd_ref, mk_ref, mv_ref):
    x = mem_ref[0]
    h = (x * lax.rsqrt(jnp.mean(x * x, axis=-1, keepdims=True) + EPS) * g_ref[...]).astype(BF16)
    zk = _dot(h, wk_ref[...])
    ms = _dot((zk * zk).astype(BF16), bd_ref[...])
    mk_ref[0] = zk * lax.rsqrt(ms + EPS) * gk_ref[...]
    mv_ref[0] = _dot(h, wv_ref[...])


def _memkv_call(mem, g_mem, w_mk, w_mv, gk_t, bd256):
    b, n_mem, d = mem.shape
    full = lambda shape: pl.BlockSpec(shape, lambda i: (0,) * len(shape))
    return pl.pallas_call(
        _memkv_body,
        grid=(b,),
        in_specs=[pl.BlockSpec((1, n_mem, d), lambda i: (i, 0, 0)), full((1, d)), full((d, M_WIDTH)),
                  full((d, M_WIDTH)), full((1, M_WIDTH)), full((M_WIDTH, M_WIDTH))],
        out_specs=[pl.BlockSpec((1, n_mem, M_WIDTH), lambda i: (i, 0, 0))] * 2,
        out_shape=[jax.ShapeDtypeStruct((b, n_mem, M_WIDTH), F32)] * 2,
        compiler_params=pltpu.CompilerParams(dimension_semantics=("arbitrary",), vmem_limit_bytes=VMEM_LIMIT),
        name="mem_kv",
    )(mem, g_mem, w_mk, w_mv, gk_t, bd256)


def _proj_body(x_ref, prev_ref, mk_ref, mv_ref, gat_ref, w_ref, bf_ref, gq_ref, gk_ref, gqm_ref,
               bd512_ref, bd256_ref, pfq_ref, pfk_ref, cq_ref, ck_ref, cw_ref, cb_ref,
               ka_ref, va_ref, logf_ref, qaug_ref, kaug_ref, vb_ref, ob_ref, om_ref, cst_ref,
               fcarry, ucarry):
    t = pl.program_id(1)
    tt = x_ref.shape[1]
    x = x_ref[0]
    h = (x * lax.rsqrt(jnp.mean(x * x, axis=-1, keepdims=True) + EPS) * gat_ref[...]).astype(BF16)

    def proj(c0, c1):
        return _dot(h, w_ref[:, c0:c1])

    @pl.when(t == 0)
    def _():
        fcarry[...] = jnp.zeros_like(fcarry)
        ucarry[...] = jnp.zeros_like(ucarry)
        ucarry[SUBLANES - 2:SUBLANES, :] = prev_ref[0]

    v = proj(C_FL, C_END) + bf_ref[...]
    logf = jnp.minimum(v, 0.0) - jnp.log1p(jnp.exp(-jnp.abs(v)))
    logf = jnp.where(_lane(logf.shape) < A_HEADS, logf, 0.0)
    logf_ref[0] = logf[:, :A_HEADS]
    row = lax.broadcasted_iota(I32, (tt, tt), 0)
    col = lax.broadcasted_iota(I32, (tt, tt), 1)
    tri = jnp.where(row >= col, 1.0, 0.0).astype(BF16)
    lh, lm, ll = _split3(logf)
    fcum = _dot(tri, lh) + _dot(tri, lm) + _dot(tri, ll) + fcarry[0:1, :]
    fcarry[...] = jnp.broadcast_to(fcum[tt - 1:tt, :], fcarry.shape)
    fparts = jnp.concatenate(_split3(fcum), axis=1)
    faq = _dot(fparts, pfq_ref[...]) + cq_ref[...]
    fak = _dot(fparts, pfk_ref[...]) + ck_ref[...]

    lane = _lane((tt, LANES))
    zq = proj(C_Q, C_K)
    qn = zq * lax.rsqrt(_dot((zq * zq).astype(BF16), bd512_ref[...]) + EPS) * gq_ref[...]
    zk = proj(C_K, C_V)
    kn = zk * lax.rsqrt(_dot((zk * zk).astype(BF16), bd512_ref[...]) + EPS) * gk_ref[...]
    ka_ref[0] = kn
    for hd in range(A_HEADS):
        c0 = LANES * (hd // 2)
        qt, kt = qn[:, c0:c0 + LANES], kn[:, c0:c0 + LANES]
        if hd % 2:
            qt, kt = pltpu.roll(qt, HEAD_DIM, 1), pltpu.roll(kt, HEAD_DIM, 1)
        qaug_ref[0, hd] = jnp.where(lane < HEAD_DIM, qt, faq[:, LANES * hd:LANES * (hd + 1)]).astype(BF16)
        kaug_ref[0, hd] = jnp.where(lane < HEAD_DIM, kt, fak[:, LANES * hd:LANES * (hd + 1)]).astype(BF16)
    zv = proj(C_V, C_BG)
    va_ref[0] = zv
    vb_ref[0] = zv.astype(BF16)

    u = proj(C_CG, C_HV) * proj(C_HV, C_QM)
    rows = lax.broadcasted_iota(I32, u.shape, 0)
    p1 = ucarry[SUBLANES - 1:SUBLANES, :]
    p2 = ucarry[SUBLANES - 2:SUBLANES - 1, :]
    u1 = jnp.where(rows == 0, p1, pltpu.roll(u, 1, 0))
    u2 = jnp.where(rows == 0, p2, jnp.where(rows == 1, p1, pltpu.roll(u, 2, 0)))
    cy = cb_ref[...] + cw_ref[2:3, :] * u + cw_ref[0:1, :] * u2 + cw_ref[1:2, :] * u1
    ob_ref[0] = (proj(C_BG, C_CG) * cy).astype(BF16)
    ucarry[...] = u[tt - SUBLANES:tt, :]
    cst_ref[0] = u[tt - 2:tt, :]

    zm = proj(C_QM, C_FL)
    qm = zm * lax.rsqrt(_dot((zm * zm).astype(BF16), bd256_ref[...]) + EPS) * gqm_ref[...]
    mkb = mk_ref[0].astype(BF16)
    mvb = mv_ref[0].astype(BF16)
    outs = []
    for pr in range(M_HEADS // 2):
        qp = qm[:, LANES * pr:LANES * (pr + 1)]
        kp = mkb[:, LANES * pr:LANES * (pr + 1)]
        vp = mvb[:, LANES * pr:LANES * (pr + 1)]
        o = []
        for sub in range(2):
            keep = (lane < HEAD_DIM) if sub == 0 else (lane >= HEAD_DIM)
            s = _dot_nt(jnp.where(keep, qp, 0.0).astype(BF16), kp)
            p = jnp.exp(s - jnp.max(s, axis=-1, keepdims=True))
            o.append(_dot(p.astype(BF16), vp) / jnp.sum(p, axis=-1, keepdims=True))
        outs.append(jnp.where(lane < HEAD_DIM, o[0], o[1]))
    om_ref[0] = jnp.concatenate(outs, axis=1).astype(BF16)


def _proj_call(x, prev, mk, mv, wts, tile):
    b, s, d = x.shape
    nt = s // tile
    n_mem = mk.shape[1]
    full = lambda a: pl.BlockSpec(a.shape, lambda i, j: (0,) * a.ndim)
    seq = lambda w: pl.BlockSpec((1, tile, w), lambda i, j: (i, j, 0))
    per_b = lambda r, w: pl.BlockSpec((1, r, w), lambda i, j: (i, 0, 0))
    heads = pl.BlockSpec((1, A_HEADS, tile, LANES), lambda i, j: (i, 0, j, 0))
    names = ["g_attn", "w_in", "b_f", "g_q", "g_k", "g_qm", "bd512", "bd256", "pfq", "pfk", "cq", "ck", "conv_w", "conv_b"]
    consts = [wts[k] for k in names]
    out_shape = [
        jax.ShapeDtypeStruct((b, s, A_WIDTH), F32),
        jax.ShapeDtypeStruct((b, s, A_WIDTH), F32),
        jax.ShapeDtypeStruct((b, s, A_HEADS), F32),
        jax.ShapeDtypeStruct((b, A_HEADS, s, LANES), BF16),
        jax.ShapeDtypeStruct((b, A_HEADS, s, LANES), BF16),
        jax.ShapeDtypeStruct((b, s, A_WIDTH), BF16),
        jax.ShapeDtypeStruct((b, s, B_WIDTH), BF16),
        jax.ShapeDtypeStruct((b, s, M_WIDTH), BF16),
        jax.ShapeDtypeStruct((b, 2, B_WIDTH), F32),
    ]
    out_specs = [seq(A_WIDTH), seq(A_WIDTH), seq(A_HEADS), heads, heads, seq(A_WIDTH), seq(B_WIDTH), seq(M_WIDTH),
                 per_b(2, B_WIDTH)]
    return pl.pallas_call(
        _proj_body,
        grid=(b, nt),
        in_specs=[seq(d), per_b(2, B_WIDTH), per_b(n_mem, M_WIDTH), per_b(n_mem, M_WIDTH)] + [full(c) for c in consts],
        out_specs=out_specs,
        out_shape=out_shape,
        scratch_shapes=[pltpu.VMEM((SUBLANES, LANES), F32), pltpu.VMEM((SUBLANES, B_WIDTH), F32)],
        compiler_params=pltpu.CompilerParams(dimension_semantics=("arbitrary", "arbitrary"), vmem_limit_bytes=VMEM_LIMIT),
        name="proj",
    )(x, prev, mk, mv, *consts)


def _fox_prompt_body(q_ref, k_ref, v_ref, o_ref, *, blk):
    qi = pl.program_id(2)
    ones = jnp.ones((blk, LANES), BF16)
    row = lax.broadcasted_iota(I32, (blk, blk), 0)
    col = lax.broadcasted_iota(I32, (blk, blk), 1)
    def step(kj, carry, masked):
        off = pl.multiple_of(kj * blk, blk)
        vv = jnp.concatenate([v_ref[0, pl.ds(off, blk), :], ones], axis=1)
        new = []
        for sub in range(2):
            m, acc = carry[sub]
            s = _dot_nt(q_ref[0, sub], k_ref[0, sub, pl.ds(off, blk), :])
            if masked:
                s = jnp.where(col <= row, s, NEG_BIG)
            m_new = jnp.maximum(m, jnp.max(s, axis=-1, keepdims=True))
            p = jnp.exp(s - m_new).astype(BF16)
            new.append((m_new, jnp.exp(m - m_new) * acc + _dot(p, vv)))
        return tuple(new)

    init = ((jnp.full((blk, 1), NEG_BIG, F32), jnp.zeros((blk, 2 * LANES), F32)),) * 2
    carry = lax.fori_loop(0, qi, functools.partial(step, masked=False), init)
    (_, acc0), (_, acc1) = step(qi, carry, True)
    outs = [acc[:, :LANES] / acc[:, LANES:] for acc in (acc0, acc1)]
    o_ref[0] = jnp.where(_lane((blk, LANES)) < HEAD_DIM, outs[0], outs[1]).astype(BF16)


def _fox_prompt_call(qaug, kaug, vb, blk):
    b, _, s, _ = qaug.shape
    return pl.pallas_call(
        functools.partial(_fox_prompt_body, blk=blk),
        grid=(b, A_HEADS // 2, s // blk),
        in_specs=[pl.BlockSpec((1, 2, blk, LANES), lambda i, hp, j: (i, hp, j, 0)),
                  pl.BlockSpec((1, 2, s, LANES), lambda i, hp, j: (i, hp, 0, 0)),
                  pl.BlockSpec((1, s, LANES), lambda i, hp, j: (i, 0, hp))],
        out_specs=pl.BlockSpec((1, blk, LANES), lambda i, hp, j: (i, j, hp)),
        out_shape=jax.ShapeDtypeStruct((b, s, A_WIDTH), BF16),
        compiler_params=pltpu.CompilerParams(dimension_semantics=("arbitrary",) * 3, vmem_limit_bytes=VMEM_LIMIT),
        name="fox_prompt",
    )(qaug, kaug, vb)


def _fox_sample_body(ck_ref, cv_ref, clf_ref, q_ref, k_ref, v_ref, pfk_ref, ckc_ref, o_ref):
    past = ck_ref.shape[1]
    ts = q_ref.shape[2]
    row = lax.broadcasted_iota(I32, (past, past), 0)
    col = lax.broadcasted_iota(I32, (past, past), 1)
    tri = jnp.where(col > row, 1.0, 0.0).astype(BF16)
    lh, lm, ll = _split3(clf_ref[0])
    suffix = _dot(tri, lh) + _dot(tri, lm) + _dot(tri, ll)
    fak = _dot(jnp.concatenate(_split3(-suffix), axis=1), pfk_ref[...]) + ckc_ref[...]
    lane = _lane((past, LANES))
    lane_s = _lane((ts, LANES))
    causal = lax.broadcasted_iota(I32, (ts, ts), 1) <= lax.broadcasted_iota(I32, (ts, ts), 0)
    outs = []
    for pr in range(A_HEADS // 2):
        kc2 = ck_ref[0, :, LANES * pr:LANES * (pr + 1)]
        vc = cv_ref[0, :, LANES * pr:LANES * (pr + 1)].astype(BF16)
        vn = v_ref[0, :, LANES * pr:LANES * (pr + 1)]
        o = []
        for sub in range(2):
            hd = 2 * pr + sub
            kt = pltpu.roll(kc2, HEAD_DIM, 1) if sub else kc2
            kc = jnp.where(lane < HEAD_DIM, kt, fak[:, LANES * hd:LANES * (hd + 1)]).astype(BF16)
            q = q_ref[0, hd]
            s1 = _dot_nt(q, kc)
            s2 = jnp.where(causal, _dot_nt(q, k_ref[0, hd]), NEG_BIG)
            m = jnp.maximum(jnp.max(s1, axis=-1, keepdims=True), jnp.max(s2, axis=-1, keepdims=True))
            p1, p2 = jnp.exp(s1 - m), jnp.exp(s2 - m)
            den = jnp.sum(p1, axis=-1, keepdims=True) + jnp.sum(p2, axis=-1, keepdims=True)
            o.append((_dot(p1.astype(BF16), vc) + _dot(p2.astype(BF16), vn)) / den)
        outs.append(jnp.where(lane_s < HEAD_DIM, o[0], o[1]))
    o_ref[0] = jnp.concatenate(outs, axis=1).astype(BF16)


def _fox_sample_call(cache_k, cache_v, cache_lf, qaug, kaug, vb, pfk, ck):
    b, past, _ = cache_k.shape
    ts = qaug.shape[2]
    full = lambda a: pl.BlockSpec(a.shape, lambda i: (0,) * a.ndim)
    return pl.pallas_call(
        _fox_sample_body,
        grid=(b,),
        in_specs=[pl.BlockSpec((1, past, A_WIDTH), lambda i: (i, 0, 0)),
                  pl.BlockSpec((1, past, A_WIDTH), lambda i: (i, 0, 0)),
                  pl.BlockSpec((1, past, LANES), lambda i: (i, 0, 0)),
                  pl.BlockSpec((1, A_HEADS, ts, LANES), lambda i: (i, 0, 0, 0)),
                  pl.BlockSpec((1, A_HEADS, ts, LANES), lambda i: (i, 0, 0, 0)),
                  pl.BlockSpec((1, ts, A_WIDTH), lambda i: (i, 0, 0)),
                  full(pfk), full(ck)],
        out_specs=pl.BlockSpec((1, ts, A_WIDTH), lambda i: (i, 0, 0)),
        out_shape=jax.ShapeDtypeStruct((b, ts, A_WIDTH), BF16),
        compiler_params=pltpu.CompilerParams(dimension_semantics=("arbitrary",), vmem_limit_bytes=VMEM_LIMIT),
        name="fox_sample",
    )(cache_k, cache_v, cache_lf, qaug, kaug, vb, pfk, ck)


def _all_sublanes(x, op):
    for shift in (4, 2, 1):
        x = op(x, pltpu.roll(x, shift, 0))
    return x


def _top_groups(groups, payload, k, sub):
    groups = list(groups)
    big = SUBLANES * len(groups)
    out = []
    for _ in range(k):
        level = [(v, g) for g, v in enumerate(groups)]
        while len(level) > 1:
            merged = []
            for (va, ga), (vb, gb) in zip(level[0::2], level[1::2]):
                take = vb > va
                merged.append((jnp.where(take, vb, va), jnp.where(take, gb, ga)))
            level = merged + ([level[-1]] if len(level) % 2 else [])
        best, where_g = level[0]
        top = _all_sublanes(best, jnp.maximum)
        row = _all_sublanes(jnp.where(best == top, where_g * SUBLANES + sub, big), jnp.minimum)
        rel = row - sub
        hits = [rel == SUBLANES * g for g in range(len(groups))]
        if payload is None:
            out.append((top, row))
        else:
            pay = jnp.where(hits[0], payload[0], -1)
            for g in range(1, len(groups)):
                pay = jnp.where(hits[g], payload[g], pay)
            out.append((top, _all_sublanes(pay, jnp.maximum)))
        groups = [jnp.where(h, -jnp.inf, v) for h, v in zip(hits, groups)]
    return out


def _pack_rows(rows, sub2):
    acc = jnp.zeros(sub2.shape, rows[0].dtype)
    for j, r in enumerate(rows):
        acc = jnp.where(sub2 == j, jnp.concatenate([r, r], axis=0), acc)
    return acc


def _route_head(sc, sub, sub2):
    groups = lambda a: [a[SUBLANES * g:SUBLANES * (g + 1)] for g in range(a.shape[0] // SUBLANES)]
    t1 = _top_groups(groups(sc[:PEER_KEYS]), None, PEER_TOPK, sub)
    t2 = _top_groups(groups(sc[PEER_KEYS:]), None, PEER_TOPK, sub)
    s1, i1 = [v for v, _ in t1], [i * PEER_KEYS for _, i in t1]
    s2p, i2p = _pack_rows([v for v, _ in t2], sub2), _pack_rows([i for _, i in t2], sub2)
    s1p, i1p = _pack_rows(s1, sub2), _pack_rows(i1, sub2)
    lo, hi = slice(0, SUBLANES), slice(SUBLANES, 2 * SUBLANES)
    half = sub < 4
    rep4 = lambda a: jnp.where(half, a[lo], pltpu.roll(a[lo], 4, 0))
    pick = lambda a, b: jnp.where(half, a, b)

    def pairs(first, first_packed, second_packed, second0):
        return [first[0] + second_packed[lo], first[0] + second_packed[hi], first[1] + second_packed[lo],
                first[2] + second_packed[lo], first[3] + second_packed[lo],
                pick(first[4], first[5]) + rep4(second_packed), pick(first[6], first[7]) + rep4(second_packed),
                first_packed[hi] + second0]

    cand = pairs(s1, s1p, s2p, t2[0][0])
    expert = pairs(i1, i1p, i2p, t2[0][1])
    best = _top_groups(cand, expert, PEER_TOPK, sub)
    top = _pack_rows([v for v, _ in best], sub2)
    p = jnp.exp(top - best[0][0][0:1, :])
    gates = p / jnp.sum(p, axis=0, keepdims=True)
    return _pack_rows([e for _, e in best], sub2) * ROW_SUB, gates


def _route_body(x_ref, oa_ref, ob_ref, om_ref, wo_ref, gf_ref, wq_ref, keys_ref,
                x1_ref, xn_ref, e_ref, g_ref, qp_scr, e_scr, g_scr):
    tile, d_model = x_ref.shape
    lane_tiles = tile // LANES
    y = x_ref[...] + _dot(oa_ref[...], wo_ref[0:A_WIDTH, :]) \
        + _dot(ob_ref[...], wo_ref[A_WIDTH:A_WIDTH + B_WIDTH, :]) \
        + _dot(om_ref[...], wo_ref[A_WIDTH + B_WIDTH:, :])
    xn = y * lax.rsqrt(jnp.mean(y * y, axis=-1, keepdims=True) + EPS) * gf_ref[...]
    for c in range(d_model // LANES):
        rows = pl.ds(c, tile, stride=SUBLANES)
        x1_ref[rows, :] = y[:, LANES * c:LANES * (c + 1)]
        xn_ref[rows, :] = xn[:, LANES * c:LANES * (c + 1)]
    qp = _dot(xn.astype(BF16), wq_ref[...])
    for hd in range(PEER_HEADS):
        qp_scr[hd] = qp[:, LANES * hd:LANES * (hd + 1)].astype(BF16)
    sub = lax.broadcasted_iota(I32, (SUBLANES, LANES), 0)
    sub2 = lax.broadcasted_iota(I32, (2 * SUBLANES, LANES), 0)

    def head(hd, carry):
        for lt in range(lane_tiles):
            q = qp_scr[hd, LANES * lt:LANES * (lt + 1), :]
            e, gates = _route_head(_dot_nt(keys_ref[hd], q), sub, sub2)
            e_scr[lt, hd] = e
            g_scr[lt, hd] = gates
        return carry

    lax.fori_loop(0, PEER_HEADS, head, 0)
    for lt in range(lane_tiles):
        rows = slice(LANES * lt, LANES * (lt + 1))
        e_ref[rows, :] = jnp.concatenate([e_scr[lt, hd] for hd in range(PEER_HEADS)], axis=0).T
        g_ref[rows, :] = jnp.concatenate([g_scr[lt, hd] for hd in range(PEER_HEADS)], axis=0).T


def _route_call(x, oa, ob, om, wts, tile):
    n, d = x.shape
    full = lambda a: pl.BlockSpec(a.shape, lambda i: (0,) * a.ndim)
    rows = lambda w: pl.BlockSpec((tile, w), lambda i: (i, 0))
    consts = [wts[k] for k in ["w_out", "g_ffn", "w_pq", "keys"]]
    return pl.pallas_call(
        _route_body,
        grid=(n // tile,),
        in_specs=[rows(d), rows(A_WIDTH), rows(B_WIDTH), rows(M_WIDTH)] + [full(c) for c in consts],
        out_specs=[pl.BlockSpec((tile * SUBLANES, LANES), lambda i: (i, 0))] * 2 + [rows(PEER_SLOTS), rows(PEER_SLOTS)],
        out_shape=[jax.ShapeDtypeStruct((n * d // LANES, LANES), F32)] * 2 + [
                   jax.ShapeDtypeStruct((n, PEER_SLOTS), I32), jax.ShapeDtypeStruct((n, PEER_SLOTS), F32)],
        scratch_shapes=[pltpu.VMEM((PEER_HEADS, tile, LANES), BF16),
                        pltpu.VMEM((tile // LANES, PEER_HEADS, PEER_TOPK, LANES), I32),
                        pltpu.VMEM((tile // LANES, PEER_HEADS, PEER_TOPK, LANES), F32)],
        compiler_params=pltpu.CompilerParams(dimension_semantics=("arbitrary",), vmem_limit_bytes=VMEM_LIMIT),
        name="merge_route",
    )(x, oa, ob, om, *consts)


GROUP_ROWS = PEER_SLOTS * ROW_SUB
WIDE = PEER_SLOTS * SUBLANES
PIPE_TOKENS = 32


def _gather_token(idx_ref, row, tab_ref, buf):
    for k in range(PEER_SLOTS):
        buf[k * ROW_SUB:(k + 1) * ROW_SUB, :] = tab_ref[pl.ds(pl.multiple_of(idx_ref[row, k], ROW_SUB), ROW_SUB), :]


def _pipelined_tokens(e_hbm, tab_ref, bufs, idx, sems, compute, tile):
    blocks = tile // PIPE_TOKENS
    first_block = pl.program_id(0) * blocks
    all_blocks = pl.num_programs(0) * blocks

    def idx_copy(block, slot):
        rows = pl.ds(pl.multiple_of(block * PIPE_TOKENS, PIPE_TOKENS), PIPE_TOKENS)
        return pltpu.make_async_copy(e_hbm.at[rows, :], idx[slot], sems.at[slot])

    @pl.when(pl.program_id(0) == 0)
    def _():
        idx_copy(0, 0).start()
        idx_copy(1, 1).start()
        idx_copy(0, 0).wait()
        _gather_token(idx[0], 0, tab_ref, bufs[0])
        _gather_token(idx[0], 1, tab_ref, bufs[1])

    def block_pair(p, carry):
        for slot in range(2):
            local = 2 * p + slot
            block = first_block + local
            for half in range(PIPE_TOKENS // 2):
                cur = bufs[2 * (half % 2):2 * (half % 2) + 2]
                nxt = bufs[2 - 2 * (half % 2):4 - 2 * (half % 2)]
                for j in range(2):
                    compute(local * PIPE_TOKENS + 2 * half + j, cur[j])
                if half + 1 < PIPE_TOKENS // 2:
                    for j in range(2):
                        _gather_token(idx[slot], 2 * half + 2 + j, tab_ref, nxt[j])
                else:
                    @pl.when(block + 1 < all_blocks)
                    def _():
                        idx_copy(block + 1, 1 - slot).wait()
                        for j in range(2):
                            _gather_token(idx[1 - slot], j, tab_ref, nxt[j])

            @pl.when(block + 2 < all_blocks)
            def _():
                idx_copy(block + 2, slot).start()
        return carry

    lax.fori_loop(0, blocks // 2, block_pair, 0)


def _pipeline_scratch():
    return ([pltpu.VMEM((GROUP_ROWS, LANES), I32)] * 4 + [pltpu.SMEM((PIPE_TOKENS, PEER_SLOTS), I32)] * 2
            + [pltpu.SemaphoreType.DMA((2,))])


def _token_tile_rows(t):
    return pl.ds(pl.multiple_of(t * SUBLANES, SUBLANES), SUBLANES)


def _split2(x):
    hi = x.astype(BF16).astype(F32)
    return jnp.concatenate([hi, x - hi], axis=0).astype(BF16)


def _peer_u_body(e_hbm, x_ref, g_ref, u_ref, col_ref, mask_ref, w_ref, b0, b1, b2, b3, i0, i1, sems, z_scr, *, tile, chunk):
    mask = mask_ref[...]

    def compute(t, buf):
        z = _dot_nt(_split2(x_ref[_token_tile_rows(t), :]), pltpu.bitcast(buf[...], BF16))
        z_scr[_token_tile_rows(t), :] = (z[:SUBLANES] + z[SUBLANES:]) * mask

    _pipelined_tokens(e_hbm, u_ref, (b0, b1, b2, b3), (i0, i1), sems, compute, tile)
    for c in range(tile // chunk):
        zs = z_scr[c * chunk * SUBLANES:(c + 1) * chunk * SUBLANES, :]
        zh = zs.astype(BF16)
        zl = (zs - zh.astype(F32)).astype(BF16)
        part = _dot(zh, col_ref[...]) + _dot(zl, col_ref[...])
        a = jnp.sum(part.reshape(chunk, SUBLANES, PEER_SLOTS), axis=1)
        rows = slice(c * chunk, (c + 1) * chunk)
        w_ref[rows, :] = g_ref[rows, :] * (0.5 * a * (1.0 + lax.erf(a * np.float32(np.sqrt(0.5)))))


def _peer_u_call(e, xn8, gates, u_packed, consts, tile):
    n = e.shape[0]
    full = lambda a: pl.BlockSpec(a.shape, lambda i: (0,) * a.ndim)
    return pl.pallas_call(
        functools.partial(_peer_u_body, tile=tile, chunk=min(tile, 32)),
        grid=(n // tile,),
        in_specs=[pl.BlockSpec(memory_space=pl.ANY),
                  pl.BlockSpec((tile * SUBLANES, LANES), lambda i: (i, 0)),
                  pl.BlockSpec((tile, PEER_SLOTS), lambda i: (i, 0)),
                  pl.BlockSpec(memory_space=pltpu.VMEM),
                  full(consts["collapse"]), full(consts["mask8"])],
        out_specs=pl.BlockSpec((tile, PEER_SLOTS), lambda i: (i, 0)),
        out_shape=jax.ShapeDtypeStruct((n, PEER_SLOTS), F32),
        scratch_shapes=_pipeline_scratch() + [pltpu.VMEM((tile * SUBLANES, WIDE), F32)],
        compiler_params=pltpu.CompilerParams(dimension_semantics=("arbitrary",), vmem_limit_bytes=VMEM_LIMIT),
        name="peer_u",
    )(e, xn8, gates, u_packed, consts["collapse"], consts["mask8"])


def _peer_v_body(e_hbm, w_ref, x_ref, v_ref, exp_ref, mask_ref, y_ref, b0, b1, b2, b3, i0, i1, sems, eh_scr, el_scr, *, tile):
    mask = mask_ref[...]
    w = w_ref[...]
    wh = w.astype(BF16)
    eh_scr[...] = _dot(wh, exp_ref[...])
    el_scr[...] = _dot((w - wh.astype(F32)).astype(BF16), exp_ref[...])

    def compute(t, buf):
        row = pl.ds(t, 1)
        lhs = jnp.concatenate([jnp.broadcast_to(eh_scr[row, :], (SUBLANES, WIDE)) * mask,
                               jnp.broadcast_to(el_scr[row, :], (SUBLANES, WIDE)) * mask], axis=0).astype(BF16)
        o = _dot(lhs, pltpu.bitcast(buf[...], BF16))
        rows = _token_tile_rows(t)
        y_ref[rows, :] = x_ref[rows, :] + o[:SUBLANES] + o[SUBLANES:]

    _pipelined_tokens(e_hbm, v_ref, (b0, b1, b2, b3), (i0, i1), sems, compute, tile)


def _peer_v_call(e, w, x8, v_packed, consts, tile):
    n = e.shape[0]
    full = lambda a: pl.BlockSpec(a.shape, lambda i: (0,) * a.ndim)
    return pl.pallas_call(
        functools.partial(_peer_v_body, tile=tile),
        grid=(n // tile,),
        in_specs=[pl.BlockSpec(memory_space=pl.ANY),
                  pl.BlockSpec((tile, PEER_SLOTS), lambda i: (i, 0)),
                  pl.BlockSpec((tile * SUBLANES, LANES), lambda i: (i, 0)),
                  pl.BlockSpec(memory_space=pltpu.VMEM),
                  full(consts["expand"]), full(consts["mask8"])],
        out_specs=pl.BlockSpec((tile * SUBLANES, LANES), lambda i: (i, 0)),
        out_shape=jax.ShapeDtypeStruct((n * SUBLANES, LANES), F32),
        scratch_shapes=_pipeline_scratch() + [pltpu.VMEM((tile, WIDE), F32), pltpu.VMEM((tile, WIDE), F32)],
        compiler_params=pltpu.CompilerParams(dimension_semantics=("arbitrary",), vmem_limit_bytes=VMEM_LIMIT),
        name="peer_v",
    )(e, w, x8, v_packed, consts["expand"], consts["mask8"])


def _peer_constants():
    lane = np.arange(WIDE)
    expand = (lane[None, :] // SUBLANES == np.arange(PEER_SLOTS)[:, None]).astype(np.float32)
    mask8 = (lane[None, :] % SUBLANES == np.arange(SUBLANES)[:, None]).astype(np.float32)
    return {"expand": jnp.asarray(expand, BF16), "collapse": jnp.asarray(expand.T, BF16), "mask8": jnp.asarray(mask8)}


def _pack_table(tab):
    n = tab.shape[0]
    b = lax.bitcast_convert_type(tab.astype(BF16), jnp.uint16).astype(jnp.uint32).reshape(n, ROW_SUB, 2, LANES)
    packed = b[:, :, 0, :] | (b[:, :, 1, :] << 16)
    return lax.bitcast_convert_type(packed, I32).reshape(n * ROW_SUB, LANES)


def _block_mean(width):
    blk = np.arange(width) // HEAD_DIM
    return jnp.asarray((blk[:, None] == blk[None, :]) / HEAD_DIM, BF16)


def _forget_placement():
    pfq = np.zeros((3 * LANES, A_HEADS * LANES), np.float32)
    pfk = np.zeros((3 * LANES, A_HEADS * LANES), np.float32)
    cq = np.zeros((1, A_HEADS * LANES), np.float32)
    ck = np.zeros((1, A_HEADS * LANES), np.float32)
    for hd in range(A_HEADS):
        base = hd * LANES + HEAD_DIM
        for piece in range(3):
            pfq[piece * LANES + hd, base + piece] = 1.0
            pfk[piece * LANES + hd, base + 3 + piece] = -1.0
            cq[0, base + 3 + piece] = 1.0
            ck[0, base + piece] = 1.0
    return jnp.asarray(pfq, BF16), jnp.asarray(pfk, BF16), jnp.asarray(cq), jnp.asarray(ck)


def _layer_weights(l, g_attn, w_in, b_f, g_q_a, g_k_a, conv_w, conv_b, g_q_m, w_out, g_ffn, w_peer_q, peer_keys):
    scale = HEAD_DIM ** -0.5
    splits = np.cumsum([A_WIDTH, A_WIDTH, A_WIDTH, A_HEADS, B_WIDTH, B_WIDTH, B_WIDTH, M_WIDTH])
    wi = w_in[l]
    qa, ka, va, fl, bg, cg, hv, qm = [wi[:, a:b] for a, b in zip(np.r_[0, splits[:-1]], splits)]
    w_packed = jnp.concatenate([qa, ka, va, bg, cg, hv, qm, fl, jnp.zeros((wi.shape[0], LANES - A_HEADS), wi.dtype)],
                               axis=1).astype(BF16)
    pfq, pfk, cq, ck = _forget_placement()
    keys = peer_keys[l]
    zeros = jnp.zeros_like(keys[:, 0])
    keys2 = jnp.concatenate([jnp.concatenate([keys[:, 0], zeros], axis=-1),
                             jnp.concatenate([zeros, keys[:, 1]], axis=-1)], axis=1).astype(BF16)
    return {
        "g_attn": g_attn[l][None, :], "w_in": w_packed,
        "b_f": jnp.pad(b_f[l], (0, LANES - A_HEADS))[None, :],
        "g_q": (jnp.tile(g_q_a[l], A_HEADS) * scale)[None, :], "g_k": jnp.tile(g_k_a[l], A_HEADS)[None, :],
        "g_qm": (jnp.tile(g_q_m[l], M_HEADS) * scale)[None, :],
        "bd512": _block_mean(A_WIDTH), "bd256": _block_mean(M_WIDTH),
        "pfq": pfq, "pfk": pfk, "cq": cq, "ck": ck,
        "conv_w": conv_w[l], "conv_b": conv_b[l][None, :],
        "w_out": w_out[l].astype(BF16), "g_ffn": g_ffn[l][None, :], "w_pq": w_peer_q[l].astype(BF16), "keys": keys2,
    }


def _peer_and_merge(x, oa, ob, om, wts, u_packed, v_packed, route_tile, peer_tile):
    b, s, d = x.shape
    n = b * s
    x1, xn, e, gates = _route_call(x.reshape(n, d), oa.reshape(n, -1), ob.reshape(n, -1), om.reshape(n, -1), wts, route_tile)
    consts = _peer_constants()
    w = _peer_u_call(e, xn.reshape(n * SUBLANES, LANES), gates, u_packed, consts, peer_tile)
    y8 = _peer_v_call(e, w, x1.reshape(n * SUBLANES, LANES), v_packed, consts, peer_tile)
    return y8.reshape(b, s, d)


def kernel(x_prompt, x_sample, cache_a_k, cache_a_v, cache_a_logf, cache_b_conv, cache_m_k, cache_m_v, mem_prompt, g_attn, w_in, b_f, g_q_a, g_k_a, conv_w, conv_b, g_mem, w_mem_k, w_mem_v, g_k_m, g_q_m, w_out, g_ffn, w_peer_q, peer_keys, peer_u, peer_v):
    depth = w_in.shape[0]
    xp, xs = x_prompt, x_sample
    bp, sp, _ = xp.shape
    bs, ts, _ = xs.shape
    outs = [[] for _ in range(10)]
    for l in range(depth):
        wts = _layer_weights(l, g_attn, w_in, b_f, g_q_a, g_k_a, conv_w, conv_b, g_q_m, w_out, g_ffn, w_peer_q, peer_keys)
        u_packed, v_packed = _pack_table(peer_u[l]), _pack_table(peer_v[l])
        fox_blk = min(512, sp)
        route_tile = 256
        peer_tile = 128

        mk, mv = _memkv_call(mem_prompt, g_mem[l][None, :], w_mem_k[l].astype(BF16), w_mem_v[l].astype(BF16),
                             jnp.tile(g_k_m[l], M_HEADS)[None, :], wts["bd256"])
        ka, va, logf, qaug, kaug, vb, ob, om, cst = _proj_call(
            xp, jnp.zeros((bp, 2, B_WIDTH), F32), mk, mv, wts, min(512, sp))
        oa = _fox_prompt_call(qaug, kaug, vb, fox_blk)
        xp = _peer_and_merge(xp, oa, ob, om, wts, u_packed, v_packed, route_tile, peer_tile)
        n_mem = mk.shape[1]
        for dst, val in zip(outs[:6], [ka.reshape(bp, sp, A_HEADS, HEAD_DIM), va.reshape(bp, sp, A_HEADS, HEAD_DIM), logf, cst,
                                       mk.reshape(bp, n_mem, M_HEADS, HEAD_DIM), mv.reshape(bp, n_mem, M_HEADS, HEAD_DIM)]):
            dst.append(val)

        past = cache_a_k.shape[2]
        ka, va, logf, qaug, kaug, vb, ob, om, cst = _proj_call(
            xs, cache_b_conv[l], cache_m_k[l].reshape(bs, -1, M_WIDTH), cache_m_v[l].reshape(bs, -1, M_WIDTH), wts, ts)
        clf = jnp.pad(cache_a_logf[l], ((0, 0), (0, 0), (0, LANES - A_HEADS)))
        oa = _fox_sample_call(cache_a_k[l].reshape(bs, past, A_WIDTH), cache_a_v[l].reshape(bs, past, A_WIDTH), clf,
                              qaug, kaug, vb, wts["pfk"], wts["ck"])
        xs = _peer_and_merge(xs, oa, ob, om, wts, u_packed, v_packed, route_tile, peer_tile)
        for dst, val in zip(outs[6:], [ka.reshape(bs, ts, A_HEADS, HEAD_DIM), va.reshape(bs, ts, A_HEADS, HEAD_DIM), logf, cst]):
            dst.append(val)

    return (xp, xs) + tuple(jnp.stack(o) for o in outs)
```

```python
import functools

import numpy as np
import jax
import jax.numpy as jnp
from jax import lax
from jax.experimental import pallas as pl
from jax.experimental.pallas import tpu as pltpu

F32, BF16, I32 = jnp.float32, jnp.bfloat16, jnp.int32
EPS = 1e-6
LANES = 128
SUBLANES = 8
HEAD_DIM = 64
A_HEADS = 8
A_WIDTH = A_HEADS * HEAD_DIM
B_WIDTH = 256
M_HEADS = 4
M_WIDTH = M_HEADS * HEAD_DIM
PEER_HEADS = 8
PEER_KEYS = 128
PEER_TOPK = 16
PEER_SLOTS = PEER_HEADS * PEER_TOPK
D_MODEL = 1024
ROW_WORDS = D_MODEL // 2
ROW_SUB = ROW_WORDS // LANES
NEG_BIG = -1e30
VMEM_LIMIT = 56 * 1024 * 1024

C_Q, C_K, C_V, C_BG, C_CG, C_HV, C_QM, C_FL, C_END = 0, 512, 1024, 1536, 1792, 2048, 2304, 2560, 2688


def _dot(a, b):
    return jnp.dot(a, b, preferred_element_type=F32)


def _dot_nt(a, b):
    return lax.dot_general(a, b, (((1,), (1,)), ((), ())), preferred_element_type=F32)


def _split3(x):
    hi = x.astype(BF16)
    r1 = x - hi.astype(F32)
    mid = r1.astype(BF16)
    lo = (r1 - mid.astype(F32)).astype(BF16)
    return hi, mid, lo


def _lane(shape):
    return lax.broadcasted_iota(I32, shape, len(shape) - 1)


def _memkv_body(mem_ref, g_ref, wk_ref, wv_ref, gk_ref, bd_ref, mk_ref, mv_ref):
    x = mem_ref[0]
    h = (x * lax.rsqrt(jnp.mean(x * x, axis=-1, keepdims=True) + EPS) * g_ref[...]).astype(BF16)
    zk = _dot(h, wk_ref[...])
    ms = _dot((zk * zk).astype(BF16), bd_ref[...])
    mk_ref[0] = zk * lax.rsqrt(ms + EPS) * gk_ref[...]
    mv_ref[0] = _dot(h, wv_ref[...])


def _memkv_call(mem, g_mem, w_mk, w_mv, gk_t, bd256):
    b, n_mem, d = mem.shape
    full = lambda shape: pl.BlockSpec(shape, lambda i: (0,) * len(shape))
    return pl.pallas_call(
        _memkv_body,
        grid=(b,),
        in_specs=[pl.BlockSpec((1, n_mem, d), lambda i: (i, 0, 0)), full((1, d)), full((d, M_WIDTH)),
                  full((d, M_WIDTH)), full((1, M_WIDTH)), full((M_WIDTH, M_WIDTH))],
        out_specs=[pl.BlockSpec((1, n_mem, M_WIDTH), lambda i: (i, 0, 0))] * 2,
        out_shape=[jax.ShapeDtypeStruct((b, n_mem, M_WIDTH), F32)] * 2,
        compiler_params=pltpu.CompilerParams(dimension_semantics=("arbitrary",), vmem_limit_bytes=VMEM_LIMIT),
        name="mem_kv",
    )(mem, g_mem, w_mk, w_mv, gk_t, bd256)


def _proj_body(x_ref, prev_ref, mk_ref, mv_ref, gat_ref, w_ref, bf_ref, gq_ref, gk_ref, gqm_ref,
               bd512_ref, bd256_ref, pfq_ref, pfk_ref, cq_ref, ck_ref, cw_ref, cb_ref,
               ka_ref, va_ref, logf_ref, qaug_ref, kaug_ref, vb_ref, ob_ref, om_ref, cst_ref,
               fcarry, ucarry):
    t = pl.program_id(1)
    tt = x_ref.shape[1]
    x = x_ref[0]
    h = (x * lax.rsqrt(jnp.mean(x * x, axis=-1, keepdims=True) + EPS) * gat_ref[...]).astype(BF16)

    def proj(c0, c1):
        return _dot(h, w_ref[:, c0:c1])

    @pl.when(t == 0)
    def _():
        fcarry[...] = jnp.zeros_like(fcarry)
        ucarry[...] = jnp.zeros_like(ucarry)
        ucarry[SUBLANES - 2:SUBLANES, :] = prev_ref[0]

    v = proj(C_FL, C_END) + bf_ref[...]
    logf = jnp.minimum(v, 0.0) - jnp.log1p(jnp.exp(-jnp.abs(v)))
    logf = jnp.where(_lane(logf.shape) < A_HEADS, logf, 0.0)
    logf_ref[0] = logf[:, :A_HEADS]
    row = lax.broadcasted_iota(I32, (tt, tt), 0)
    col = lax.broadcasted_iota(I32, (tt, tt), 1)
    tri = jnp.where(row >= col, 1.0, 0.0).astype(BF16)
    lh, lm, ll = _split3(logf)
    fcum = _dot(tri, lh) + _dot(tri, lm) + _dot(tri, ll) + fcarry[0:1, :]
    fcarry[...] = jnp.broadcast_to(fcum[tt - 1:tt, :], fcarry.shape)
    fparts = jnp.concatenate(_split3(fcum), axis=1)
    faq = _dot(fparts, pfq_ref[...]) + cq_ref[...]
    fak = _dot(fparts, pfk_ref[...]) + ck_ref[...]

    lane = _lane((tt, LANES))
    zq = proj(C_Q, C_K)
    qn = zq * lax.rsqrt(_dot((zq * zq).astype(BF16), bd512_ref[...]) + EPS) * gq_ref[...]
    zk = proj(C_K, C_V)
    kn = zk * lax.rsqrt(_dot((zk * zk).astype(BF16), bd512_ref[...]) + EPS) * gk_ref[...]
    ka_ref[0] = kn
    for hd in range(A_HEADS):
        c0 = LANES * (hd // 2)
        qt, kt = qn[:, c0:c0 + LANES], kn[:, c0:c0 + LANES]
        if hd % 2:
            qt, kt = pltpu.roll(qt, HEAD_DIM, 1), pltpu.roll(kt, HEAD_DIM, 1)
        qaug_ref[0, hd] = jnp.where(lane < HEAD_DIM, qt, faq[:, LANES * hd:LANES * (hd + 1)]).astype(BF16)
        kaug_ref[0, hd] = jnp.where(lane < HEAD_DIM, kt, fak[:, LANES * hd:LANES * (hd + 1)]).astype(BF16)
    zv = proj(C_V, C_BG)
    va_ref[0] = zv
    vb_ref[0] = zv.astype(BF16)

    u = proj(C_CG, C_HV) * proj(C_HV, C_QM)
    rows = lax.broadcasted_iota(I32, u.shape, 0)
    p1 = ucarry[SUBLANES - 1:SUBLANES, :]
    p2 = ucarry[SUBLANES - 2:SUBLANES - 1, :]
    u1 = jnp.where(rows == 0, p1, pltpu.roll(u, 1, 0))
    u2 = jnp.where(rows == 0, p2, jnp.where(rows == 1, p1, pltpu.roll(u, 2, 0)))
    cy = cb_ref[...] + cw_ref[2:3, :] * u + cw_ref[0:1, :] * u2 + cw_ref[1:2, :] * u1
    ob_ref[0] = (proj(C_BG, C_CG) * cy).astype(BF16)
    ucarry[...] = u[tt - SUBLANES:tt, :]
    cst_ref[0] = u[tt - 2:tt, :]

    zm = proj(C_QM, C_FL)
    qm = zm * lax.rsqrt(_dot((zm * zm).astype(BF16), bd256_ref[...]) + EPS) * gqm_ref[...]
    mkb = mk_ref[0].astype(BF16)
    mvb = mv_ref[0].astype(BF16)
    outs = []
    for pr in range(M_HEADS // 2):
        qp = qm[:, LANES * pr:LANES * (pr + 1)]
        kp = mkb[:, LANES * pr:LANES * (pr + 1)]
        vp = mvb[:, LANES * pr:LANES * (pr + 1)]
        o = []
        for sub in range(2):
            keep = (lane < HEAD_DIM) if sub == 0 else (lane >= HEAD_DIM)
            s = _dot_nt(jnp.where(keep, qp, 0.0).astype(BF16), kp)
            p = jnp.exp(s - jnp.max(s, axis=-1, keepdims=True))
            o.append(_dot(p.astype(BF16), vp) / jnp.sum(p, axis=-1, keepdims=True))
        outs.append(jnp.where(lane < HEAD_DIM, o[0], o[1]))
    om_ref[0] = jnp.concatenate(outs, axis=1).astype(BF16)


def _proj_call(x, prev, mk, mv, wts, tile):
    b, s, d = x.shape
    nt = s // tile
    n_mem = mk.shape[1]
    full = lambda a: pl.BlockSpec(a.shape, lambda i, j: (0,) * a.ndim)
    seq = lambda w: pl.BlockSpec((1, tile, w), lambda i, j: (i, j, 0))
    per_b = lambda r, w: pl.BlockSpec((1, r, w), lambda i, j: (i, 0, 0))
    heads = pl.BlockSpec((1, A_HEADS, tile, LANES), lambda i, j: (i, 0, j, 0))
    names = ["g_attn", "w_in", "b_f", "g_q", "g_k", "g_qm", "bd512", "bd256", "pfq", "pfk", "cq", "ck", "conv_w", "conv_b"]
    consts = [wts[k] for k in names]
    out_shape = [
        jax.ShapeDtypeStruct((b, s, A_WIDTH), F32),
        jax.ShapeDtypeStruct((b, s, A_WIDTH), F32),
        jax.ShapeDtypeStruct((b, s, A_HEADS), F32),
        jax.ShapeDtypeStruct((b, A_HEADS, s, LANES), BF16),
        jax.ShapeDtypeStruct((b, A_HEADS, s, LANES), BF16),
        jax.ShapeDtypeStruct((b, s, A_WIDTH), BF16),
        jax.ShapeDtypeStruct((b, s, B_WIDTH), BF16),
        jax.ShapeDtypeStruct((b, s, M_WIDTH), BF16),
        jax.ShapeDtypeStruct((b, 2, B_WIDTH), F32),
    ]
    out_specs = [seq(A_WIDTH), seq(A_WIDTH), seq(A_HEADS), heads, heads, seq(A_WIDTH), seq(B_WIDTH), seq(M_WIDTH),
                 per_b(2, B_WIDTH)]
    return pl.pallas_call(
        _proj_body,
        grid=(b, nt),
        in_specs=[seq(d), per_b(2, B_WIDTH), per_b(n_mem, M_WIDTH), per_b(n_mem, M_WIDTH)] + [full(c) for c in consts],
        out_specs=out_specs,
        out_shape=out_shape,
        scratch_shapes=[pltpu.VMEM((SUBLANES, LANES), F32), pltpu.VMEM((SUBLANES, B_WIDTH), F32)],
        compiler_params=pltpu.CompilerParams(dimension_semantics=("arbitrary", "arbitrary"), vmem_limit_bytes=VMEM_LIMIT),
        name="proj",
    )(x, prev, mk, mv, *consts)


def _fox_prompt_body(q_ref, k_ref, v_ref, o_ref, *, blk):
    qi = pl.program_id(2)
    ones = jnp.ones((blk, LANES), BF16)
    row = lax.broadcasted_iota(I32, (blk, blk), 0)
    col = lax.broadcasted_iota(I32, (blk, blk), 1)
    def step(kj, carry, masked):
        off = pl.multiple_of(kj * blk, blk)
        vv = jnp.concatenate([v_ref[0, pl.ds(off, blk), :], ones], axis=1)
        new = []
        for sub in range(2):
            m, acc = carry[sub]
            s = _dot_nt(q_ref[0, sub], k_ref[0, sub, pl.ds(off, blk), :])
            if masked:
                s = jnp.where(col <= row, s, NEG_BIG)
            m_new = jnp.maximum(m, jnp.max(s, axis=-1, keepdims=True))
            p = jnp.exp(s - m_new).astype(BF16)
            new.append((m_new, jnp.exp(m - m_new) * acc + _dot(p, vv)))
        return tuple(new)

    init = ((jnp.full((blk, 1), NEG_BIG, F32), jnp.zeros((blk, 2 * LANES), F32)),) * 2
    carry = lax.fori_loop(0, qi, functools.partial(step, masked=False), init)
    (_, acc0), (_, acc1) = step(qi, carry, True)
    outs = [acc[:, :LANES] / acc[:, LANES:] for acc in (acc0, acc1)]
    o_ref[0] = jnp.where(_lane((blk, LANES)) < HEAD_DIM, outs[0], outs[1]).astype(BF16)


def _fox_prompt_call(qaug, kaug, vb, blk):
    b, _, s, _ = qaug.shape
    return pl.pallas_call(
        functools.partial(_fox_prompt_body, blk=blk),
        grid=(b, A_HEADS // 2, s // blk),
        in_specs=[pl.BlockSpec((1, 2, blk, LANES), lambda i, hp, j: (i, hp, j, 0)),
                  pl.BlockSpec((1, 2, s, LANES), lambda i, hp, j: (i, hp, 0, 0)),
                  pl.BlockSpec((1, s, LANES), lambda i, hp, j: (i, 0, hp))],
        out_specs=pl.BlockSpec((1, blk, LANES), lambda i, hp, j: (i, j, hp)),
        out_shape=jax.ShapeDtypeStruct((b, s, A_WIDTH), BF16),
        compiler_params=pltpu.CompilerParams(dimension_semantics=("arbitrary",) * 3, vmem_limit_bytes=VMEM_LIMIT),
        name="fox_prompt",
    )(qaug, kaug, vb)


def _fox_sample_body(ck_ref, cv_ref, clf_ref, q_ref, k_ref, v_ref, pfk_ref, ckc_ref, o_ref):
    past = ck_ref.shape[1]
    ts = q_ref.shape[2]
    row = lax.broadcasted_iota(I32, (past, past), 0)
    col = lax.broadcasted_iota(I32, (past, past), 1)
    tri = jnp.where(col > row, 1.0, 0.0).astype(BF16)
    lh, lm, ll = _split3(clf_ref[0])
    suffix = _dot(tri, lh) + _dot(tri, lm) + _dot(tri, ll)
    fak = _dot(jnp.concatenate(_split3(-suffix), axis=1), pfk_ref[...]) + ckc_ref[...]
    lane = _lane((past, LANES))
    lane_s = _lane((ts, LANES))
    causal = lax.broadcasted_iota(I32, (ts, ts), 1) <= lax.broadcasted_iota(I32, (ts, ts), 0)
    outs = []
    for pr in range(A_HEADS // 2):
        kc2 = ck_ref[0, :, LANES * pr:LANES * (pr + 1)]
        vc = cv_ref[0, :, LANES * pr:LANES * (pr + 1)].astype(BF16)
        vn = v_ref[0, :, LANES * pr:LANES * (pr + 1)]
        o = []
        for sub in range(2):
            hd = 2 * pr + sub
            kt = pltpu.roll(kc2, HEAD_DIM, 1) if sub else kc2
            kc = jnp.where(lane < HEAD_DIM, kt, fak[:, LANES * hd:LANES * (hd + 1)]).astype(BF16)
            q = q_ref[0, hd]
            s1 = _dot_nt(q, kc)
            s2 = jnp.where(causal, _dot_nt(q, k_ref[0, hd]), NEG_BIG)
            m = jnp.maximum(jnp.max(s1, axis=-1, keepdims=True), jnp.max(s2, axis=-1, keepdims=True))
            p1, p2 = jnp.exp(s1 - m), jnp.exp(s2 - m)
            den = jnp.sum(p1, axis=-1, keepdims=True) + jnp.sum(p2, axis=-1, keepdims=True)
            o.append((_dot(p1.astype(BF16), vc) + _dot(p2.astype(BF16), vn)) / den)
        outs.append(jnp.where(lane_s < HEAD_DIM, o[0], o[1]))
    o_ref[0] = jnp.concatenate(outs, axis=1).astype(BF16)


def _fox_sample_call(cache_k, cache_v, cache_lf, qaug, kaug, vb, pfk, ck):
    b, past, _ = cache_k.shape
    ts = qaug.shape[2]
    full = lambda a: pl.BlockSpec(a.shape, lambda i: (0,) * a.ndim)
    return pl.pallas_call(
        _fox_sample_body,
        grid=(b,),
        in_specs=[pl.BlockSpec((1, past, A_WIDTH), lambda i: (i, 0, 0)),
                  pl.BlockSpec((1, past, A_WIDTH), lambda i: (i, 0, 0)),
                  pl.BlockSpec((1, past, LANES), lambda i: (i, 0, 0)),
                  pl.BlockSpec((1, A_HEADS, ts, LANES), lambda i: (i, 0, 0, 0)),
                  pl.BlockSpec((1, A_HEADS, ts, LANES), lambda i: (i, 0, 0, 0)),
                  pl.BlockSpec((1, ts, A_WIDTH), lambda i: (i, 0, 0)),
                  full(pfk), full(ck)],
        out_specs=pl.BlockSpec((1, ts, A_WIDTH), lambda i: (i, 0, 0)),
        out_shape=jax.ShapeDtypeStruct((b, ts, A_WIDTH), BF16),
        compiler_params=pltpu.CompilerParams(dimension_semantics=("arbitrary",), vmem_limit_bytes=VMEM_LIMIT),
        name="fox_sample",
    )(cache_k, cache_v, cache_lf, qaug, kaug, vb, pfk, ck)


def _all_sublanes(x, op):
    for shift in (4, 2, 1):
        x = op(x, pltpu.roll(x, shift, 0))
    return x


def _top_groups(groups, payload, k, sub):
    groups = list(groups)
    big = SUBLANES * len(groups)
    out = []
    for _ in range(k):
        level = [(v, g) for g, v in enumerate(groups)]
        while len(level) > 1:
            merged = []
            for (va, ga), (vb, gb) in zip(level[0::2], level[1::2]):
                take = vb > va
                merged.append((jnp.where(take, vb, va), jnp.where(take, gb, ga)))
            level = merged + ([level[-1]] if len(level) % 2 else [])
        best, where_g = level[0]
        top = _all_sublanes(best, jnp.maximum)
        row = _all_sublanes(jnp.where(best == top, where_g * SUBLANES + sub, big), jnp.minimum)
        rel = row - sub
        hits = [rel == SUBLANES * g for g in range(len(groups))]
        if payload is None:
            out.append((top, row))
        else:
            pay = jnp.where(hits[0], payload[0], -1)
            for g in range(1, len(groups)):
                pay = jnp.where(hits[g], payload[g], pay)
            out.append((top, _all_sublanes(pay, jnp.maximum)))
        groups = [jnp.where(h, -jnp.inf, v) for h, v in zip(hits, groups)]
    return out


def _pack_rows(rows, sub2):
    acc = jnp.zeros(sub2.shape, rows[0].dtype)
    for j, r in enumerate(rows):
        acc = jnp.where(sub2 == j, jnp.concatenate([r, r], axis=0), acc)
    return acc


def _route_head(sc, sub, sub2):
    groups = lambda a: [a[SUBLANES * g:SUBLANES * (g + 1)] for g in range(a.shape[0] // SUBLANES)]
    t1 = _top_groups(groups(sc[:PEER_KEYS]), None, PEER_TOPK, sub)
    t2 = _top_groups(groups(sc[PEER_KEYS:]), None, PEER_TOPK, sub)
    s1, i1 = [v for v, _ in t1], [i * PEER_KEYS for _, i in t1]
    s2p, i2p = _pack_rows([v for v, _ in t2], sub2), _pack_rows([i for _, i in t2], sub2)
    s1p, i1p = _pack_rows(s1, sub2), _pack_rows(i1, sub2)
    lo, hi = slice(0, SUBLANES), slice(SUBLANES, 2 * SUBLANES)
    half = sub < 4
    rep4 = lambda a: jnp.where(half, a[lo], pltpu.roll(a[lo], 4, 0))
    pick = lambda a, b: jnp.where(half, a, b)

    def pairs(first, first_packed, second_packed, second0):
        return [first[0] + second_packed[lo], first[0] + second_packed[hi], first[1] + second_packed[lo],
                first[2] + second_packed[lo], first[3] + second_packed[lo],
                pick(first[4], first[5]) + rep4(second_packed), pick(first[6], first[7]) + rep4(second_packed),
                first_packed[hi] + second0]

    cand = pairs(s1, s1p, s2p, t2[0][0])
    expert = pairs(i1, i1p, i2p, t2[0][1])
    best = _top_groups(cand, expert, PEER_TOPK, sub)
    top = _pack_rows([v for v, _ in best], sub2)
    p = jnp.exp(top - best[0][0][0:1, :])
    gates = p / jnp.sum(p, axis=0, keepdims=True)
    return _pack_rows([e for _, e in best], sub2) * ROW_SUB, gates


def _route_body(x_ref, oa_ref, ob_ref, om_ref, wo_ref, gf_ref, wq_ref, keys_ref,
                x1_ref, xn_ref, e_ref, g_ref, qp_scr, e_scr, g_scr):
    tile, d_model = x_ref.shape
    lane_tiles = tile // LANES
    y = x_ref[...] + _dot(oa_ref[...], wo_ref[0:A_WIDTH, :]) \
        + _dot(ob_ref[...], wo_ref[A_WIDTH:A_WIDTH + B_WIDTH, :]) \
        + _dot(om_ref[...], wo_ref[A_WIDTH + B_WIDTH:, :])
    xn = y * lax.rsqrt(jnp.mean(y * y, axis=-1, keepdims=True) + EPS) * gf_ref[...]
    for c in range(d_model // LANES):
        rows = pl.ds(c, tile, stride=SUBLANES)
        x1_ref[rows, :] = y[:, LANES * c:LANES * (c + 1)]
        xn_ref[rows, :] = xn[:, LANES * c:LANES * (c + 1)]
    qp = _dot(xn.astype(BF16), wq_ref[...])
    for hd in range(PEER_HEADS):
        qp_scr[hd] = qp[:, LANES * hd:LANES * (hd + 1)].astype(BF16)
    sub = lax.broadcasted_iota(I32, (SUBLANES, LANES), 0)
    sub2 = lax.broadcasted_iota(I32, (2 * SUBLANES, LANES), 0)

    def head(hd, carry):
        for lt in range(lane_tiles):
            q = qp_scr[hd, LANES * lt:LANES * (lt + 1), :]
            e, gates = _route_head(_dot_nt(keys_ref[hd], q), sub, sub2)
            e_scr[lt, hd] = e
            g_scr[lt, hd] = gates
        return carry

    lax.fori_loop(0, PEER_HEADS, head, 0)
    for lt in range(lane_tiles):
        rows = slice(LANES * lt, LANES * (lt + 1))
        e_ref[rows, :] = jnp.concatenate([e_scr[lt, hd] for hd in range(PEER_HEADS)], axis=0).T
        g_ref[rows, :] = jnp.concatenate([g_scr[lt, hd] for hd in range(PEER_HEADS)], axis=0).T


def _route_call(x, oa, ob, om, wts, tile):
    n, d = x.shape
    full = lambda a: pl.BlockSpec(a.shape, lambda i: (0,) * a.ndim)
    rows = lambda w: pl.BlockSpec((tile, w), lambda i: (i, 0))
    consts = [wts[k] for k in ["w_out", "g_ffn", "w_pq", "keys"]]
    return pl.pallas_call(
        _route_body,
        grid=(n // tile,),
        in_specs=[rows(d), rows(A_WIDTH), rows(B_WIDTH), rows(M_WIDTH)] + [full(c) for c in consts],
        out_specs=[pl.BlockSpec((tile * SUBLANES, LANES), lambda i: (i, 0))] * 2 + [rows(PEER_SLOTS), rows(PEER_SLOTS)],
        out_shape=[jax.ShapeDtypeStruct((n * d // LANES, LANES), F32)] * 2 + [
                   jax.ShapeDtypeStruct((n, PEER_SLOTS), I32), jax.ShapeDtypeStruct((n, PEER_SLOTS), F32)],
        scratch_shapes=[pltpu.VMEM((PEER_HEADS, tile, LANES), BF16),
                        pltpu.VMEM((tile // LANES, PEER_HEADS, PEER_TOPK, LANES), I32),
                        pltpu.VMEM((tile // LANES, PEER_HEADS, PEER_TOPK, LANES), F32)],
        compiler_params=pltpu.CompilerParams(dimension_semantics=("arbitrary",), vmem_limit_bytes=VMEM_LIMIT),
        name="merge_route",
    )(x, oa, ob, om, *consts)


GROUP_ROWS = PEER_SLOTS * ROW_SUB
WIDE = PEER_SLOTS * SUBLANES
PIPE_TOKENS = 64
PACK_ROWS = 512


def _gather_token(idx_ref, row, tab_ref, buf):
    for k in range(PEER_SLOTS):
        buf[k * ROW_SUB:(k + 1) * ROW_SUB, :] = tab_ref[pl.ds(pl.multiple_of(idx_ref[row, k], ROW_SUB), ROW_SUB), :]


def _pipelined_tokens(e_hbm, tab_ref, bufs, idx, sems, compute, tile):
    blocks = tile // PIPE_TOKENS
    first_block = pl.program_id(0) * blocks
    all_blocks = pl.num_programs(0) * blocks

    def idx_copy(block, slot):
        rows = pl.ds(pl.multiple_of(block * PIPE_TOKENS, PIPE_TOKENS), PIPE_TOKENS)
        return pltpu.make_async_copy(e_hbm.at[rows, :], idx[slot], sems.at[slot])

    @pl.when(pl.program_id(0) == 0)
    def _():
        idx_copy(0, 0).start()
        idx_copy(1, 1).start()
        idx_copy(0, 0).wait()
        _gather_token(idx[0], 0, tab_ref, bufs[0])
        _gather_token(idx[0], 1, tab_ref, bufs[1])

    def block_pair(p, carry):
        for slot in range(2):
            local = 2 * p + slot
            block = first_block + local
            for half in range(PIPE_TOKENS // 2):
                cur = bufs[2 * (half % 2):2 * (half % 2) + 2]
                nxt = bufs[2 - 2 * (half % 2):4 - 2 * (half % 2)]
                for j in range(2):
                    compute(local * PIPE_TOKENS + 2 * half + j, cur[j])
                if half + 1 < PIPE_TOKENS // 2:
                    for j in range(2):
                        _gather_token(idx[slot], 2 * half + 2 + j, tab_ref, nxt[j])
                else:
                    @pl.when(block + 1 < all_blocks)
                    def _():
                        idx_copy(block + 1, 1 - slot).wait()
                        for j in range(2):
                            _gather_token(idx[1 - slot], j, tab_ref, nxt[j])

            @pl.when(block + 2 < all_blocks)
            def _():
                idx_copy(block + 2, slot).start()
        return carry

    lax.fori_loop(0, blocks // 2, block_pair, 0)


def _pipeline_scratch():
    return ([pltpu.VMEM((GROUP_ROWS, LANES), I32)] * 4 + [pltpu.SMEM((PIPE_TOKENS, PEER_SLOTS), I32)] * 2
            + [pltpu.SemaphoreType.DMA((2,))])


def _token_tile_rows(t):
    return pl.ds(pl.multiple_of(t * SUBLANES, SUBLANES), SUBLANES)


def _split2(x):
    hi = x.astype(BF16).astype(F32)
    return jnp.concatenate([hi, x - hi], axis=0).astype(BF16)


def _peer_u_body(e_hbm, x_ref, g_ref, u_ref, col_ref, mask_ref, w_ref, b0, b1, b2, b3, i0, i1, sems, z_scr, *, tile, chunk):
    mask = mask_ref[...]

    def compute(t, buf):
        z = _dot_nt(_split2(x_ref[_token_tile_rows(t), :]), pltpu.bitcast(buf[...], BF16))
        z_scr[_token_tile_rows(t), :] = (z[:SUBLANES] + z[SUBLANES:]) * mask

    _pipelined_tokens(e_hbm, u_ref, (b0, b1, b2, b3), (i0, i1), sems, compute, tile)
    for c in range(tile // chunk):
        zs = z_scr[c * chunk * SUBLANES:(c + 1) * chunk * SUBLANES, :]
        zh = zs.astype(BF16)
        zl = (zs - zh.astype(F32)).astype(BF16)
        part = _dot(zh, col_ref[...]) + _dot(zl, col_ref[...])
        a = jnp.sum(part.reshape(chunk, SUBLANES, PEER_SLOTS), axis=1)
        rows = slice(c * chunk, (c + 1) * chunk)
        w_ref[rows, :] = g_ref[rows, :] * (0.5 * a * (1.0 + lax.erf(a * np.float32(np.sqrt(0.5)))))


def _peer_u_call(e, xn8, gates, u_packed, consts, tile):
    n = e.shape[0]
    full = lambda a: pl.BlockSpec(a.shape, lambda i: (0,) * a.ndim)
    return pl.pallas_call(
        functools.partial(_peer_u_body, tile=tile, chunk=min(tile, 32)),
        grid=(n // tile,),
        in_specs=[pl.BlockSpec(memory_space=pl.ANY),
                  pl.BlockSpec((tile * SUBLANES, LANES), lambda i: (i, 0)),
                  pl.BlockSpec((tile, PEER_SLOTS), lambda i: (i, 0)),
                  pl.BlockSpec(memory_space=pltpu.VMEM),
                  full(consts["collapse"]), full(consts["mask8"])],
        out_specs=pl.BlockSpec((tile, PEER_SLOTS), lambda i: (i, 0)),
        out_shape=jax.ShapeDtypeStruct((n, PEER_SLOTS), F32),
        scratch_shapes=_pipeline_scratch() + [pltpu.VMEM((tile * SUBLANES, WIDE), F32)],
        compiler_params=pltpu.CompilerParams(dimension_semantics=("arbitrary",), vmem_limit_bytes=VMEM_LIMIT),
        name="peer_u",
    )(e, xn8, gates, u_packed, consts["collapse"], consts["mask8"])


def _peer_v_body(e_hbm, w_ref, x_ref, v_ref, exp_ref, mask_ref, y_ref, b0, b1, b2, b3, i0, i1, sems, eh_scr, el_scr, *, tile):
    mask = mask_ref[...]
    w = w_ref[...]
    wh = w.astype(BF16)
    eh_scr[...] = _dot(wh, exp_ref[...])
    el_scr[...] = _dot((w - wh.astype(F32)).astype(BF16), exp_ref[...])

    def compute(t, buf):
        row = pl.ds(t, 1)
        lhs = jnp.concatenate([jnp.broadcast_to(eh_scr[row, :], (SUBLANES, WIDE)) * mask,
                               jnp.broadcast_to(el_scr[row, :], (SUBLANES, WIDE)) * mask], axis=0).astype(BF16)
        o = _dot(lhs, pltpu.bitcast(buf[...], BF16))
        rows = _token_tile_rows(t)
        y_ref[rows, :] = x_ref[rows, :] + o[:SUBLANES] + o[SUBLANES:]

    _pipelined_tokens(e_hbm, v_ref, (b0, b1, b2, b3), (i0, i1), sems, compute, tile)


def _peer_v_call(e, w, x8, v_packed, consts, tile):
    n = e.shape[0]
    full = lambda a: pl.BlockSpec(a.shape, lambda i: (0,) * a.ndim)
    return pl.pallas_call(
        functools.partial(_peer_v_body, tile=tile),
        grid=(n // tile,),
        in_specs=[pl.BlockSpec(memory_space=pl.ANY),
                  pl.BlockSpec((tile, PEER_SLOTS), lambda i: (i, 0)),
                  pl.BlockSpec((tile * SUBLANES, LANES), lambda i: (i, 0)),
                  pl.BlockSpec(memory_space=pltpu.VMEM),
                  full(consts["expand"]), full(consts["mask8"])],
        out_specs=pl.BlockSpec((tile * SUBLANES, LANES), lambda i: (i, 0)),
        out_shape=jax.ShapeDtypeStruct((n * SUBLANES, LANES), F32),
        scratch_shapes=_pipeline_scratch() + [pltpu.VMEM((tile, WIDE), F32), pltpu.VMEM((tile, WIDE), F32)],
        compiler_params=pltpu.CompilerParams(dimension_semantics=("arbitrary",), vmem_limit_bytes=VMEM_LIMIT),
        name="peer_v",
    )(e, w, x8, v_packed, consts["expand"], consts["mask8"])


def _peer_constants():
    lane = np.arange(WIDE)
    expand = (lane[None, :] // SUBLANES == np.arange(PEER_SLOTS)[:, None]).astype(np.float32)
    mask8 = (lane[None, :] % SUBLANES == np.arange(SUBLANES)[:, None]).astype(np.float32)
    return {"expand": jnp.asarray(expand, BF16), "collapse": jnp.asarray(expand.T, BF16), "mask8": jnp.asarray(mask8)}


def _pack_table(tab):
    n, d = tab.shape
    rows = PACK_ROWS

    def body(t_ref, o_ref):
        for j in range(ROW_SUB):
            lo = pltpu.bitcast(t_ref[:, 2 * j * LANES:(2 * j + 1) * LANES].astype(BF16).astype(F32), I32)
            hi = pltpu.bitcast(t_ref[:, (2 * j + 1) * LANES:(2 * j + 2) * LANES].astype(BF16).astype(F32), I32)
            o_ref[pl.ds(j, rows, stride=ROW_SUB), :] = hi | lax.shift_right_logical(lo, jnp.int32(16))

    return pl.pallas_call(
        body,
        grid=(n // rows,),
        in_specs=[pl.BlockSpec((rows, d), lambda i: (i, 0))],
        out_specs=pl.BlockSpec((rows * ROW_SUB, LANES), lambda i: (i, 0)),
        out_shape=jax.ShapeDtypeStruct((n * ROW_SUB, LANES), I32),
        compiler_params=pltpu.CompilerParams(dimension_semantics=("arbitrary",), vmem_limit_bytes=VMEM_LIMIT),
        name="pack_table",
    )(tab)


def _block_mean(width):
    blk = np.arange(width) // HEAD_DIM
    return jnp.asarray((blk[:, None] == blk[None, :]) / HEAD_DIM, BF16)


def _forget_placement():
    pfq = np.zeros((3 * LANES, A_HEADS * LANES), np.float32)
    pfk = np.zeros((3 * LANES, A_HEADS * LANES), np.float32)
    cq = np.zeros((1, A_HEADS * LANES), np.float32)
    ck = np.zeros((1, A_HEADS * LANES), np.float32)
    for hd in range(A_HEADS):
        base = hd * LANES + HEAD_DIM
        for piece in range(3):
            pfq[piece * LANES + hd, base + piece] = 1.0
            pfk[piece * LANES + hd, base + 3 + piece] = -1.0
            cq[0, base + 3 + piece] = 1.0
            ck[0, base + piece] = 1.0
    return jnp.asarray(pfq, BF16), jnp.asarray(pfk, BF16), jnp.asarray(cq), jnp.asarray(ck)


def _layer_weights(l, g_attn, w_in, b_f, g_q_a, g_k_a, conv_w, conv_b, g_q_m, w_out, g_ffn, w_peer_q, peer_keys):
    scale = HEAD_DIM ** -0.5
    splits = np.cumsum([A_WIDTH, A_WIDTH, A_WIDTH, A_HEADS, B_WIDTH, B_WIDTH, B_WIDTH, M_WIDTH])
    wi = w_in[l]
    qa, ka, va, fl, bg, cg, hv, qm = [wi[:, a:b] for a, b in zip(np.r_[0, splits[:-1]], splits)]
    w_packed = jnp.concatenate([qa, ka, va, bg, cg, hv, qm, fl, jnp.zeros((wi.shape[0], LANES - A_HEADS), wi.dtype)],
                               axis=1).astype(BF16)
    pfq, pfk, cq, ck = _forget_placement()
    keys = peer_keys[l]
    zeros = jnp.zeros_like(keys[:, 0])
    keys2 = jnp.concatenate([jnp.concatenate([keys[:, 0], zeros], axis=-1),
                             jnp.concatenate([zeros, keys[:, 1]], axis=-1)], axis=1).astype(BF16)
    return {
        "g_attn": g_attn[l][None, :], "w_in": w_packed,
        "b_f": jnp.pad(b_f[l], (0, LANES - A_HEADS))[None, :],
        "g_q": (jnp.tile(g_q_a[l], A_HEADS) * scale)[None, :], "g_k": jnp.tile(g_k_a[l], A_HEADS)[None, :],
        "g_qm": (jnp.tile(g_q_m[l], M_HEADS) * scale)[None, :],
        "bd512": _block_mean(A_WIDTH), "bd256": _block_mean(M_WIDTH),
        "pfq": pfq, "pfk": pfk, "cq": cq, "ck": ck,
        "conv_w": conv_w[l], "conv_b": conv_b[l][None, :],
        "w_out": w_out[l].astype(BF16), "g_ffn": g_ffn[l][None, :], "w_pq": w_peer_q[l].astype(BF16), "keys": keys2,
    }


def _peer_and_merge(x, oa, ob, om, wts, u_packed, v_packed, route_tile, peer_tile):
    b, s, d = x.shape
    n = b * s
    x1, xn, e, gates = _route_call(x.reshape(n, d), oa.reshape(n, -1), ob.reshape(n, -1), om.reshape(n, -1), wts, route_tile)
    consts = _peer_constants()
    w = _peer_u_call(e, xn.reshape(n * SUBLANES, LANES), gates, u_packed, consts, peer_tile)
    y8 = _peer_v_call(e, w, x1.reshape(n * SUBLANES, LANES), v_packed, consts, peer_tile)
    return y8.reshape(b, s, d)


def kernel(x_prompt, x_sample, cache_a_k, cache_a_v, cache_a_logf, cache_b_conv, cache_m_k, cache_m_v, mem_prompt, g_attn, w_in, b_f, g_q_a, g_k_a, conv_w, conv_b, g_mem, w_mem_k, w_mem_v, g_k_m, g_q_m, w_out, g_ffn, w_peer_q, peer_keys, peer_u, peer_v):
    depth = w_in.shape[0]
    xp, xs = x_prompt, x_sample
    bp, sp, _ = xp.shape
    bs, ts, _ = xs.shape
    outs = [[] for _ in range(10)]
    for l in range(depth):
        wts = _layer_weights(l, g_attn, w_in, b_f, g_q_a, g_k_a, conv_w, conv_b, g_q_m, w_out, g_ffn, w_peer_q, peer_keys)
        u_packed, v_packed = _pack_table(peer_u[l]), _pack_table(peer_v[l])
        fox_blk = min(512, sp)
        route_tile = 256
        peer_tile = 128

        mk, mv = _memkv_call(mem_prompt, g_mem[l][None, :], w_mem_k[l].astype(BF16), w_mem_v[l].astype(BF16),
                             jnp.tile(g_k_m[l], M_HEADS)[None, :], wts["bd256"])
        ka, va, logf, qaug, kaug, vb, ob, om, cst = _proj_call(
            xp, jnp.zeros((bp, 2, B_WIDTH), F32), mk, mv, wts, min(512, sp))
        oa = _fox_prompt_call(qaug, kaug, vb, fox_blk)
        xp = _peer_and_merge(xp, oa, ob, om, wts, u_packed, v_packed, route_tile, peer_tile)
        n_mem = mk.shape[1]
        for dst, val in zip(outs[:6], [ka.reshape(bp, sp, A_HEADS, HEAD_DIM), va.reshape(bp, sp, A_HEADS, HEAD_DIM), logf, cst,
                                       mk.reshape(bp, n_mem, M_HEADS, HEAD_DIM), mv.reshape(bp, n_mem, M_HEADS, HEAD_DIM)]):
            dst.append(val)

        past = cache_a_k.shape[2]
        ka, va, logf, qaug, kaug, vb, ob, om, cst = _proj_call(
            xs, cache_b_conv[l], cache_m_k[l].reshape(bs, -1, M_WIDTH), cache_m_v[l].reshape(bs, -1, M_WIDTH), wts, ts)
        clf = jnp.pad(cache_a_logf[l], ((0, 0), (0, 0), (0, LANES - A_HEADS)))
        oa = _fox_sample_call(cache_a_k[l].reshape(bs, past, A_WIDTH), cache_a_v[l].reshape(bs, past, A_WIDTH), clf,
                              qaug, kaug, vb, wts["pfk"], wts["ck"])
        xs = _peer_and_merge(xs, oa, ob, om, wts, u_packed, v_packed, route_tile, peer_tile)
        for dst, val in zip(outs[6:], [ka.reshape(bs, ts, A_HEADS, HEAD_DIM), va.reshape(bs, ts, A_HEADS, HEAD_DIM), logf, cst]):
            dst.append(val)

    return (xp, xs) + tuple(jnp.stack(o) for o in outs)
```

```python
import functools

import numpy as np
import jax
import jax.numpy as jnp
from jax import lax
from jax.experimental import pallas as pl
from jax.experimental.pallas import tpu as pltpu

F32, BF16, I32 = jnp.float32, jnp.bfloat16, jnp.int32
EPS = 1e-6
LANES = 128
SUBLANES = 8
HEAD_DIM = 64
A_HEADS = 8
A_WIDTH = A_HEADS * HEAD_DIM
B_WIDTH = 256
M_HEADS = 4
M_WIDTH = M_HEADS * HEAD_DIM
PEER_HEADS = 8
PEER_KEYS = 128
PEER_TOPK = 16
PEER_SLOTS = PEER_HEADS * PEER_TOPK
D_MODEL = 1024
ROW_WORDS = D_MODEL // 2
ROW_SUB = ROW_WORDS // LANES
NEG_BIG = -1e30
VMEM_LIMIT = 56 * 1024 * 1024

C_Q, C_K, C_V, C_BG, C_CG, C_HV, C_QM, C_FL, C_END = 0, 512, 1024, 1536, 1792, 2048, 2304, 2560, 2688


def _dot(a, b):
    return jnp.dot(a, b, preferred_element_type=F32)


def _dot_nt(a, b):
    return lax.dot_general(a, b, (((1,), (1,)), ((), ())), preferred_element_type=F32)


def _split3(x):
    hi = x.astype(BF16)
    r1 = x - hi.astype(F32)
    mid = r1.astype(BF16)
    lo = (r1 - mid.astype(F32)).astype(BF16)
    return hi, mid, lo


def _lane(shape):
    return lax.broadcasted_iota(I32, shape, len(shape) - 1)


def _memkv_body(mem_ref, g_ref, wk_ref, wv_ref, gk_ref, bd_ref, mk_ref, mv_ref):
    x = mem_ref[0]
    h = (x * lax.rsqrt(jnp.mean(x * x, axis=-1, keepdims=True) + EPS) * g_ref[...]).astype(BF16)
    zk = _dot(h, wk_ref[...])
    ms = _dot((zk * zk).astype(BF16), bd_ref[...])
    mk_ref[0] = zk * lax.rsqrt(ms + EPS) * gk_ref[...]
    mv_ref[0] = _dot(h, wv_ref[...])


def _memkv_call(mem, g_mem, w_mk, w_mv, gk_t, bd256):
    b, n_mem, d = mem.shape
    full = lambda shape: pl.BlockSpec(shape, lambda i: (0,) * len(shape))
    return pl.pallas_call(
        _memkv_body,
        grid=(b,),
        in_specs=[pl.BlockSpec((1, n_mem, d), lambda i: (i, 0, 0)), full((1, d)), full((d, M_WIDTH)),
                  full((d, M_WIDTH)), full((1, M_WIDTH)), full((M_WIDTH, M_WIDTH))],
        out_specs=[pl.BlockSpec((1, n_mem, M_WIDTH), lambda i: (i, 0, 0))] * 2,
        out_shape=[jax.ShapeDtypeStruct((b, n_mem, M_WIDTH), F32)] * 2,
        compiler_params=pltpu.CompilerParams(dimension_semantics=("arbitrary",), vmem_limit_bytes=VMEM_LIMIT),
        name="mem_kv",
    )(mem, g_mem, w_mk, w_mv, gk_t, bd256)


def _proj_body(x_ref, prev_ref, mk_ref, mv_ref, gat_ref, w_ref, bf_ref, gq_ref, gk_ref, gqm_ref,
               bd512_ref, bd256_ref, pfq_ref, pfk_ref, cq_ref, ck_ref, cw_ref, cb_ref,
               ka_ref, va_ref, logf_ref, qaug_ref, kaug_ref, vb_ref, ob_ref, om_ref, cst_ref,
               fcarry, ucarry):
    t = pl.program_id(1)
    tt = x_ref.shape[1]
    x = x_ref[0]
    h = (x * lax.rsqrt(jnp.mean(x * x, axis=-1, keepdims=True) + EPS) * gat_ref[...]).astype(BF16)

    def proj(c0, c1):
        return _dot(h, w_ref[:, c0:c1])

    @pl.when(t == 0)
    def _():
        fcarry[...] = jnp.zeros_like(fcarry)
        ucarry[...] = jnp.zeros_like(ucarry)
        ucarry[SUBLANES - 2:SUBLANES, :] = prev_ref[0]

    v = proj(C_FL, C_END) + bf_ref[...]
    logf = jnp.minimum(v, 0.0) - jnp.log1p(jnp.exp(-jnp.abs(v)))
    logf = jnp.where(_lane(logf.shape) < A_HEADS, logf, 0.0)
    logf_ref[0] = logf[:, :A_HEADS]
    row = lax.broadcasted_iota(I32, (tt, tt), 0)
    col = lax.broadcasted_iota(I32, (tt, tt), 1)
    tri = jnp.where(row >= col, 1.0, 0.0).astype(BF16)
    lh, lm, ll = _split3(logf)
    fcum = _dot(tri, lh) + _dot(tri, lm) + _dot(tri, ll) + fcarry[0:1, :]
    fcarry[...] = jnp.broadcast_to(fcum[tt - 1:tt, :], fcarry.shape)
    fparts = jnp.concatenate(_split3(fcum), axis=1)
    faq = _dot(fparts, pfq_ref[...]) + cq_ref[...]
    fak = _dot(fparts, pfk_ref[...]) + ck_ref[...]

    lane = _lane((tt, LANES))
    zq = proj(C_Q, C_K)
    qn = zq * lax.rsqrt(_dot((zq * zq).astype(BF16), bd512_ref[...]) + EPS) * gq_ref[...]
    zk = proj(C_K, C_V)
    kn = zk * lax.rsqrt(_dot((zk * zk).astype(BF16), bd512_ref[...]) + EPS) * gk_ref[...]
    ka_ref[0] = kn
    for hd in range(A_HEADS):
        c0 = LANES * (hd // 2)
        qt, kt = qn[:, c0:c0 + LANES], kn[:, c0:c0 + LANES]
        if hd % 2:
            qt, kt = pltpu.roll(qt, HEAD_DIM, 1), pltpu.roll(kt, HEAD_DIM, 1)
        qaug_ref[0, hd] = jnp.where(lane < HEAD_DIM, qt, faq[:, LANES * hd:LANES * (hd + 1)]).astype(BF16)
        kaug_ref[0, hd] = jnp.where(lane < HEAD_DIM, kt, fak[:, LANES * hd:LANES * (hd + 1)]).astype(BF16)
    zv = proj(C_V, C_BG)
    va_ref[0] = zv
    vb_ref[0] = zv.astype(BF16)

    u = proj(C_CG, C_HV) * proj(C_HV, C_QM)
    rows = lax.broadcasted_iota(I32, u.shape, 0)
    p1 = ucarry[SUBLANES - 1:SUBLANES, :]
    p2 = ucarry[SUBLANES - 2:SUBLANES - 1, :]
    u1 = jnp.where(rows == 0, p1, pltpu.roll(u, 1, 0))
    u2 = jnp.where(rows == 0, p2, jnp.where(rows == 1, p1, pltpu.roll(u, 2, 0)))
    cy = cb_ref[...] + cw_ref[2:3, :] * u + cw_ref[0:1, :] * u2 + cw_ref[1:2, :] * u1
    ob_ref[0] = (proj(C_BG, C_CG) * cy).astype(BF16)
    ucarry[...] = u[tt - SUBLANES:tt, :]
    cst_ref[0] = u[tt - 2:tt, :]

    zm = proj(C_QM, C_FL)
    qm = zm * lax.rsqrt(_dot((zm * zm).astype(BF16), bd256_ref[...]) + EPS) * gqm_ref[...]
    mkb = mk_ref[0].astype(BF16)
    mvb = mv_ref[0].astype(BF16)
    outs = []
    for pr in range(M_HEADS // 2):
        qp = qm[:, LANES * pr:LANES * (pr + 1)]
        kp = mkb[:, LANES * pr:LANES * (pr + 1)]
        vp = mvb[:, LANES * pr:LANES * (pr + 1)]
        o = []
        for sub in range(2):
            keep = (lane < HEAD_DIM) if sub == 0 else (lane >= HEAD_DIM)
            s = _dot_nt(jnp.where(keep, qp, 0.0).astype(BF16), kp)
            p = jnp.exp(s - jnp.max(s, axis=-1, keepdims=True))
            o.append(_dot(p.astype(BF16), vp) / jnp.sum(p, axis=-1, keepdims=True))
        outs.append(jnp.where(lane < HEAD_DIM, o[0], o[1]))
    om_ref[0] = jnp.concatenate(outs, axis=1).astype(BF16)


def _proj_call(x, prev, mk, mv, wts, tile):
    b, s, d = x.shape
    nt = s // tile
    n_mem = mk.shape[1]
    full = lambda a: pl.BlockSpec(a.shape, lambda i, j: (0,) * a.ndim)
    seq = lambda w: pl.BlockSpec((1, tile, w), lambda i, j: (i, j, 0))
    per_b = lambda r, w: pl.BlockSpec((1, r, w), lambda i, j: (i, 0, 0))
    heads = pl.BlockSpec((1, A_HEADS, tile, LANES), lambda i, j: (i, 0, j, 0))
    names = ["g_attn", "w_in", "b_f", "g_q", "g_k", "g_qm", "bd512", "bd256", "pfq", "pfk", "cq", "ck", "conv_w", "conv_b"]
    consts = [wts[k] for k in names]
    out_shape = [
        jax.ShapeDtypeStruct((b, s, A_WIDTH), F32),
        jax.ShapeDtypeStruct((b, s, A_WIDTH), F32),
        jax.ShapeDtypeStruct((b, s, A_HEADS), F32),
        jax.ShapeDtypeStruct((b, A_HEADS, s, LANES), BF16),
        jax.ShapeDtypeStruct((b, A_HEADS, s, LANES), BF16),
        jax.ShapeDtypeStruct((b, s, A_WIDTH), BF16),
        jax.ShapeDtypeStruct((b, s, B_WIDTH), BF16),
        jax.ShapeDtypeStruct((b, s, M_WIDTH), BF16),
        jax.ShapeDtypeStruct((b, 2, B_WIDTH), F32),
    ]
    out_specs = [seq(A_WIDTH), seq(A_WIDTH), seq(A_HEADS), heads, heads, seq(A_WIDTH), seq(B_WIDTH), seq(M_WIDTH),
                 per_b(2, B_WIDTH)]
    return pl.pallas_call(
        _proj_body,
        grid=(b, nt),
        in_specs=[seq(d), per_b(2, B_WIDTH), per_b(n_mem, M_WIDTH), per_b(n_mem, M_WIDTH)] + [full(c) for c in consts],
        out_specs=out_specs,
        out_shape=out_shape,
        scratch_shapes=[pltpu.VMEM((SUBLANES, LANES), F32), pltpu.VMEM((SUBLANES, B_WIDTH), F32)],
        compiler_params=pltpu.CompilerParams(dimension_semantics=("arbitrary", "arbitrary"), vmem_limit_bytes=VMEM_LIMIT),
        name="proj",
    )(x, prev, mk, mv, *consts)


def _fox_prompt_body(q_ref, k_ref, v_ref, o_ref, *, bq, bk):
    qi = pl.program_id(2)
    ratio = bq // bk
    ones = jnp.ones((bk, LANES), BF16)
    row = lax.broadcasted_iota(I32, (bq, bk), 0)
    col = lax.broadcasted_iota(I32, (bq, bk), 1)

    def step(kj, carry, masked):
        off = pl.multiple_of(kj * bk, bk)
        vv = jnp.concatenate([v_ref[0, pl.ds(off, bk), :], ones], axis=1)
        new = []
        for sub in range(2):
            m, acc = carry[sub]
            s = _dot_nt(q_ref[0, sub], k_ref[0, sub, pl.ds(off, bk), :])
            if masked:
                s = jnp.where(col + (kj * bk - qi * bq) <= row, s, NEG_BIG)
            m_new = jnp.maximum(m, jnp.max(s, axis=-1, keepdims=True))
            p = jnp.exp(s - m_new).astype(BF16)
            new.append((m_new, jnp.exp(m - m_new) * acc + _dot(p, vv)))
        return tuple(new)

    init = ((jnp.full((bq, 1), NEG_BIG, F32), jnp.zeros((bq, 2 * LANES), F32)),) * 2
    carry = lax.fori_loop(0, qi * ratio, functools.partial(step, masked=False), init)
    for d in range(ratio):
        carry = step(qi * ratio + d, carry, True)
    outs = [acc[:, :LANES] / acc[:, LANES:] for _, acc in carry]
    o_ref[0] = jnp.where(_lane((bq, LANES)) < HEAD_DIM, outs[0], outs[1]).astype(BF16)


def _fox_prompt_call(qaug, kaug, vb, bq, bk):
    b, _, s, _ = qaug.shape
    return pl.pallas_call(
        functools.partial(_fox_prompt_body, bq=bq, bk=bk),
        grid=(b, A_HEADS // 2, s // bq),
        in_specs=[pl.BlockSpec((1, 2, bq, LANES), lambda i, hp, j: (i, hp, j, 0)),
                  pl.BlockSpec((1, 2, s, LANES), lambda i, hp, j: (i, hp, 0, 0)),
                  pl.BlockSpec((1, s, LANES), lambda i, hp, j: (i, 0, hp))],
        out_specs=pl.BlockSpec((1, bq, LANES), lambda i, hp, j: (i, j, hp)),
        out_shape=jax.ShapeDtypeStruct((b, s, A_WIDTH), BF16),
        compiler_params=pltpu.CompilerParams(dimension_semantics=("arbitrary",) * 3, vmem_limit_bytes=VMEM_LIMIT),
        name="fox_prompt",
    )(qaug, kaug, vb)


def _fox_sample_body(ck_ref, cv_ref, clf_ref, q_ref, k_ref, v_ref, pfk_ref, ckc_ref, o_ref):
    past = ck_ref.shape[1]
    ts = q_ref.shape[2]
    row = lax.broadcasted_iota(I32, (past, past), 0)
    col = lax.broadcasted_iota(I32, (past, past), 1)
    tri = jnp.where(col > row, 1.0, 0.0).astype(BF16)
    lh, lm, ll = _split3(clf_ref[0])
    suffix = _dot(tri, lh) + _dot(tri, lm) + _dot(tri, ll)
    fak = _dot(jnp.concatenate(_split3(-suffix), axis=1), pfk_ref[...]) + ckc_ref[...]
    lane = _lane((past, LANES))
    lane_s = _lane((ts, LANES))
    causal = lax.broadcasted_iota(I32, (ts, ts), 1) <= lax.broadcasted_iota(I32, (ts, ts), 0)
    outs = []
    for pr in range(A_HEADS // 2):
        kc2 = ck_ref[0, :, LANES * pr:LANES * (pr + 1)]
        vc = cv_ref[0, :, LANES * pr:LANES * (pr + 1)].astype(BF16)
        vn = v_ref[0, :, LANES * pr:LANES * (pr + 1)]
        o = []
        for sub in range(2):
            hd = 2 * pr + sub
            kt = pltpu.roll(kc2, HEAD_DIM, 1) if sub else kc2
            kc = jnp.where(lane < HEAD_DIM, kt, fak[:, LANES * hd:LANES * (hd + 1)]).astype(BF16)
            q = q_ref[0, hd]
            s1 = _dot_nt(q, kc)
            s2 = jnp.where(causal, _dot_nt(q, k_ref[0, hd]), NEG_BIG)
            m = jnp.maximum(jnp.max(s1, axis=-1, keepdims=True), jnp.max(s2, axis=-1, keepdims=True))
            p1, p2 = jnp.exp(s1 - m), jnp.exp(s2 - m)
            den = jnp.sum(p1, axis=-1, keepdims=True) + jnp.sum(p2, axis=-1, keepdims=True)
            o.append((_dot(p1.astype(BF16), vc) + _dot(p2.astype(BF16), vn)) / den)
        outs.append(jnp.where(lane_s < HEAD_DIM, o[0], o[1]))
    o_ref[0] = jnp.concatenate(outs, axis=1).astype(BF16)


def _fox_sample_call(cache_k, cache_v, cache_lf, qaug, kaug, vb, pfk, ck):
    b, past, _ = cache_k.shape
    ts = qaug.shape[2]
    full = lambda a: pl.BlockSpec(a.shape, lambda i: (0,) * a.ndim)
    return pl.pallas_call(
        _fox_sample_body,
        grid=(b,),
        in_specs=[pl.BlockSpec((1, past, A_WIDTH), lambda i: (i, 0, 0)),
                  pl.BlockSpec((1, past, A_WIDTH), lambda i: (i, 0, 0)),
                  pl.BlockSpec((1, past, LANES), lambda i: (i, 0, 0)),
                  pl.BlockSpec((1, A_HEADS, ts, LANES), lambda i: (i, 0, 0, 0)),
                  pl.BlockSpec((1, A_HEADS, ts, LANES), lambda i: (i, 0, 0, 0)),
                  pl.BlockSpec((1, ts, A_WIDTH), lambda i: (i, 0, 0)),
                  full(pfk), full(ck)],
        out_specs=pl.BlockSpec((1, ts, A_WIDTH), lambda i: (i, 0, 0)),
        out_shape=jax.ShapeDtypeStruct((b, ts, A_WIDTH), BF16),
        compiler_params=pltpu.CompilerParams(dimension_semantics=("arbitrary",), vmem_limit_bytes=VMEM_LIMIT),
        name="fox_sample",
    )(cache_k, cache_v, cache_lf, qaug, kaug, vb, pfk, ck)


def _all_sublanes(x, op):
    for shift in (4, 2, 1):
        x = op(x, pltpu.roll(x, shift, 0))
    return x


def _top_groups(groups, payload, k, sub):
    groups = list(groups)
    big = SUBLANES * len(groups)
    out = []
    for _ in range(k):
        level = [(v, g) for g, v in enumerate(groups)]
        while len(level) > 1:
            merged = []
            for (va, ga), (vb, gb) in zip(level[0::2], level[1::2]):
                take = vb > va
                merged.append((jnp.where(take, vb, va), jnp.where(take, gb, ga)))
            level = merged + ([level[-1]] if len(level) % 2 else [])
        best, where_g = level[0]
        top = _all_sublanes(best, jnp.maximum)
        row = _all_sublanes(jnp.where(best == top, where_g * SUBLANES + sub, big), jnp.minimum)
        rel = row - sub
        hits = [rel == SUBLANES * g for g in range(len(groups))]
        if payload is None:
            out.append((top, row))
        else:
            pay = jnp.where(hits[0], payload[0], -1)
            for g in range(1, len(groups)):
                pay = jnp.where(hits[g], payload[g], pay)
            out.append((top, _all_sublanes(pay, jnp.maximum)))
        groups = [jnp.where(h, -jnp.inf, v) for h, v in zip(hits, groups)]
    return out


def _pack_rows(rows, sub2):
    acc = jnp.zeros(sub2.shape, rows[0].dtype)
    for j, r in enumerate(rows):
        acc = jnp.where(sub2 == j, jnp.concatenate([r, r], axis=0), acc)
    return acc


def _route_head(sc, sub, sub2):
    groups = lambda a: [a[SUBLANES * g:SUBLANES * (g + 1)] for g in range(a.shape[0] // SUBLANES)]
    t1 = _top_groups(groups(sc[:PEER_KEYS]), None, PEER_TOPK, sub)
    t2 = _top_groups(groups(sc[PEER_KEYS:]), None, PEER_TOPK, sub)
    s1, i1 = [v for v, _ in t1], [i * PEER_KEYS for _, i in t1]
    s2p, i2p = _pack_rows([v for v, _ in t2], sub2), _pack_rows([i for _, i in t2], sub2)
    s1p, i1p = _pack_rows(s1, sub2), _pack_rows(i1, sub2)
    lo, hi = slice(0, SUBLANES), slice(SUBLANES, 2 * SUBLANES)
    half = sub < 4
    rep4 = lambda a: jnp.where(half, a[lo], pltpu.roll(a[lo], 4, 0))
    pick = lambda a, b: jnp.where(half, a, b)

    def pairs(first, first_packed, second_packed, second0):
        return [first[0] + second_packed[lo], first[0] + second_packed[hi], first[1] + second_packed[lo],
                first[2] + second_packed[lo], first[3] + second_packed[lo],
                pick(first[4], first[5]) + rep4(second_packed), pick(first[6], first[7]) + rep4(second_packed),
                first_packed[hi] + second0]

    cand = pairs(s1, s1p, s2p, t2[0][0])
    expert = pairs(i1, i1p, i2p, t2[0][1])
    best = _top_groups(cand, expert, PEER_TOPK, sub)
    top = _pack_rows([v for v, _ in best], sub2)
    p = jnp.exp(top - best[0][0][0:1, :])
    gates = p / jnp.sum(p, axis=0, keepdims=True)
    return _pack_rows([e for _, e in best], sub2) * ROW_SUB, gates


def _route_body(x_ref, oa_ref, ob_ref, om_ref, wo_ref, gf_ref, wq_ref, keys_ref,
                x1_ref, xn_ref, e_ref, g_ref, qp_scr, e_scr, g_scr):
    tile, d_model = x_ref.shape
    lane_tiles = tile // LANES
    y = x_ref[...] + _dot(oa_ref[...], wo_ref[0:A_WIDTH, :]) \
        + _dot(ob_ref[...], wo_ref[A_WIDTH:A_WIDTH + B_WIDTH, :]) \
        + _dot(om_ref[...], wo_ref[A_WIDTH + B_WIDTH:, :])
    xn = y * lax.rsqrt(jnp.mean(y * y, axis=-1, keepdims=True) + EPS) * gf_ref[...]
    for c in range(d_model // LANES):
        rows = pl.ds(c, tile, stride=SUBLANES)
        x1_ref[rows, :] = y[:, LANES * c:LANES * (c + 1)]
        xn_ref[rows, :] = xn[:, LANES * c:LANES * (c + 1)]
    qp = _dot(xn.astype(BF16), wq_ref[...])
    for hd in range(PEER_HEADS):
        qp_scr[hd] = qp[:, LANES * hd:LANES * (hd + 1)].astype(BF16)
    sub = lax.broadcasted_iota(I32, (SUBLANES, LANES), 0)
    sub2 = lax.broadcasted_iota(I32, (2 * SUBLANES, LANES), 0)

    def head_pair(hp, carry):
        for hd in (2 * hp, 2 * hp + 1):
            for lt in range(lane_tiles):
                q = qp_scr[hd, LANES * lt:LANES * (lt + 1), :]
                e, gates = _route_head(_dot_nt(keys_ref[hd], q), sub, sub2)
                e_scr[lt, hd] = e
                g_scr[lt, hd] = gates
        return carry

    lax.fori_loop(0, PEER_HEADS // 2, head_pair, 0)
    for lt in range(lane_tiles):
        rows = slice(LANES * lt, LANES * (lt + 1))
        e_ref[rows, :] = jnp.concatenate([e_scr[lt, hd] for hd in range(PEER_HEADS)], axis=0).T
        g_ref[rows, :] = jnp.concatenate([g_scr[lt, hd] for hd in range(PEER_HEADS)], axis=0).T


def _route_call(x, oa, ob, om, wts, tile):
    n, d = x.shape
    full = lambda a: pl.BlockSpec(a.shape, lambda i: (0,) * a.ndim)
    rows = lambda w: pl.BlockSpec((tile, w), lambda i: (i, 0))
    consts = [wts[k] for k in ["w_out", "g_ffn", "w_pq", "keys"]]
    return pl.pallas_call(
        _route_body,
        grid=(n // tile,),
        in_specs=[rows(d), rows(A_WIDTH), rows(B_WIDTH), rows(M_WIDTH)] + [full(c) for c in consts],
        out_specs=[pl.BlockSpec((tile * SUBLANES, LANES), lambda i: (i, 0))] * 2 + [rows(PEER_SLOTS), rows(PEER_SLOTS)],
        out_shape=[jax.ShapeDtypeStruct((n * d // LANES, LANES), F32)] * 2 + [
                   jax.ShapeDtypeStruct((n, PEER_SLOTS), I32), jax.ShapeDtypeStruct((n, PEER_SLOTS), F32)],
        scratch_shapes=[pltpu.VMEM((PEER_HEADS, tile, LANES), BF16),
                        pltpu.VMEM((tile // LANES, PEER_HEADS, PEER_TOPK, LANES), I32),
                        pltpu.VMEM((tile // LANES, PEER_HEADS, PEER_TOPK, LANES), F32)],
        compiler_params=pltpu.CompilerParams(dimension_semantics=("arbitrary",), vmem_limit_bytes=VMEM_LIMIT),
        name="merge_route",
    )(x, oa, ob, om, *consts)


GROUP_ROWS = PEER_SLOTS * ROW_SUB
WIDE = PEER_SLOTS * SUBLANES
PIPE_TOKENS = 64
PACK_ROWS = 512


def _gather_token(idx_ref, row, tab_ref, buf):
    for k in range(PEER_SLOTS):
        buf[k * ROW_SUB:(k + 1) * ROW_SUB, :] = tab_ref[pl.ds(pl.multiple_of(idx_ref[row, k], ROW_SUB), ROW_SUB), :]


def _pipelined_tokens(e_hbm, tab_ref, bufs, idx, sems, compute, tile):
    blocks = tile // PIPE_TOKENS
    first_block = pl.program_id(0) * blocks
    all_blocks = pl.num_programs(0) * blocks

    def idx_copy(block, slot):
        rows = pl.ds(pl.multiple_of(block * PIPE_TOKENS, PIPE_TOKENS), PIPE_TOKENS)
        return pltpu.make_async_copy(e_hbm.at[rows, :], idx[slot], sems.at[slot])

    @pl.when(pl.program_id(0) == 0)
    def _():
        idx_copy(0, 0).start()
        idx_copy(1, 1).start()
        idx_copy(0, 0).wait()
        _gather_token(idx[0], 0, tab_ref, bufs[0])
        _gather_token(idx[0], 1, tab_ref, bufs[1])

    def block_pair(p, carry):
        for slot in range(2):
            local = 2 * p + slot
            block = first_block + local
            for half in range(PIPE_TOKENS // 2):
                cur = bufs[2 * (half % 2):2 * (half % 2) + 2]
                nxt = bufs[2 - 2 * (half % 2):4 - 2 * (half % 2)]
                for j in range(2):
                    compute(local * PIPE_TOKENS + 2 * half + j, cur[j])
                if half + 1 < PIPE_TOKENS // 2:
                    for j in range(2):
                        _gather_token(idx[slot], 2 * half + 2 + j, tab_ref, nxt[j])
                else:
                    @pl.when(block + 1 < all_blocks)
                    def _():
                        idx_copy(block + 1, 1 - slot).wait()
                        for j in range(2):
                            _gather_token(idx[1 - slot], j, tab_ref, nxt[j])

            @pl.when(block + 2 < all_blocks)
            def _():
                idx_copy(block + 2, slot).start()
        return carry

    lax.fori_loop(0, blocks // 2, block_pair, 0)


def _pipeline_scratch():
    return ([pltpu.VMEM((GROUP_ROWS, LANES), I32)] * 4 + [pltpu.SMEM((PIPE_TOKENS, PEER_SLOTS), I32)] * 2
            + [pltpu.SemaphoreType.DMA((2,))])


def _token_tile_rows(t):
    return pl.ds(pl.multiple_of(t * SUBLANES, SUBLANES), SUBLANES)


def _split2(x):
    hi = x.astype(BF16).astype(F32)
    return jnp.concatenate([hi, x - hi], axis=0).astype(BF16)


def _peer_u_body(e_hbm, x_ref, g_ref, u_ref, col_ref, mask_ref, w_ref, b0, b1, b2, b3, i0, i1, sems, z_scr, *, tile, chunk):
    mask = mask_ref[...]

    def compute(t, buf):
        z = _dot_nt(_split2(x_ref[_token_tile_rows(t), :]), pltpu.bitcast(buf[...], BF16))
        z_scr[_token_tile_rows(t), :] = (z[:SUBLANES] + z[SUBLANES:]) * mask

    _pipelined_tokens(e_hbm, u_ref, (b0, b1, b2, b3), (i0, i1), sems, compute, tile)
    for c in range(tile // chunk):
        zs = z_scr[c * chunk * SUBLANES:(c + 1) * chunk * SUBLANES, :]
        zh = zs.astype(BF16)
        zl = (zs - zh.astype(F32)).astype(BF16)
        part = _dot(zh, col_ref[...]) + _dot(zl, col_ref[...])
        a = jnp.sum(part.reshape(chunk, SUBLANES, PEER_SLOTS), axis=1)
        rows = slice(c * chunk, (c + 1) * chunk)
        w_ref[rows, :] = g_ref[rows, :] * (0.5 * a * (1.0 + lax.erf(a * np.float32(np.sqrt(0.5)))))


def _peer_u_call(e, xn8, gates, u_packed, consts, tile):
    n = e.shape[0]
    full = lambda a: pl.BlockSpec(a.shape, lambda i: (0,) * a.ndim)
    return pl.pallas_call(
        functools.partial(_peer_u_body, tile=tile, chunk=min(tile, 32)),
        grid=(n // tile,),
        in_specs=[pl.BlockSpec(memory_space=pl.ANY),
                  pl.BlockSpec((tile * SUBLANES, LANES), lambda i: (i, 0)),
                  pl.BlockSpec((tile, PEER_SLOTS), lambda i: (i, 0)),
                  pl.BlockSpec(memory_space=pltpu.VMEM),
                  full(consts["collapse"]), full(consts["mask8"])],
        out_specs=pl.BlockSpec((tile, PEER_SLOTS), lambda i: (i, 0)),
        out_shape=jax.ShapeDtypeStruct((n, PEER_SLOTS), F32),
        scratch_shapes=_pipeline_scratch() + [pltpu.VMEM((tile * SUBLANES, WIDE), F32)],
        compiler_params=pltpu.CompilerParams(dimension_semantics=("arbitrary",), vmem_limit_bytes=VMEM_LIMIT),
        name="peer_u",
    )(e, xn8, gates, u_packed, consts["collapse"], consts["mask8"])


def _peer_v_body(e_hbm, w_ref, x_ref, v_ref, exp_ref, mask_ref, y_ref, b0, b1, b2, b3, i0, i1, sems, eh_scr, el_scr, *, tile):
    mask = mask_ref[...]
    w = w_ref[...]
    wh = w.astype(BF16)
    eh_scr[...] = _dot(wh, exp_ref[...])
    el_scr[...] = _dot((w - wh.astype(F32)).astype(BF16), exp_ref[...])

    def compute(t, buf):
        row = pl.ds(t, 1)
        lhs = jnp.concatenate([jnp.broadcast_to(eh_scr[row, :], (SUBLANES, WIDE)) * mask,
                               jnp.broadcast_to(el_scr[row, :], (SUBLANES, WIDE)) * mask], axis=0).astype(BF16)
        o = _dot(lhs, pltpu.bitcast(buf[...], BF16))
        rows = _token_tile_rows(t)
        y_ref[rows, :] = x_ref[rows, :] + o[:SUBLANES] + o[SUBLANES:]

    _pipelined_tokens(e_hbm, v_ref, (b0, b1, b2, b3), (i0, i1), sems, compute, tile)


def _peer_v_call(e, w, x8, v_packed, consts, tile):
    n = e.shape[0]
    full = lambda a: pl.BlockSpec(a.shape, lambda i: (0,) * a.ndim)
    return pl.pallas_call(
        functools.partial(_peer_v_body, tile=tile),
        grid=(n // tile,),
        in_specs=[pl.BlockSpec(memory_space=pl.ANY),
                  pl.BlockSpec((tile, PEER_SLOTS), lambda i: (i, 0)),
                  pl.BlockSpec((tile * SUBLANES, LANES), lambda i: (i, 0)),
                  pl.BlockSpec(memory_space=pltpu.VMEM),
                  full(consts["expand"]), full(consts["mask8"])],
        out_specs=pl.BlockSpec((tile * SUBLANES, LANES), lambda i: (i, 0)),
        out_shape=jax.ShapeDtypeStruct((n * SUBLANES, LANES), F32),
        scratch_shapes=_pipeline_scratch() + [pltpu.VMEM((tile, WIDE), F32), pltpu.VMEM((tile, WIDE), F32)],
        compiler_params=pltpu.CompilerParams(dimension_semantics=("arbitrary",), vmem_limit_bytes=VMEM_LIMIT),
        name="peer_v",
    )(e, w, x8, v_packed, consts["expand"], consts["mask8"])


def _peer_constants():
    lane = np.arange(WIDE)
    expand = (lane[None, :] // SUBLANES == np.arange(PEER_SLOTS)[:, None]).astype(np.float32)
    mask8 = (lane[None, :] % SUBLANES == np.arange(SUBLANES)[:, None]).astype(np.float32)
    return {"expand": jnp.asarray(expand, BF16), "collapse": jnp.asarray(expand.T, BF16), "mask8": jnp.asarray(mask8)}


def _pack_table(tab):
    n, d = tab.shape
    rows = PACK_ROWS

    def body(t_ref, o_ref):
        for j in range(ROW_SUB):
            lo = pltpu.bitcast(t_ref[:, 2 * j * LANES:(2 * j + 1) * LANES].astype(BF16).astype(F32), I32)
            hi = pltpu.bitcast(t_ref[:, (2 * j + 1) * LANES:(2 * j + 2) * LANES].astype(BF16).astype(F32), I32)
            o_ref[pl.ds(j, rows, stride=ROW_SUB), :] = hi | lax.shift_right_logical(lo, jnp.int32(16))

    return pl.pallas_call(
        body,
        grid=(n // rows,),
        in_specs=[pl.BlockSpec((rows, d), lambda i: (i, 0))],
        out_specs=pl.BlockSpec((rows * ROW_SUB, LANES), lambda i: (i, 0)),
        out_shape=jax.ShapeDtypeStruct((n * ROW_SUB, LANES), I32),
        compiler_params=pltpu.CompilerParams(dimension_semantics=("arbitrary",), vmem_limit_bytes=VMEM_LIMIT),
        name="pack_table",
    )(tab)


def _block_mean(width):
    blk = np.arange(width) // HEAD_DIM
    return jnp.asarray((blk[:, None] == blk[None, :]) / HEAD_DIM, BF16)


def _forget_placement():
    pfq = np.zeros((3 * LANES, A_HEADS * LANES), np.float32)
    pfk = np.zeros((3 * LANES, A_HEADS * LANES), np.float32)
    cq = np.zeros((1, A_HEADS * LANES), np.float32)
    ck = np.zeros((1, A_HEADS * LANES), np.float32)
    for hd in range(A_HEADS):
        base = hd * LANES + HEAD_DIM
        for piece in range(3):
            pfq[piece * LANES + hd, base + piece] = 1.0
            pfk[piece * LANES + hd, base + 3 + piece] = -1.0
            cq[0, base + 3 + piece] = 1.0
            ck[0, base + piece] = 1.0
    return jnp.asarray(pfq, BF16), jnp.asarray(pfk, BF16), jnp.asarray(cq), jnp.asarray(ck)


def _layer_weights(l, g_attn, w_in, b_f, g_q_a, g_k_a, conv_w, conv_b, g_q_m, w_out, g_ffn, w_peer_q, peer_keys):
    scale = HEAD_DIM ** -0.5
    splits = np.cumsum([A_WIDTH, A_WIDTH, A_WIDTH, A_HEADS, B_WIDTH, B_WIDTH, B_WIDTH, M_WIDTH])
    wi = w_in[l]
    qa, ka, va, fl, bg, cg, hv, qm = [wi[:, a:b] for a, b in zip(np.r_[0, splits[:-1]], splits)]
    w_packed = jnp.concatenate([qa, ka, va, bg, cg, hv, qm, fl, jnp.zeros((wi.shape[0], LANES - A_HEADS), wi.dtype)],
                               axis=1).astype(BF16)
    pfq, pfk, cq, ck = _forget_placement()
    keys = peer_keys[l]
    zeros = jnp.zeros_like(keys[:, 0])
    keys2 = jnp.concatenate([jnp.concatenate([keys[:, 0], zeros], axis=-1),
                             jnp.concatenate([zeros, keys[:, 1]], axis=-1)], axis=1).astype(BF16)
    return {
        "g_attn": g_attn[l][None, :], "w_in": w_packed,
        "b_f": jnp.pad(b_f[l], (0, LANES - A_HEADS))[None, :],
        "g_q": (jnp.tile(g_q_a[l], A_HEADS) * scale)[None, :], "g_k": jnp.tile(g_k_a[l], A_HEADS)[None, :],
        "g_qm": (jnp.tile(g_q_m[l], M_HEADS) * scale)[None, :],
        "bd512": _block_mean(A_WIDTH), "bd256": _block_mean(M_WIDTH),
        "pfq": pfq, "pfk": pfk, "cq": cq, "ck": ck,
        "conv_w": conv_w[l], "conv_b": conv_b[l][None, :],
        "w_out": w_out[l].astype(BF16), "g_ffn": g_ffn[l][None, :], "w_pq": w_peer_q[l].astype(BF16), "keys": keys2,
    }


def _peer_and_merge(x, oa, ob, om, wts, u_packed, v_packed, route_tile, peer_tile):
    b, s, d = x.shape
    n = b * s
    x1, xn, e, gates = _route_call(x.reshape(n, d), oa.reshape(n, -1), ob.reshape(n, -1), om.reshape(n, -1), wts, route_tile)
    consts = _peer_constants()
    w = _peer_u_call(e, xn.reshape(n * SUBLANES, LANES), gates, u_packed, consts, peer_tile)
    y8 = _peer_v_call(e, w, x1.reshape(n * SUBLANES, LANES), v_packed, consts, peer_tile)
    return y8.reshape(b, s, d)


def kernel(x_prompt, x_sample, cache_a_k, cache_a_v, cache_a_logf, cache_b_conv, cache_m_k, cache_m_v, mem_prompt, g_attn, w_in, b_f, g_q_a, g_k_a, conv_w, conv_b, g_mem, w_mem_k, w_mem_v, g_k_m, g_q_m, w_out, g_ffn, w_peer_q, peer_keys, peer_u, peer_v):
    depth = w_in.shape[0]
    xp, xs = x_prompt, x_sample
    bp, sp, _ = xp.shape
    bs, ts, _ = xs.shape
    outs = [[] for _ in range(10)]
    for l in range(depth):
        wts = _layer_weights(l, g_attn, w_in, b_f, g_q_a, g_k_a, conv_w, conv_b, g_q_m, w_out, g_ffn, w_peer_q, peer_keys)
        u_packed, v_packed = _pack_table(peer_u[l]), _pack_table(peer_v[l])
        fox_blk = min(512, sp)
        route_tile = 256
        peer_tile = 128

        mk, mv = _memkv_call(mem_prompt, g_mem[l][None, :], w_mem_k[l].astype(BF16), w_mem_v[l].astype(BF16),
                             jnp.tile(g_k_m[l], M_HEADS)[None, :], wts["bd256"])
        ka, va, logf, qaug, kaug, vb, ob, om, cst = _proj_call(
            xp, jnp.zeros((bp, 2, B_WIDTH), F32), mk, mv, wts, min(512, sp))
        oa = _fox_prompt_call(qaug, kaug, vb, min(2 * fox_blk, sp), fox_blk)
        xp = _peer_and_merge(xp, oa, ob, om, wts, u_packed, v_packed, route_tile, peer_tile)
        n_mem = mk.shape[1]
        for dst, val in zip(outs[:6], [ka.reshape(bp, sp, A_HEADS, HEAD_DIM), va.reshape(bp, sp, A_HEADS, HEAD_DIM), logf, cst,
                                       mk.reshape(bp, n_mem, M_HEADS, HEAD_DIM), mv.reshape(bp, n_mem, M_HEADS, HEAD_DIM)]):
            dst.append(val)

        past = cache_a_k.shape[2]
        ka, va, logf, qaug, kaug, vb, ob, om, cst = _proj_call(
            xs, cache_b_conv[l], cache_m_k[l].reshape(bs, -1, M_WIDTH), cache_m_v[l].reshape(bs, -1, M_WIDTH), wts, ts)
        clf = jnp.pad(cache_a_logf[l], ((0, 0), (0, 0), (0, LANES - A_HEADS)))
        oa = _fox_sample_call(cache_a_k[l].reshape(bs, past, A_WIDTH), cache_a_v[l].reshape(bs, past, A_WIDTH), clf,
                              qaug, kaug, vb, wts["pfk"], wts["ck"])
        xs = _peer_and_merge(xs, oa, ob, om, wts, u_packed, v_packed, route_tile, peer_tile)
        for dst, val in zip(outs[6:], [ka.reshape(bs, ts, A_HEADS, HEAD_DIM), va.reshape(bs, ts, A_HEADS, HEAD_DIM), logf, cst]):
            dst.append(val)

    return (xp, xs) + tuple(jnp.stack(o) for o in outs)
```

```python
import functools

import numpy as np
import jax
import jax.numpy as jnp
from jax import lax
from jax.experimental import pallas as pl
from jax.experimental.pallas import tpu as pltpu

F32, BF16, I32 = jnp.float32, jnp.bfloat16, jnp.int32
EPS = 1e-6
LANES = 128
SUBLANES = 8
HEAD_DIM = 64
A_HEADS = 8
A_WIDTH = A_HEADS * HEAD_DIM
B_WIDTH = 256
M_HEADS = 4
M_WIDTH = M_HEADS * HEAD_DIM
PEER_HEADS = 8
PEER_KEYS = 128
PEER_TOPK = 16
PEER_SLOTS = PEER_HEADS * PEER_TOPK
D_MODEL = 1024
ROW_WORDS = D_MODEL // 2
ROW_SUB = ROW_WORDS // LANES
NEG_BIG = -1e30
VMEM_LIMIT = 56 * 1024 * 1024

C_Q, C_K, C_V, C_BG, C_CG, C_HV, C_QM, C_FL, C_END = 0, 512, 1024, 1536, 1792, 2048, 2304, 2560, 2688


def _dot(a, b):
    return jnp.dot(a, b, preferred_element_type=F32)


def _dot_nt(a, b):
    return lax.dot_general(a, b, (((1,), (1,)), ((), ())), preferred_element_type=F32)


def _split3(x):
    hi = x.astype(BF16)
    r1 = x - hi.astype(F32)
    mid = r1.astype(BF16)
    lo = (r1 - mid.astype(F32)).astype(BF16)
    return hi, mid, lo


def _lane(shape):
    return lax.broadcasted_iota(I32, shape, len(shape) - 1)


def _memkv_body(mem_ref, g_ref, wk_ref, wv_ref, gk_ref, bd_ref, mk_ref, mv_ref):
    x = mem_ref[0]
    h = (x * lax.rsqrt(jnp.mean(x * x, axis=-1, keepdims=True) + EPS) * g_ref[...]).astype(BF16)
    zk = _dot(h, wk_ref[...])
    ms = _dot((zk * zk).astype(BF16), bd_ref[...])
    mk_ref[0] = zk * lax.rsqrt(ms + EPS) * gk_ref[...]
    mv_ref[0] = _dot(h, wv_ref[...])


def _memkv_call(mem, g_mem, w_mk, w_mv, gk_t, bd256):
    b, n_mem, d = mem.shape
    full = lambda shape: pl.BlockSpec(shape, lambda i: (0,) * len(shape))
    return pl.pallas_call(
        _memkv_body,
        grid=(b,),
        in_specs=[pl.BlockSpec((1, n_mem, d), lambda i: (i, 0, 0)), full((1, d)), full((d, M_WIDTH)),
                  full((d, M_WIDTH)), full((1, M_WIDTH)), full((M_WIDTH, M_WIDTH))],
        out_specs=[pl.BlockSpec((1, n_mem, M_WIDTH), lambda i: (i, 0, 0))] * 2,
        out_shape=[jax.ShapeDtypeStruct((b, n_mem, M_WIDTH), F32)] * 2,
        compiler_params=pltpu.CompilerParams(dimension_semantics=("arbitrary",), vmem_limit_bytes=VMEM_LIMIT),
        name="mem_kv",
    )(mem, g_mem, w_mk, w_mv, gk_t, bd256)


def _proj_body(x_ref, prev_ref, mk_ref, mv_ref, gat_ref, w_ref, bf_ref, gq_ref, gk_ref, gqm_ref,
               bd512_ref, bd256_ref, pfq_ref, pfk_ref, cq_ref, ck_ref, cw_ref, cb_ref,
               ka_ref, va_ref, logf_ref, qaug_ref, kaug_ref, vb_ref, ob_ref, om_ref, cst_ref,
               fcarry, ucarry):
    t = pl.program_id(1)
    tt = x_ref.shape[1]
    x = x_ref[0]
    h = (x * lax.rsqrt(jnp.mean(x * x, axis=-1, keepdims=True) + EPS) * gat_ref[...]).astype(BF16)

    def proj(c0, c1):
        return _dot(h, w_ref[:, c0:c1])

    @pl.when(t == 0)
    def _():
        fcarry[...] = jnp.zeros_like(fcarry)
        ucarry[...] = jnp.zeros_like(ucarry)
        ucarry[SUBLANES - 2:SUBLANES, :] = prev_ref[0]

    v = proj(C_FL, C_END) + bf_ref[...]
    logf = jnp.minimum(v, 0.0) - jnp.log1p(jnp.exp(-jnp.abs(v)))
    logf = jnp.where(_lane(logf.shape) < A_HEADS, logf, 0.0)
    logf_ref[0] = logf[:, :A_HEADS]
    row = lax.broadcasted_iota(I32, (tt, tt), 0)
    col = lax.broadcasted_iota(I32, (tt, tt), 1)
    tri = jnp.where(row >= col, 1.0, 0.0).astype(BF16)
    lh, lm, ll = _split3(logf)
    fcum = _dot(tri, lh) + _dot(tri, lm) + _dot(tri, ll) + fcarry[0:1, :]
    fcarry[...] = jnp.broadcast_to(fcum[tt - 1:tt, :], fcarry.shape)
    fparts = jnp.concatenate(_split3(fcum), axis=1)
    faq = _dot(fparts, pfq_ref[...]) + cq_ref[...]
    fak = _dot(fparts, pfk_ref[...]) + ck_ref[...]

    lane = _lane((tt, LANES))
    zq = proj(C_Q, C_K)
    qn = zq * lax.rsqrt(_dot((zq * zq).astype(BF16), bd512_ref[...]) + EPS) * gq_ref[...]
    zk = proj(C_K, C_V)
    kn = zk * lax.rsqrt(_dot((zk * zk).astype(BF16), bd512_ref[...]) + EPS) * gk_ref[...]
    ka_ref[0] = kn
    for hd in range(A_HEADS):
        c0 = LANES * (hd // 2)
        qt, kt = qn[:, c0:c0 + LANES], kn[:, c0:c0 + LANES]
        if hd % 2:
            qt, kt = pltpu.roll(qt, HEAD_DIM, 1), pltpu.roll(kt, HEAD_DIM, 1)
        qaug_ref[0, hd] = jnp.where(lane < HEAD_DIM, qt, faq[:, LANES * hd:LANES * (hd + 1)]).astype(BF16)
        kaug_ref[0, hd] = jnp.where(lane < HEAD_DIM, kt, fak[:, LANES * hd:LANES * (hd + 1)]).astype(BF16)
    zv = proj(C_V, C_BG)
    va_ref[0] = zv
    vb_ref[0] = zv.astype(BF16)

    u = proj(C_CG, C_HV) * proj(C_HV, C_QM)
    rows = lax.broadcasted_iota(I32, u.shape, 0)
    p1 = ucarry[SUBLANES - 1:SUBLANES, :]
    p2 = ucarry[SUBLANES - 2:SUBLANES - 1, :]
    u1 = jnp.where(rows == 0, p1, pltpu.roll(u, 1, 0))
    u2 = jnp.where(rows == 0, p2, jnp.where(rows == 1, p1, pltpu.roll(u, 2, 0)))
    cy = cb_ref[...] + cw_ref[2:3, :] * u + cw_ref[0:1, :] * u2 + cw_ref[1:2, :] * u1
    ob_ref[0] = (proj(C_BG, C_CG) * cy).astype(BF16)
    ucarry[...] = u[tt - SUBLANES:tt, :]
    cst_ref[0] = u[tt - 2:tt, :]

    zm = proj(C_QM, C_FL)
    qm = zm * lax.rsqrt(_dot((zm * zm).astype(BF16), bd256_ref[...]) + EPS) * gqm_ref[...]
    mkb = mk_ref[0].astype(BF16)
    mvb = mv_ref[0].astype(BF16)
    outs = []
    for pr in range(M_HEADS // 2):
        qp = qm[:, LANES * pr:LANES * (pr + 1)]
        kp = mkb[:, LANES * pr:LANES * (pr + 1)]
        vp = mvb[:, LANES * pr:LANES * (pr + 1)]
        o = []
        for sub in range(2):
            keep = (lane < HEAD_DIM) if sub == 0 else (lane >= HEAD_DIM)
            s = _dot_nt(jnp.where(keep, qp, 0.0).astype(BF16), kp)
            p = jnp.exp(s - jnp.max(s, axis=-1, keepdims=True))
            o.append(_dot(p.astype(BF16), vp) / jnp.sum(p, axis=-1, keepdims=True))
        outs.append(jnp.where(lane < HEAD_DIM, o[0], o[1]))
    om_ref[0] = jnp.concatenate(outs, axis=1).astype(BF16)


def _proj_call(x, prev, mk, mv, wts, tile):
    b, s, d = x.shape
    nt = s // tile
    n_mem = mk.shape[1]
    full = lambda a: pl.BlockSpec(a.shape, lambda i, j: (0,) * a.ndim)
    seq = lambda w: pl.BlockSpec((1, tile, w), lambda i, j: (i, j, 0))
    per_b = lambda r, w: pl.BlockSpec((1, r, w), lambda i, j: (i, 0, 0))
    heads = pl.BlockSpec((1, A_HEADS, tile, LANES), lambda i, j: (i, 0, j, 0))
    names = ["g_attn", "w_in", "b_f", "g_q", "g_k", "g_qm", "bd512", "bd256", "pfq", "pfk", "cq", "ck", "conv_w", "conv_b"]
    consts = [wts[k] for k in names]
    out_shape = [
        jax.ShapeDtypeStruct((b, s, A_WIDTH), F32),
        jax.ShapeDtypeStruct((b, s, A_WIDTH), F32),
        jax.ShapeDtypeStruct((b, s, A_HEADS), F32),
        jax.ShapeDtypeStruct((b, A_HEADS, s, LANES), BF16),
        jax.ShapeDtypeStruct((b, A_HEADS, s, LANES), BF16),
        jax.ShapeDtypeStruct((b, s, A_WIDTH), BF16),
        jax.ShapeDtypeStruct((b, s, B_WIDTH), BF16),
        jax.ShapeDtypeStruct((b, s, M_WIDTH), BF16),
        jax.ShapeDtypeStruct((b, 2, B_WIDTH), F32),
    ]
    out_specs = [seq(A_WIDTH), seq(A_WIDTH), seq(A_HEADS), heads, heads, seq(A_WIDTH), seq(B_WIDTH), seq(M_WIDTH),
                 per_b(2, B_WIDTH)]
    return pl.pallas_call(
        _proj_body,
        grid=(b, nt),
        in_specs=[seq(d), per_b(2, B_WIDTH), per_b(n_mem, M_WIDTH), per_b(n_mem, M_WIDTH)] + [full(c) for c in consts],
        out_specs=out_specs,
        out_shape=out_shape,
        scratch_shapes=[pltpu.VMEM((SUBLANES, LANES), F32), pltpu.VMEM((SUBLANES, B_WIDTH), F32)],
        compiler_params=pltpu.CompilerParams(dimension_semantics=("arbitrary", "arbitrary"), vmem_limit_bytes=VMEM_LIMIT),
        name="proj",
    )(x, prev, mk, mv, *consts)


def _fox_prompt_body(q_ref, k_ref, v_ref, o_ref, *, bq, bk):
    qi = pl.program_id(2)
    ratio = bq // bk
    ones = jnp.ones((bk, LANES), BF16)
    row = lax.broadcasted_iota(I32, (bq, bk), 0)
    col = lax.broadcasted_iota(I32, (bq, bk), 1)

    def step(kj, carry, masked):
        off = pl.multiple_of(kj * bk, bk)
        vv = jnp.concatenate([v_ref[0, pl.ds(off, bk), :], ones], axis=1)
        new = []
        for sub in range(2):
            m, acc = carry[sub]
            s = _dot_nt(q_ref[0, sub], k_ref[0, sub, pl.ds(off, bk), :])
            if masked:
                s = jnp.where(col + (kj * bk - qi * bq) <= row, s, NEG_BIG)
            m_new = jnp.maximum(m, jnp.max(s, axis=-1, keepdims=True))
            p = jnp.exp(s - m_new).astype(BF16)
            new.append((m_new, jnp.exp(m - m_new) * acc + _dot(p, vv)))
        return tuple(new)

    init = ((jnp.full((bq, 1), NEG_BIG, F32), jnp.zeros((bq, 2 * LANES), F32)),) * 2
    carry = lax.fori_loop(0, qi * ratio, functools.partial(step, masked=False), init)
    for d in range(ratio):
        carry = step(qi * ratio + d, carry, True)
    outs = [acc[:, :LANES] / acc[:, LANES:] for _, acc in carry]
    o_ref[0] = jnp.where(_lane((bq, LANES)) < HEAD_DIM, outs[0], outs[1]).astype(BF16)


def _fox_prompt_call(qaug, kaug, vb, bq, bk):
    b, _, s, _ = qaug.shape
    return pl.pallas_call(
        functools.partial(_fox_prompt_body, bq=bq, bk=bk),
        grid=(b, A_HEADS // 2, s // bq),
        in_specs=[pl.BlockSpec((1, 2, bq, LANES), lambda i, hp, j: (i, hp, j, 0)),
                  pl.BlockSpec((1, 2, s, LANES), lambda i, hp, j: (i, hp, 0, 0)),
                  pl.BlockSpec((1, s, LANES), lambda i, hp, j: (i, 0, hp))],
        out_specs=pl.BlockSpec((1, bq, LANES), lambda i, hp, j: (i, j, hp)),
        out_shape=jax.ShapeDtypeStruct((b, s, A_WIDTH), BF16),
        compiler_params=pltpu.CompilerParams(dimension_semantics=("arbitrary",) * 3, vmem_limit_bytes=VMEM_LIMIT),
        name="fox_prompt",
    )(qaug, kaug, vb)


def _fox_sample_body(ck_ref, cv_ref, clf_ref, q_ref, k_ref, v_ref, pfk_ref, ckc_ref, o_ref):
    past = ck_ref.shape[1]
    ts = q_ref.shape[2]
    row = lax.broadcasted_iota(I32, (past, past), 0)
    col = lax.broadcasted_iota(I32, (past, past), 1)
    tri = jnp.where(col > row, 1.0, 0.0).astype(BF16)
    lh, lm, ll = _split3(clf_ref[0])
    suffix = _dot(tri, lh) + _dot(tri, lm) + _dot(tri, ll)
    fak = _dot(jnp.concatenate(_split3(-suffix), axis=1), pfk_ref[...]) + ckc_ref[...]
    lane = _lane((past, LANES))
    lane_s = _lane((ts, LANES))
    causal = lax.broadcasted_iota(I32, (ts, ts), 1) <= lax.broadcasted_iota(I32, (ts, ts), 0)
    outs = []
    for pr in range(A_HEADS // 2):
        kc2 = ck_ref[0, :, LANES * pr:LANES * (pr + 1)]
        vc = cv_ref[0, :, LANES * pr:LANES * (pr + 1)].astype(BF16)
        vn = v_ref[0, :, LANES * pr:LANES * (pr + 1)]
        o = []
        for sub in range(2):
            hd = 2 * pr + sub
            kt = pltpu.roll(kc2, HEAD_DIM, 1) if sub else kc2
            kc = jnp.where(lane < HEAD_DIM, kt, fak[:, LANES * hd:LANES * (hd + 1)]).astype(BF16)
            q = q_ref[0, hd]
            s1 = _dot_nt(q, kc)
            s2 = jnp.where(causal, _dot_nt(q, k_ref[0, hd]), NEG_BIG)
            m = jnp.maximum(jnp.max(s1, axis=-1, keepdims=True), jnp.max(s2, axis=-1, keepdims=True))
            p1, p2 = jnp.exp(s1 - m), jnp.exp(s2 - m)
            den = jnp.sum(p1, axis=-1, keepdims=True) + jnp.sum(p2, axis=-1, keepdims=True)
            o.append((_dot(p1.astype(BF16), vc) + _dot(p2.astype(BF16), vn)) / den)
        outs.append(jnp.where(lane_s < HEAD_DIM, o[0], o[1]))
    o_ref[0] = jnp.concatenate(outs, axis=1).astype(BF16)


def _fox_sample_call(cache_k, cache_v, cache_lf, qaug, kaug, vb, pfk, ck):
    b, past, _ = cache_k.shape
    ts = qaug.shape[2]
    full = lambda a: pl.BlockSpec(a.shape, lambda i: (0,) * a.ndim)
    return pl.pallas_call(
        _fox_sample_body,
        grid=(b,),
        in_specs=[pl.BlockSpec((1, past, A_WIDTH), lambda i: (i, 0, 0)),
                  pl.BlockSpec((1, past, A_WIDTH), lambda i: (i, 0, 0)),
                  pl.BlockSpec((1, past, LANES), lambda i: (i, 0, 0)),
                  pl.BlockSpec((1, A_HEADS, ts, LANES), lambda i: (i, 0, 0, 0)),
                  pl.BlockSpec((1, A_HEADS, ts, LANES), lambda i: (i, 0, 0, 0)),
                  pl.BlockSpec((1, ts, A_WIDTH), lambda i: (i, 0, 0)),
                  full(pfk), full(ck)],
        out_specs=pl.BlockSpec((1, ts, A_WIDTH), lambda i: (i, 0, 0)),
        out_shape=jax.ShapeDtypeStruct((b, ts, A_WIDTH), BF16),
        compiler_params=pltpu.CompilerParams(dimension_semantics=("arbitrary",), vmem_limit_bytes=VMEM_LIMIT),
        name="fox_sample",
    )(cache_k, cache_v, cache_lf, qaug, kaug, vb, pfk, ck)


def _all_sublanes(x, op):
    for shift in (4, 2, 1):
        x = op(x, pltpu.roll(x, shift, 0))
    return x


def _top_groups(groups, payload, k, sub):
    groups = list(groups)
    big = SUBLANES * len(groups)
    out = []
    for _ in range(k):
        level = [(v, g) for g, v in enumerate(groups)]
        while len(level) > 1:
            merged = []
            for (va, ga), (vb, gb) in zip(level[0::2], level[1::2]):
                take = vb > va
                merged.append((jnp.where(take, vb, va), jnp.where(take, gb, ga)))
            level = merged + ([level[-1]] if len(level) % 2 else [])
        best, where_g = level[0]
        top = _all_sublanes(best, jnp.maximum)
        row = _all_sublanes(jnp.where(best == top, where_g * SUBLANES + sub, big), jnp.minimum)
        rel = row - sub
        hits = [rel == SUBLANES * g for g in range(len(groups))]
        if payload is None:
            out.append((top, row))
        else:
            pay = jnp.where(hits[0], payload[0], -1)
            for g in range(1, len(groups)):
                pay = jnp.where(hits[g], payload[g], pay)
            out.append((top, _all_sublanes(pay, jnp.maximum)))
        groups = [jnp.where(h, -jnp.inf, v) for h, v in zip(hits, groups)]
    return out


def _pack_rows(rows, sub2):
    acc = jnp.zeros(sub2.shape, rows[0].dtype)
    for j, r in enumerate(rows):
        acc = jnp.where(sub2 == j, jnp.concatenate([r, r], axis=0), acc)
    return acc


def _route_head(sc, sub, sub2):
    groups = lambda a: [a[SUBLANES * g:SUBLANES * (g + 1)] for g in range(a.shape[0] // SUBLANES)]
    t1 = _top_groups(groups(sc[:PEER_KEYS]), None, PEER_TOPK, sub)
    t2 = _top_groups(groups(sc[PEER_KEYS:]), None, PEER_TOPK, sub)
    s1, i1 = [v for v, _ in t1], [i * PEER_KEYS for _, i in t1]
    s2p, i2p = _pack_rows([v for v, _ in t2], sub2), _pack_rows([i for _, i in t2], sub2)
    s1p, i1p = _pack_rows(s1, sub2), _pack_rows(i1, sub2)
    lo, hi = slice(0, SUBLANES), slice(SUBLANES, 2 * SUBLANES)
    half = sub < 4
    rep4 = lambda a: jnp.where(half, a[lo], pltpu.roll(a[lo], 4, 0))
    pick = lambda a, b: jnp.where(half, a, b)

    def pairs(first, first_packed, second_packed, second0):
        return [first[0] + second_packed[lo], first[0] + second_packed[hi], first[1] + second_packed[lo],
                first[2] + second_packed[lo], first[3] + second_packed[lo],
                pick(first[4], first[5]) + rep4(second_packed), pick(first[6], first[7]) + rep4(second_packed),
                first_packed[hi] + second0]

    cand = pairs(s1, s1p, s2p, t2[0][0])
    expert = pairs(i1, i1p, i2p, t2[0][1])
    best = _top_groups(cand, expert, PEER_TOPK, sub)
    top = _pack_rows([v for v, _ in best], sub2)
    p = jnp.exp(top - best[0][0][0:1, :])
    gates = p / jnp.sum(p, axis=0, keepdims=True)
    return _pack_rows([e for _, e in best], sub2) * ROW_SUB, gates


def _route_body(x_ref, oa_ref, ob_ref, om_ref, wo_ref, gf_ref, wq_ref, keys_ref,
                x1_ref, xn_ref, e_ref, g_ref, qp_scr, e_scr, g_scr):
    tile, d_model = x_ref.shape
    lane_tiles = tile // LANES
    y = x_ref[...] + _dot(oa_ref[...], wo_ref[0:A_WIDTH, :]) \
        + _dot(ob_ref[...], wo_ref[A_WIDTH:A_WIDTH + B_WIDTH, :]) \
        + _dot(om_ref[...], wo_ref[A_WIDTH + B_WIDTH:, :])
    xn = y * lax.rsqrt(jnp.mean(y * y, axis=-1, keepdims=True) + EPS) * gf_ref[...]
    for c in range(d_model // LANES):
        rows = pl.ds(c, tile, stride=SUBLANES)
        x1_ref[rows, :] = y[:, LANES * c:LANES * (c + 1)]
        xn_ref[rows, :] = xn[:, LANES * c:LANES * (c + 1)]
    qp = _dot(xn.astype(BF16), wq_ref[...])
    for hd in range(PEER_HEADS):
        qp_scr[hd] = qp[:, LANES * hd:LANES * (hd + 1)].astype(BF16)
    sub = lax.broadcasted_iota(I32, (SUBLANES, LANES), 0)
    sub2 = lax.broadcasted_iota(I32, (2 * SUBLANES, LANES), 0)

    def head_pair(hp, carry):
        for hd in (2 * hp, 2 * hp + 1):
            for lt in range(lane_tiles):
                q = qp_scr[hd, LANES * lt:LANES * (lt + 1), :]
                e, gates = _route_head(_dot_nt(keys_ref[hd], q), sub, sub2)
                e_scr[lt, hd] = e
                g_scr[lt, hd] = gates
        return carry

    lax.fori_loop(0, PEER_HEADS // 2, head_pair, 0)
    for lt in range(lane_tiles):
        rows = slice(LANES * lt, LANES * (lt + 1))
        e_ref[rows, :] = jnp.concatenate([e_scr[lt, hd] for hd in range(PEER_HEADS)], axis=0).T
        g_ref[rows, :] = jnp.concatenate([g_scr[lt, hd] for hd in range(PEER_HEADS)], axis=0).T


def _route_call(x, oa, ob, om, wts, tile):
    n, d = x.shape
    full = lambda a: pl.BlockSpec(a.shape, lambda i: (0,) * a.ndim)
    rows = lambda w: pl.BlockSpec((tile, w), lambda i: (i, 0))
    consts = [wts[k] for k in ["w_out", "g_ffn", "w_pq", "keys"]]
    return pl.pallas_call(
        _route_body,
        grid=(n // tile,),
        in_specs=[rows(d), rows(A_WIDTH), rows(B_WIDTH), rows(M_WIDTH)] + [full(c) for c in consts],
        out_specs=[pl.BlockSpec((tile * SUBLANES, LANES), lambda i: (i, 0))] * 2 + [rows(PEER_SLOTS), rows(PEER_SLOTS)],
        out_shape=[jax.ShapeDtypeStruct((n * d // LANES, LANES), F32)] * 2 + [
                   jax.ShapeDtypeStruct((n, PEER_SLOTS), I32), jax.ShapeDtypeStruct((n, PEER_SLOTS), F32)],
        scratch_shapes=[pltpu.VMEM((PEER_HEADS, tile, LANES), BF16),
                        pltpu.VMEM((tile // LANES, PEER_HEADS, PEER_TOPK, LANES), I32),
                        pltpu.VMEM((tile // LANES, PEER_HEADS, PEER_TOPK, LANES), F32)],
        compiler_params=pltpu.CompilerParams(dimension_semantics=("arbitrary",), vmem_limit_bytes=VMEM_LIMIT),
        name="merge_route",
    )(x, oa, ob, om, *consts)


GROUP_ROWS = PEER_SLOTS * ROW_SUB
WIDE = PEER_SLOTS * SUBLANES
PIPE_TOKENS = 64
PACK_ROWS = 512


def _gather_token(idx_ref, row, tab_ref, buf):
    for k in range(PEER_SLOTS):
        buf[k * ROW_SUB:(k + 1) * ROW_SUB, :] = tab_ref[pl.ds(pl.multiple_of(idx_ref[row, k], ROW_SUB), ROW_SUB), :]


def _pipelined_tokens(e_hbm, tab_ref, bufs, idx, sems, compute, tile):
    blocks = tile // PIPE_TOKENS
    first_block = pl.program_id(0) * blocks
    all_blocks = pl.num_programs(0) * blocks

    def idx_copy(block, slot):
        rows = pl.ds(pl.multiple_of(block * PIPE_TOKENS, PIPE_TOKENS), PIPE_TOKENS)
        return pltpu.make_async_copy(e_hbm.at[rows, :], idx[slot], sems.at[slot])

    @pl.when(pl.program_id(0) == 0)
    def _():
        idx_copy(0, 0).start()
        idx_copy(1, 1).start()
        idx_copy(0, 0).wait()
        _gather_token(idx[0], 0, tab_ref, bufs[0])
        _gather_token(idx[0], 1, tab_ref, bufs[1])

    def block_pair(p, carry):
        for slot in range(2):
            local = 2 * p + slot
            block = first_block + local
            for half in range(PIPE_TOKENS // 2):
                cur = bufs[2 * (half % 2):2 * (half % 2) + 2]
                nxt = bufs[2 - 2 * (half % 2):4 - 2 * (half % 2)]
                for j in range(2):
                    compute(local * PIPE_TOKENS + 2 * half + j, cur[j])
                if half + 1 < PIPE_TOKENS // 2:
                    for j in range(2):
                        _gather_token(idx[slot], 2 * half + 2 + j, tab_ref, nxt[j])
                else:
                    @pl.when(block + 1 < all_blocks)
                    def _():
                        idx_copy(block + 1, 1 - slot).wait()
                        for j in range(2):
                            _gather_token(idx[1 - slot], j, tab_ref, nxt[j])

            @pl.when(block + 2 < all_blocks)
            def _():
                idx_copy(block + 2, slot).start()
        return carry

    lax.fori_loop(0, blocks // 2, block_pair, 0)


def _pipeline_scratch():
    return ([pltpu.VMEM((GROUP_ROWS, LANES), I32)] * 4 + [pltpu.SMEM((PIPE_TOKENS, PEER_SLOTS), I32)] * 2
            + [pltpu.SemaphoreType.DMA((2,))])


def _token_tile_rows(t):
    return pl.ds(pl.multiple_of(t * SUBLANES, SUBLANES), SUBLANES)


def _split2(x):
    hi = x.astype(BF16).astype(F32)
    return jnp.concatenate([hi, x - hi], axis=0).astype(BF16)


def _peer_u_body(e_hbm, x_ref, g_ref, u_ref, col_ref, mask_ref, w_ref, b0, b1, b2, b3, i0, i1, sems, z_scr, *, tile, chunk):
    mask = mask_ref[...]

    def compute(t, buf):
        z = _dot_nt(_split2(x_ref[_token_tile_rows(t), :]), pltpu.bitcast(buf[...], BF16))
        z_scr[_token_tile_rows(t), :] = (z[:SUBLANES] + z[SUBLANES:]) * mask

    _pipelined_tokens(e_hbm, u_ref, (b0, b1, b2, b3), (i0, i1), sems, compute, tile)
    for c in range(tile // chunk):
        zs = z_scr[c * chunk * SUBLANES:(c + 1) * chunk * SUBLANES, :]
        zh = zs.astype(BF16)
        zl = (zs - zh.astype(F32)).astype(BF16)
        part = _dot(zh, col_ref[...]) + _dot(zl, col_ref[...])
        a = jnp.sum(part.reshape(chunk, SUBLANES, PEER_SLOTS), axis=1)
        rows = slice(c * chunk, (c + 1) * chunk)
        w_ref[rows, :] = g_ref[rows, :] * (0.5 * a * (1.0 + lax.erf(a * np.float32(np.sqrt(0.5)))))


def _peer_u_call(e, xn8, gates, u_packed, consts, tile):
    n = e.shape[0]
    full = lambda a: pl.BlockSpec(a.shape, lambda i: (0,) * a.ndim)
    return pl.pallas_call(
        functools.partial(_peer_u_body, tile=tile, chunk=min(tile, 32)),
        grid=(n // tile,),
        in_specs=[pl.BlockSpec(memory_space=pl.ANY),
                  pl.BlockSpec((tile * SUBLANES, LANES), lambda i: (i, 0)),
                  pl.BlockSpec((tile, PEER_SLOTS), lambda i: (i, 0)),
                  pl.BlockSpec(memory_space=pltpu.VMEM),
                  full(consts["collapse"]), full(consts["mask8"])],
        out_specs=pl.BlockSpec((tile, PEER_SLOTS), lambda i: (i, 0)),
        out_shape=jax.ShapeDtypeStruct((n, PEER_SLOTS), F32),
        scratch_shapes=_pipeline_scratch() + [pltpu.VMEM((tile * SUBLANES, WIDE), F32)],
        compiler_params=pltpu.CompilerParams(dimension_semantics=("arbitrary",), vmem_limit_bytes=VMEM_LIMIT),
        name="peer_u",
    )(e, xn8, gates, u_packed, consts["collapse"], consts["mask8"])


def _peer_v_body(e_hbm, w_ref, x_ref, v_ref, exp_ref, mask_ref, y_ref, b0, b1, b2, b3, i0, i1, sems, eh_scr, el_scr,
                 y_scr, *, tile):
    mask = mask_ref[...]
    w = w_ref[...]
    wh = w.astype(BF16)
    eh_scr[...] = _dot(wh, exp_ref[...])
    el_scr[...] = _dot((w - wh.astype(F32)).astype(BF16), exp_ref[...])

    def compute(t, buf):
        row = pl.ds(t, 1)
        lhs = jnp.concatenate([jnp.broadcast_to(eh_scr[row, :], (SUBLANES, WIDE)) * mask,
                               jnp.broadcast_to(el_scr[row, :], (SUBLANES, WIDE)) * mask], axis=0).astype(BF16)
        o = _dot(lhs, pltpu.bitcast(buf[...], BF16))
        rows = _token_tile_rows(t)
        y_scr[rows, :] = x_ref[rows, :] + o[:SUBLANES] + o[SUBLANES:]

    _pipelined_tokens(e_hbm, v_ref, (b0, b1, b2, b3), (i0, i1), sems, compute, tile)
    for c in range(y_ref.shape[1] // LANES):
        y_ref[:, LANES * c:LANES * (c + 1)] = y_scr[pl.ds(c, tile, stride=SUBLANES), :]


def _peer_v_call(e, w, x8, v_packed, consts, tile):
    n = e.shape[0]
    full = lambda a: pl.BlockSpec(a.shape, lambda i: (0,) * a.ndim)
    return pl.pallas_call(
        functools.partial(_peer_v_body, tile=tile),
        grid=(n // tile,),
        in_specs=[pl.BlockSpec(memory_space=pl.ANY),
                  pl.BlockSpec((tile, PEER_SLOTS), lambda i: (i, 0)),
                  pl.BlockSpec((tile * SUBLANES, LANES), lambda i: (i, 0)),
                  pl.BlockSpec(memory_space=pltpu.VMEM),
                  full(consts["expand"]), full(consts["mask8"])],
        out_specs=pl.BlockSpec((tile, D_MODEL), lambda i: (i, 0)),
        out_shape=jax.ShapeDtypeStruct((n, D_MODEL), F32),
        scratch_shapes=_pipeline_scratch() + [pltpu.VMEM((tile, WIDE), F32), pltpu.VMEM((tile, WIDE), F32),
                                              pltpu.VMEM((tile * SUBLANES, LANES), F32)],
        compiler_params=pltpu.CompilerParams(dimension_semantics=("arbitrary",), vmem_limit_bytes=VMEM_LIMIT),
        name="peer_v",
    )(e, w, x8, v_packed, consts["expand"], consts["mask8"])


def _peer_constants():
    lane = np.arange(WIDE)
    expand = (lane[None, :] // SUBLANES == np.arange(PEER_SLOTS)[:, None]).astype(np.float32)
    mask8 = (lane[None, :] % SUBLANES == np.arange(SUBLANES)[:, None]).astype(np.float32)
    return {"expand": jnp.asarray(expand, BF16), "collapse": jnp.asarray(expand.T, BF16), "mask8": jnp.asarray(mask8)}


def _pack_table(tab):
    n, d = tab.shape
    rows = PACK_ROWS

    def body(t_ref, o_ref):
        for j in range(ROW_SUB):
            lo = pltpu.bitcast(t_ref[:, 2 * j * LANES:(2 * j + 1) * LANES].astype(BF16).astype(F32), I32)
            hi = pltpu.bitcast(t_ref[:, (2 * j + 1) * LANES:(2 * j + 2) * LANES].astype(BF16).astype(F32), I32)
            o_ref[pl.ds(j, rows, stride=ROW_SUB), :] = hi | lax.shift_right_logical(lo, jnp.int32(16))

    return pl.pallas_call(
        body,
        grid=(n // rows,),
        in_specs=[pl.BlockSpec((rows, d), lambda i: (i, 0))],
        out_specs=pl.BlockSpec((rows * ROW_SUB, LANES), lambda i: (i, 0)),
        out_shape=jax.ShapeDtypeStruct((n * ROW_SUB, LANES), I32),
        compiler_params=pltpu.CompilerParams(dimension_semantics=("arbitrary",), vmem_limit_bytes=VMEM_LIMIT),
        name="pack_table",
    )(tab)


def _block_mean(width):
    blk = np.arange(width) // HEAD_DIM
    return jnp.asarray((blk[:, None] == blk[None, :]) / HEAD_DIM, BF16)


def _forget_placement():
    pfq = np.zeros((3 * LANES, A_HEADS * LANES), np.float32)
    pfk = np.zeros((3 * LANES, A_HEADS * LANES), np.float32)
    cq = np.zeros((1, A_HEADS * LANES), np.float32)
    ck = np.zeros((1, A_HEADS * LANES), np.float32)
    for hd in range(A_HEADS):
        base = hd * LANES + HEAD_DIM
        for piece in range(3):
            pfq[piece * LANES + hd, base + piece] = 1.0
            pfk[piece * LANES + hd, base + 3 + piece] = -1.0
            cq[0, base + 3 + piece] = 1.0
            ck[0, base + piece] = 1.0
    return jnp.asarray(pfq, BF16), jnp.asarray(pfk, BF16), jnp.asarray(cq), jnp.asarray(ck)


def _layer_weights(l, g_attn, w_in, b_f, g_q_a, g_k_a, conv_w, conv_b, g_q_m, w_out, g_ffn, w_peer_q, peer_keys):
    scale = HEAD_DIM ** -0.5
    splits = np.cumsum([A_WIDTH, A_WIDTH, A_WIDTH, A_HEADS, B_WIDTH, B_WIDTH, B_WIDTH, M_WIDTH])
    wi = w_in[l]
    qa, ka, va, fl, bg, cg, hv, qm = [wi[:, a:b] for a, b in zip(np.r_[0, splits[:-1]], splits)]
    w_packed = jnp.concatenate([qa, ka, va, bg, cg, hv, qm, fl, jnp.zeros((wi.shape[0], LANES - A_HEADS), wi.dtype)],
                               axis=1).astype(BF16)
    pfq, pfk, cq, ck = _forget_placement()
    keys = peer_keys[l]
    zeros = jnp.zeros_like(keys[:, 0])
    keys2 = jnp.concatenate([jnp.concatenate([keys[:, 0], zeros], axis=-1),
                             jnp.concatenate([zeros, keys[:, 1]], axis=-1)], axis=1).astype(BF16)
    return {
        "g_attn": g_attn[l][None, :], "w_in": w_packed,
        "b_f": jnp.pad(b_f[l], (0, LANES - A_HEADS))[None, :],
        "g_q": (jnp.tile(g_q_a[l], A_HEADS) * scale)[None, :], "g_k": jnp.tile(g_k_a[l], A_HEADS)[None, :],
        "g_qm": (jnp.tile(g_q_m[l], M_HEADS) * scale)[None, :],
        "bd512": _block_mean(A_WIDTH), "bd256": _block_mean(M_WIDTH),
        "pfq": pfq, "pfk": pfk, "cq": cq, "ck": ck,
        "conv_w": conv_w[l], "conv_b": conv_b[l][None, :],
        "w_out": w_out[l].astype(BF16), "g_ffn": g_ffn[l][None, :], "w_pq": w_peer_q[l].astype(BF16), "keys": keys2,
    }


def _peer_and_merge(x, oa, ob, om, wts, u_packed, v_packed, route_tile, peer_tile):
    b, s, d = x.shape
    n = b * s
    x1, xn, e, gates = _route_call(x.reshape(n, d), oa.reshape(n, -1), ob.reshape(n, -1), om.reshape(n, -1), wts, route_tile)
    consts = _peer_constants()
    w = _peer_u_call(e, xn.reshape(n * SUBLANES, LANES), gates, u_packed, consts, peer_tile)
    y8 = _peer_v_call(e, w, x1.reshape(n * SUBLANES, LANES), v_packed, consts, peer_tile)
    return y8.reshape(b, s, d)


def kernel(x_prompt, x_sample, cache_a_k, cache_a_v, cache_a_logf, cache_b_conv, cache_m_k, cache_m_v, mem_prompt, g_attn, w_in, b_f, g_q_a, g_k_a, conv_w, conv_b, g_mem, w_mem_k, w_mem_v, g_k_m, g_q_m, w_out, g_ffn, w_peer_q, peer_keys, peer_u, peer_v):
    depth = w_in.shape[0]
    xp, xs = x_prompt, x_sample
    bp, sp, _ = xp.shape
    bs, ts, _ = xs.shape
    outs = [[] for _ in range(10)]
    for l in range(depth):
        wts = _layer_weights(l, g_attn, w_in, b_f, g_q_a, g_k_a, conv_w, conv_b, g_q_m, w_out, g_ffn, w_peer_q, peer_keys)
        u_packed, v_packed = _pack_table(peer_u[l]), _pack_table(peer_v[l])
        fox_blk = min(512, sp)
        route_tile = 256
        peer_tile = 128

        mk, mv = _memkv_call(mem_prompt, g_mem[l][None, :], w_mem_k[l].astype(BF16), w_mem_v[l].astype(BF16),
                             jnp.tile(g_k_m[l], M_HEADS)[None, :], wts["bd256"])
        ka, va, logf, qaug, kaug, vb, ob, om, cst = _proj_call(
            xp, jnp.zeros((bp, 2, B_WIDTH), F32), mk, mv, wts, min(512, sp))
        oa = _fox_prompt_call(qaug, kaug, vb, min(2 * fox_blk, sp), fox_blk)
        xp = _peer_and_merge(xp, oa, ob, om, wts, u_packed, v_packed, route_tile, peer_tile)
        n_mem = mk.shape[1]
        for dst, val in zip(outs[:6], [ka.reshape(bp, sp, A_HEADS, HEAD_DIM), va.reshape(bp, sp, A_HEADS, HEAD_DIM), logf, cst,
                                       mk.reshape(bp, n_mem, M_HEADS, HEAD_DIM), mv.reshape(bp, n_mem, M_HEADS, HEAD_DIM)]):
            dst.append(val)

        past = cache_a_k.shape[2]
        ka, va, logf, qaug, kaug, vb, ob, om, cst = _proj_call(
            xs, cache_b_conv[l], cache_m_k[l].reshape(bs, -1, M_WIDTH), cache_m_v[l].reshape(bs, -1, M_WIDTH), wts, ts)
        clf = jnp.pad(cache_a_logf[l], ((0, 0), (0, 0), (0, LANES - A_HEADS)))
        oa = _fox_sample_call(cache_a_k[l].reshape(bs, past, A_WIDTH), cache_a_v[l].reshape(bs, past, A_WIDTH), clf,
                              qaug, kaug, vb, wts["pfk"], wts["ck"])
        xs = _peer_and_merge(xs, oa, ob, om, wts, u_packed, v_packed, route_tile, peer_tile)
        for dst, val in zip(outs[6:], [ka.reshape(bs, ts, A_HEADS, HEAD_DIM), va.reshape(bs, ts, A_HEADS, HEAD_DIM), logf, cst]):
            dst.append(val)

    return (xp, xs) + tuple(jnp.stack(o) for o in outs)
```

```python
import functools

import numpy as np
import jax
import jax.numpy as jnp
from jax import lax
from jax.experimental import pallas as pl
from jax.experimental.pallas import tpu as pltpu

F32, BF16, I32 = jnp.float32, jnp.bfloat16, jnp.int32
EPS = 1e-6
LANES = 128
SUBLANES = 8
HEAD_DIM = 64
A_HEADS = 8
A_WIDTH = A_HEADS * HEAD_DIM
B_WIDTH = 256
M_HEADS = 4
M_WIDTH = M_HEADS * HEAD_DIM
PEER_HEADS = 8
PEER_KEYS = 128
PEER_TOPK = 16
PEER_SLOTS = PEER_HEADS * PEER_TOPK
D_MODEL = 1024
ROW_WORDS = D_MODEL // 2
ROW_SUB = ROW_WORDS // LANES
NEG_BIG = -1e30
VMEM_LIMIT = 56 * 1024 * 1024

C_Q, C_K, C_V, C_BG, C_CG, C_HV, C_QM, C_FL, C_END = 0, 512, 1024, 1536, 1792, 2048, 2304, 2560, 2688


def _dot(a, b):
    return jnp.dot(a, b, preferred_element_type=F32)


def _dot_nt(a, b):
    return lax.dot_general(a, b, (((1,), (1,)), ((), ())), preferred_element_type=F32)


def _split3(x):
    hi = x.astype(BF16)
    r1 = x - hi.astype(F32)
    mid = r1.astype(BF16)
    lo = (r1 - mid.astype(F32)).astype(BF16)
    return hi, mid, lo


def _lane(shape):
    return lax.broadcasted_iota(I32, shape, len(shape) - 1)


def _memkv_body(mem_ref, g_ref, wk_ref, wv_ref, gk_ref, bd_ref, mk_ref, mv_ref):
    x = mem_ref[0]
    h = (x * lax.rsqrt(jnp.mean(x * x, axis=-1, keepdims=True) + EPS) * g_ref[...]).astype(BF16)
    zk = _dot(h, wk_ref[...])
    ms = _dot((zk * zk).astype(BF16), bd_ref[...])
    mk_ref[0] = zk * lax.rsqrt(ms + EPS) * gk_ref[...]
    mv_ref[0] = _dot(h, wv_ref[...])


def _memkv_call(mem, g_mem, w_mk, w_mv, gk_t, bd256):
    b, n_mem, d = mem.shape
    full = lambda shape: pl.BlockSpec(shape, lambda i: (0,) * len(shape))
    return pl.pallas_call(
        _memkv_body,
        grid=(b,),
        in_specs=[pl.BlockSpec((1, n_mem, d), lambda i: (i, 0, 0)), full((1, d)), full((d, M_WIDTH)),
                  full((d, M_WIDTH)), full((1, M_WIDTH)), full((M_WIDTH, M_WIDTH))],
        out_specs=[pl.BlockSpec((1, n_mem, M_WIDTH), lambda i: (i, 0, 0))] * 2,
        out_shape=[jax.ShapeDtypeStruct((b, n_mem, M_WIDTH), F32)] * 2,
        compiler_params=pltpu.CompilerParams(dimension_semantics=("arbitrary",), vmem_limit_bytes=VMEM_LIMIT),
        name="mem_kv",
    )(mem, g_mem, w_mk, w_mv, gk_t, bd256)


def _proj_body(x_ref, prev_ref, mk_ref, mv_ref, gat_ref, w_ref, bf_ref, gq_ref, gk_ref, gqm_ref,
               bd512_ref, bd256_ref, pfq_ref, pfk_ref, cq_ref, ck_ref, cw_ref, cb_ref,
               ka_ref, va_ref, logf_ref, qaug_ref, kaug_ref, vb_ref, ob_ref, om_ref, cst_ref,
               fcarry, ucarry):
    t = pl.program_id(1)
    tt = x_ref.shape[1]
    x = x_ref[0]
    h = (x * lax.rsqrt(jnp.mean(x * x, axis=-1, keepdims=True) + EPS) * gat_ref[...]).astype(BF16)

    def proj(c0, c1):
        return _dot(h, w_ref[:, c0:c1])

    @pl.when(t == 0)
    def _():
        fcarry[...] = jnp.zeros_like(fcarry)
        ucarry[...] = jnp.zeros_like(ucarry)
        ucarry[SUBLANES - 2:SUBLANES, :] = prev_ref[0]

    v = proj(C_FL, C_END) + bf_ref[...]
    logf = jnp.minimum(v, 0.0) - jnp.log1p(jnp.exp(-jnp.abs(v)))
    logf = jnp.where(_lane(logf.shape) < A_HEADS, logf, 0.0)
    logf_ref[0] = logf[:, :A_HEADS]
    row = lax.broadcasted_iota(I32, (tt, tt), 0)
    col = lax.broadcasted_iota(I32, (tt, tt), 1)
    tri = jnp.where(row >= col, 1.0, 0.0).astype(BF16)
    lh, lm, ll = _split3(logf)
    fcum = _dot(tri, lh) + _dot(tri, lm) + _dot(tri, ll) + fcarry[0:1, :]
    fcarry[...] = jnp.broadcast_to(fcum[tt - 1:tt, :], fcarry.shape)
    fparts = jnp.concatenate(_split3(fcum), axis=1)
    faq = _dot(fparts, pfq_ref[...]) + cq_ref[...]
    fak = _dot(fparts, pfk_ref[...]) + ck_ref[...]

    lane = _lane((tt, LANES))
    zq = proj(C_Q, C_K)
    qn = zq * lax.rsqrt(_dot((zq * zq).astype(BF16), bd512_ref[...]) + EPS) * gq_ref[...]
    zk = proj(C_K, C_V)
    kn = zk * lax.rsqrt(_dot((zk * zk).astype(BF16), bd512_ref[...]) + EPS) * gk_ref[...]
    ka_ref[0] = kn
    for hd in range(A_HEADS):
        c0 = LANES * (hd // 2)
        qt, kt = qn[:, c0:c0 + LANES], kn[:, c0:c0 + LANES]
        if hd % 2:
            qt, kt = pltpu.roll(qt, HEAD_DIM, 1), pltpu.roll(kt, HEAD_DIM, 1)
        qaug_ref[0, hd] = jnp.where(lane < HEAD_DIM, qt, faq[:, LANES * hd:LANES * (hd + 1)]).astype(BF16)
        kaug_ref[0, hd] = jnp.where(lane < HEAD_DIM, kt, fak[:, LANES * hd:LANES * (hd + 1)]).astype(BF16)
    zv = proj(C_V, C_BG)
    va_ref[0] = zv
    vb_ref[0] = zv.astype(BF16)

    u = proj(C_CG, C_HV) * proj(C_HV, C_QM)
    rows = lax.broadcasted_iota(I32, u.shape, 0)
    p1 = ucarry[SUBLANES - 1:SUBLANES, :]
    p2 = ucarry[SUBLANES - 2:SUBLANES - 1, :]
    u1 = jnp.where(rows == 0, p1, pltpu.roll(u, 1, 0))
    u2 = jnp.where(rows == 0, p2, jnp.where(rows == 1, p1, pltpu.roll(u, 2, 0)))
    cy = cb_ref[...] + cw_ref[2:3, :] * u + cw_ref[0:1, :] * u2 + cw_ref[1:2, :] * u1
    ob_ref[0] = (proj(C_BG, C_CG) * cy).astype(BF16)
    ucarry[...] = u[tt - SUBLANES:tt, :]
    cst_ref[0] = u[tt - 2:tt, :]

    zm = proj(C_QM, C_FL)
    qm = zm * lax.rsqrt(_dot((zm * zm).astype(BF16), bd256_ref[...]) + EPS) * gqm_ref[...]
    mkb = mk_ref[0].astype(BF16)
    mvb = mv_ref[0].astype(BF16)
    outs = []
    for pr in range(M_HEADS // 2):
        qp = qm[:, LANES * pr:LANES * (pr + 1)]
        kp = mkb[:, LANES * pr:LANES * (pr + 1)]
        vp = mvb[:, LANES * pr:LANES * (pr + 1)]
        o = []
        for sub in range(2):
            keep = (lane < HEAD_DIM) if sub == 0 else (lane >= HEAD_DIM)
            s = _dot_nt(jnp.where(keep, qp, 0.0).astype(BF16), kp)
            p = jnp.exp(s - jnp.max(s, axis=-1, keepdims=True))
            o.append(_dot(p.astype(BF16), vp) / jnp.sum(p, axis=-1, keepdims=True))
        outs.append(jnp.where(lane < HEAD_DIM, o[0], o[1]))
    om_ref[0] = jnp.concatenate(outs, axis=1).astype(BF16)


def _proj_call(x, prev, mk, mv, wts, tile):
    b, s, d = x.shape
    nt = s // tile
    n_mem = mk.shape[1]
    full = lambda a: pl.BlockSpec(a.shape, lambda i, j: (0,) * a.ndim)
    seq = lambda w: pl.BlockSpec((1, tile, w), lambda i, j: (i, j, 0))
    per_b = lambda r, w: pl.BlockSpec((1, r, w), lambda i, j: (i, 0, 0))
    heads = pl.BlockSpec((1, A_HEADS, tile, LANES), lambda i, j: (i, 0, j, 0))
    names = ["g_attn", "w_in", "b_f", "g_q", "g_k", "g_qm", "bd512", "bd256", "pfq", "pfk", "cq", "ck", "conv_w", "conv_b"]
    consts = [wts[k] for k in names]
    out_shape = [
        jax.ShapeDtypeStruct((b, s, A_WIDTH), F32),
        jax.ShapeDtypeStruct((b, s, A_WIDTH), F32),
        jax.ShapeDtypeStruct((b, s, A_HEADS), F32),
        jax.ShapeDtypeStruct((b, A_HEADS, s, LANES), BF16),
        jax.ShapeDtypeStruct((b, A_HEADS, s, LANES), BF16),
        jax.ShapeDtypeStruct((b, s, A_WIDTH), BF16),
        jax.ShapeDtypeStruct((b, s, B_WIDTH), BF16),
        jax.ShapeDtypeStruct((b, s, M_WIDTH), BF16),
        jax.ShapeDtypeStruct((b, 2, B_WIDTH), F32),
    ]
    out_specs = [seq(A_WIDTH), seq(A_WIDTH), seq(A_HEADS), heads, heads, seq(A_WIDTH), seq(B_WIDTH), seq(M_WIDTH),
                 per_b(2, B_WIDTH)]
    return pl.pallas_call(
        _proj_body,
        grid=(b, nt),
        in_specs=[seq(d), per_b(2, B_WIDTH), per_b(n_mem, M_WIDTH), per_b(n_mem, M_WIDTH)] + [full(c) for c in consts],
        out_specs=out_specs,
        out_shape=out_shape,
        scratch_shapes=[pltpu.VMEM((SUBLANES, LANES), F32), pltpu.VMEM((SUBLANES, B_WIDTH), F32)],
        compiler_params=pltpu.CompilerParams(dimension_semantics=("arbitrary", "arbitrary"), vmem_limit_bytes=VMEM_LIMIT),
        name="proj",
    )(x, prev, mk, mv, *consts)


def _fox_prompt_body(q_ref, k_ref, v_ref, o_ref, *, bq, bk):
    qi = pl.program_id(2)
    ratio = bq // bk
    ones = jnp.ones((bk, LANES), BF16)
    row = lax.broadcasted_iota(I32, (bq, bk), 0)
    col = lax.broadcasted_iota(I32, (bq, bk), 1)

    def step(kj, carry, masked):
        off = pl.multiple_of(kj * bk, bk)
        vv = jnp.concatenate([v_ref[0, pl.ds(off, bk), :], ones], axis=1)
        new = []
        for sub in range(2):
            m, acc = carry[sub]
            s = _dot_nt(q_ref[0, sub], k_ref[0, sub, pl.ds(off, bk), :])
            if masked:
                s = jnp.where(col + (kj * bk - qi * bq) <= row, s, NEG_BIG)
            m_new = jnp.maximum(m, jnp.max(s, axis=-1, keepdims=True))
            p = jnp.exp(s - m_new).astype(BF16)
            new.append((m_new, jnp.exp(m - m_new) * acc + _dot(p, vv)))
        return tuple(new)

    init = ((jnp.full((bq, 1), NEG_BIG, F32), jnp.zeros((bq, 2 * LANES), F32)),) * 2
    carry = lax.fori_loop(0, qi * ratio, functools.partial(step, masked=False), init)
    for d in range(ratio):
        carry = step(qi * ratio + d, carry, True)
    outs = [acc[:, :LANES] / acc[:, LANES:] for _, acc in carry]
    o_ref[0] = jnp.where(_lane((bq, LANES)) < HEAD_DIM, outs[0], outs[1]).astype(BF16)


def _fox_prompt_call(qaug, kaug, vb, bq, bk):
    b, _, s, _ = qaug.shape
    return pl.pallas_call(
        functools.partial(_fox_prompt_body, bq=bq, bk=bk),
        grid=(b, A_HEADS // 2, s // bq),
        in_specs=[pl.BlockSpec((1, 2, bq, LANES), lambda i, hp, j: (i, hp, j, 0)),
                  pl.BlockSpec((1, 2, s, LANES), lambda i, hp, j: (i, hp, 0, 0)),
                  pl.BlockSpec((1, s, LANES), lambda i, hp, j: (i, 0, hp))],
        out_specs=pl.BlockSpec((1, bq, LANES), lambda i, hp, j: (i, j, hp)),
        out_shape=jax.ShapeDtypeStruct((b, s, A_WIDTH), BF16),
        compiler_params=pltpu.CompilerParams(dimension_semantics=("arbitrary",) * 3, vmem_limit_bytes=VMEM_LIMIT),
        name="fox_prompt",
    )(qaug, kaug, vb)


def _fox_sample_body(ck_ref, cv_ref, clf_ref, q_ref, k_ref, v_ref, pfk_ref, ckc_ref, o_ref):
    past = ck_ref.shape[1]
    ts = q_ref.shape[2]
    row = lax.broadcasted_iota(I32, (past, past), 0)
    col = lax.broadcasted_iota(I32, (past, past), 1)
    tri = jnp.where(col > row, 1.0, 0.0).astype(BF16)
    lh, lm, ll = _split3(clf_ref[0])
    suffix = _dot(tri, lh) + _dot(tri, lm) + _dot(tri, ll)
    fak = _dot(jnp.concatenate(_split3(-suffix), axis=1), pfk_ref[...]) + ckc_ref[...]
    lane = _lane((past, LANES))
    lane_s = _lane((ts, LANES))
    causal = lax.broadcasted_iota(I32, (ts, ts), 1) <= lax.broadcasted_iota(I32, (ts, ts), 0)
    outs = []
    for pr in range(A_HEADS // 2):
        kc2 = ck_ref[0, :, LANES * pr:LANES * (pr + 1)]
        vc = cv_ref[0, :, LANES * pr:LANES * (pr + 1)].astype(BF16)
        vn = v_ref[0, :, LANES * pr:LANES * (pr + 1)]
        o = []
        for sub in range(2):
            hd = 2 * pr + sub
            kt = pltpu.roll(kc2, HEAD_DIM, 1) if sub else kc2
            kc = jnp.where(lane < HEAD_DIM, kt, fak[:, LANES * hd:LANES * (hd + 1)]).astype(BF16)
            q = q_ref[0, hd]
            s1 = _dot_nt(q, kc)
            s2 = jnp.where(causal, _dot_nt(q, k_ref[0, hd]), NEG_BIG)
            m = jnp.maximum(jnp.max(s1, axis=-1, keepdims=True), jnp.max(s2, axis=-1, keepdims=True))
            p1, p2 = jnp.exp(s1 - m), jnp.exp(s2 - m)
            den = jnp.sum(p1, axis=-1, keepdims=True) + jnp.sum(p2, axis=-1, keepdims=True)
            o.append((_dot(p1.astype(BF16), vc) + _dot(p2.astype(BF16), vn)) / den)
        outs.append(jnp.where(lane_s < HEAD_DIM, o[0], o[1]))
    o_ref[0] = jnp.concatenate(outs, axis=1).astype(BF16)


def _fox_sample_call(cache_k, cache_v, cache_lf, qaug, kaug, vb, pfk, ck):
    b, past, _ = cache_k.shape
    ts = qaug.shape[2]
    full = lambda a: pl.BlockSpec(a.shape, lambda i: (0,) * a.ndim)
    return pl.pallas_call(
        _fox_sample_body,
        grid=(b,),
        in_specs=[pl.BlockSpec((1, past, A_WIDTH), lambda i: (i, 0, 0)),
                  pl.BlockSpec((1, past, A_WIDTH), lambda i: (i, 0, 0)),
                  pl.BlockSpec((1, past, LANES), lambda i: (i, 0, 0)),
                  pl.BlockSpec((1, A_HEADS, ts, LANES), lambda i: (i, 0, 0, 0)),
                  pl.BlockSpec((1, A_HEADS, ts, LANES), lambda i: (i, 0, 0, 0)),
                  pl.BlockSpec((1, ts, A_WIDTH), lambda i: (i, 0, 0)),
                  full(pfk), full(ck)],
        out_specs=pl.BlockSpec((1, ts, A_WIDTH), lambda i: (i, 0, 0)),
        out_shape=jax.ShapeDtypeStruct((b, ts, A_WIDTH), BF16),
        compiler_params=pltpu.CompilerParams(dimension_semantics=("arbitrary",), vmem_limit_bytes=VMEM_LIMIT),
        name="fox_sample",
    )(cache_k, cache_v, cache_lf, qaug, kaug, vb, pfk, ck)


def _all_sublanes(x, op):
    for shift in (4, 2, 1):
        x = op(x, pltpu.roll(x, shift, 0))
    return x


def _top_groups(groups, payload, k, sub):
    groups = list(groups)
    big = SUBLANES * len(groups)
    out = []
    for _ in range(k):
        level = [(v, g) for g, v in enumerate(groups)]
        while len(level) > 1:
            merged = []
            for (va, ga), (vb, gb) in zip(level[0::2], level[1::2]):
                take = vb > va
                merged.append((jnp.where(take, vb, va), jnp.where(take, gb, ga)))
            level = merged + ([level[-1]] if len(level) % 2 else [])
        best, where_g = level[0]
        top = _all_sublanes(best, jnp.maximum)
        row = _all_sublanes(jnp.where(best == top, where_g * SUBLANES + sub, big), jnp.minimum)
        rel = row - sub
        hits = [rel == SUBLANES * g for g in range(len(groups))]
        if payload is None:
            out.append((top, row))
        else:
            pay = jnp.where(hits[0], payload[0], -1)
            for g in range(1, len(groups)):
                pay = jnp.where(hits[g], payload[g], pay)
            out.append((top, _all_sublanes(pay, jnp.maximum)))
        groups = [jnp.where(h, -jnp.inf, v) for h, v in zip(hits, groups)]
    return out


def _pack_rows(rows, sub2):
    acc = jnp.zeros(sub2.shape, rows[0].dtype)
    for j, r in enumerate(rows):
        acc = jnp.where(sub2 == j, jnp.concatenate([r, r], axis=0), acc)
    return acc


def _route_head(sc, sub, sub2):
    groups = lambda a: [a[SUBLANES * g:SUBLANES * (g + 1)] for g in range(a.shape[0] // SUBLANES)]
    t1 = _top_groups(groups(sc[:PEER_KEYS]), None, PEER_TOPK, sub)
    t2 = _top_groups(groups(sc[PEER_KEYS:]), None, PEER_TOPK, sub)
    s1, i1 = [v for v, _ in t1], [i * PEER_KEYS for _, i in t1]
    s2p, i2p = _pack_rows([v for v, _ in t2], sub2), _pack_rows([i for _, i in t2], sub2)
    s1p, i1p = _pack_rows(s1, sub2), _pack_rows(i1, sub2)
    lo, hi = slice(0, SUBLANES), slice(SUBLANES, 2 * SUBLANES)
    half = sub < 4
    rep4 = lambda a: jnp.where(half, a[lo], pltpu.roll(a[lo], 4, 0))
    pick = lambda a, b: jnp.where(half, a, b)

    def pairs(first, first_packed, second_packed, second0):
        return [first[0] + second_packed[lo], first[0] + second_packed[hi], first[1] + second_packed[lo],
                first[2] + second_packed[lo], first[3] + second_packed[lo],
                pick(first[4], first[5]) + rep4(second_packed), pick(first[6], first[7]) + rep4(second_packed),
                first_packed[hi] + second0]

    cand = pairs(s1, s1p, s2p, t2[0][0])
    expert = pairs(i1, i1p, i2p, t2[0][1])
    best = _top_groups(cand, expert, PEER_TOPK, sub)
    top = _pack_rows([v for v, _ in best], sub2)
    p = jnp.exp(top - best[0][0][0:1, :])
    gates = p / jnp.sum(p, axis=0, keepdims=True)
    return _pack_rows([e for _, e in best], sub2) * ROW_SUB, gates


def _route_body(x_ref, oa_ref, ob_ref, om_ref, wo_ref, gf_ref, wq_ref, keys_ref,
                x1_ref, xn_ref, e_ref, g_ref, qp_scr, e_scr, g_scr):
    tile, d_model = x_ref.shape
    lane_tiles = tile // LANES
    y = x_ref[...] + _dot(oa_ref[...], wo_ref[0:A_WIDTH, :]) \
        + _dot(ob_ref[...], wo_ref[A_WIDTH:A_WIDTH + B_WIDTH, :]) \
        + _dot(om_ref[...], wo_ref[A_WIDTH + B_WIDTH:, :])
    xn = y * lax.rsqrt(jnp.mean(y * y, axis=-1, keepdims=True) + EPS) * gf_ref[...]
    for c in range(d_model // LANES):
        rows = pl.ds(c, tile, stride=SUBLANES)
        x1_ref[rows, :] = y[:, LANES * c:LANES * (c + 1)]
        xn_ref[rows, :] = xn[:, LANES * c:LANES * (c + 1)]
    qp = _dot(xn.astype(BF16), wq_ref[...])
    for hd in range(PEER_HEADS):
        qp_scr[hd] = qp[:, LANES * hd:LANES * (hd + 1)].astype(BF16)
    sub = lax.broadcasted_iota(I32, (SUBLANES, LANES), 0)
    sub2 = lax.broadcasted_iota(I32, (2 * SUBLANES, LANES), 0)

    def head_pair(hp, carry):
        for hd in (2 * hp, 2 * hp + 1):
            for lt in range(lane_tiles):
                q = qp_scr[hd, LANES * lt:LANES * (lt + 1), :]
                e, gates = _route_head(_dot_nt(keys_ref[hd], q), sub, sub2)
                e_scr[lt, hd] = e
                g_scr[lt, hd] = gates
        return carry

    lax.fori_loop(0, PEER_HEADS // 2, head_pair, 0)
    for lt in range(lane_tiles):
        rows = slice(LANES * lt, LANES * (lt + 1))
        e_ref[rows, :] = jnp.concatenate([e_scr[lt, hd] for hd in range(PEER_HEADS)], axis=0).T
        g_ref[rows, :] = jnp.concatenate([g_scr[lt, hd] for hd in range(PEER_HEADS)], axis=0).T


def _route_call(x, oa, ob, om, wts, tile):
    n, d = x.shape
    full = lambda a: pl.BlockSpec(a.shape, lambda i: (0,) * a.ndim)
    rows = lambda w: pl.BlockSpec((tile, w), lambda i: (i, 0))
    consts = [wts[k] for k in ["w_out", "g_ffn", "w_pq", "keys"]]
    return pl.pallas_call(
        _route_body,
        grid=(n // tile,),
        in_specs=[rows(d), rows(A_WIDTH), rows(B_WIDTH), rows(M_WIDTH)] + [full(c) for c in consts],
        out_specs=[pl.BlockSpec((tile * SUBLANES, LANES), lambda i: (i, 0))] * 2 + [rows(PEER_SLOTS), rows(PEER_SLOTS)],
        out_shape=[jax.ShapeDtypeStruct((n * d // LANES, LANES), F32)] * 2 + [
                   jax.ShapeDtypeStruct((n, PEER_SLOTS), I32), jax.ShapeDtypeStruct((n, PEER_SLOTS), F32)],
        scratch_shapes=[pltpu.VMEM((PEER_HEADS, tile, LANES), BF16),
                        pltpu.VMEM((tile // LANES, PEER_HEADS, PEER_TOPK, LANES), I32),
                        pltpu.VMEM((tile // LANES, PEER_HEADS, PEER_TOPK, LANES), F32)],
        compiler_params=pltpu.CompilerParams(dimension_semantics=("arbitrary",), vmem_limit_bytes=VMEM_LIMIT),
        name="merge_route",
    )(x, oa, ob, om, *consts)


GROUP_ROWS = PEER_SLOTS * ROW_SUB
WIDE = PEER_SLOTS * SUBLANES
PIPE_TOKENS = 64
PACK_ROWS = 512


def _gather_token(idx_ref, row, tab_ref, buf):
    for k in range(PEER_SLOTS):
        buf[k * ROW_SUB:(k + 1) * ROW_SUB, :] = tab_ref[pl.ds(pl.multiple_of(idx_ref[row, k], ROW_SUB), ROW_SUB), :]


def _pipelined_tokens(e_hbm, tab_ref, bufs, idx, sems, compute, tile):
    blocks = tile // PIPE_TOKENS
    first_block = pl.program_id(0) * blocks
    all_blocks = pl.num_programs(0) * blocks

    def idx_copy(block, slot):
        rows = pl.ds(pl.multiple_of(block * PIPE_TOKENS, PIPE_TOKENS), PIPE_TOKENS)
        return pltpu.make_async_copy(e_hbm.at[rows, :], idx[slot], sems.at[slot])

    @pl.when(pl.program_id(0) == 0)
    def _():
        idx_copy(0, 0).start()
        idx_copy(1, 1).start()
        idx_copy(0, 0).wait()
        _gather_token(idx[0], 0, tab_ref, bufs[0])
        _gather_token(idx[0], 1, tab_ref, bufs[1])

    def block_pair(p, carry):
        for slot in range(2):
            local = 2 * p + slot
            block = first_block + local
            for half in range(PIPE_TOKENS // 2):
                cur = bufs[2 * (half % 2):2 * (half % 2) + 2]
                nxt = bufs[2 - 2 * (half % 2):4 - 2 * (half % 2)]
                for j in range(2):
                    compute(local * PIPE_TOKENS + 2 * half + j, cur[j])
                if half + 1 < PIPE_TOKENS // 2:
                    for j in range(2):
                        _gather_token(idx[slot], 2 * half + 2 + j, tab_ref, nxt[j])
                else:
                    @pl.when(block + 1 < all_blocks)
                    def _():
                        idx_copy(block + 1, 1 - slot).wait()
                        for j in range(2):
                            _gather_token(idx[1 - slot], j, tab_ref, nxt[j])

            @pl.when(block + 2 < all_blocks)
            def _():
                idx_copy(block + 2, slot).start()
        return carry

    lax.fori_loop(0, blocks // 2, block_pair, 0)


def _pipeline_scratch():
    return ([pltpu.VMEM((GROUP_ROWS, LANES), I32)] * 4 + [pltpu.SMEM((PIPE_TOKENS, PEER_SLOTS), I32)] * 2
            + [pltpu.SemaphoreType.DMA((2,))])


def _token_tile_rows(t):
    return pl.ds(pl.multiple_of(t * SUBLANES, SUBLANES), SUBLANES)


def _split2(x):
    hi = x.astype(BF16).astype(F32)
    return jnp.concatenate([hi, x - hi], axis=0).astype(BF16)


def _peer_u_body(e_hbm, x_ref, g_ref, u_ref, col_ref, mask_ref, w_ref, b0, b1, b2, b3, i0, i1, sems, z_scr, *, tile, chunk):
    mask = mask_ref[...]

    def compute(t, buf):
        z = _dot_nt(_split2(x_ref[_token_tile_rows(t), :]), pltpu.bitcast(buf[...], BF16))
        z_scr[_token_tile_rows(t), :] = (z[:SUBLANES] + z[SUBLANES:]) * mask

    _pipelined_tokens(e_hbm, u_ref, (b0, b1, b2, b3), (i0, i1), sems, compute, tile)
    for c in range(tile // chunk):
        zs = z_scr[c * chunk * SUBLANES:(c + 1) * chunk * SUBLANES, :]
        zh = zs.astype(BF16)
        zl = (zs - zh.astype(F32)).astype(BF16)
        part = _dot(zh, col_ref[...]) + _dot(zl, col_ref[...])
        a = jnp.sum(part.reshape(chunk, SUBLANES, PEER_SLOTS), axis=1)
        rows = slice(c * chunk, (c + 1) * chunk)
        w_ref[rows, :] = g_ref[rows, :] * (0.5 * a * (1.0 + lax.erf(a * np.float32(np.sqrt(0.5)))))


def _peer_u_call(e, xn8, gates, u_packed, consts, tile):
    n = e.shape[0]
    full = lambda a: pl.BlockSpec(a.shape, lambda i: (0,) * a.ndim)
    return pl.pallas_call(
        functools.partial(_peer_u_body, tile=tile, chunk=min(tile, 32)),
        grid=(n // tile,),
        in_specs=[pl.BlockSpec(memory_space=pl.ANY),
                  pl.BlockSpec((tile * SUBLANES, LANES), lambda i: (i, 0)),
                  pl.BlockSpec((tile, PEER_SLOTS), lambda i: (i, 0)),
                  pl.BlockSpec(memory_space=pltpu.VMEM),
                  full(consts["collapse"]), full(consts["mask8"])],
        out_specs=pl.BlockSpec((tile, PEER_SLOTS), lambda i: (i, 0)),
        out_shape=jax.ShapeDtypeStruct((n, PEER_SLOTS), F32),
        scratch_shapes=_pipeline_scratch() + [pltpu.VMEM((tile * SUBLANES, WIDE), F32)],
        compiler_params=pltpu.CompilerParams(dimension_semantics=("arbitrary",), vmem_limit_bytes=VMEM_LIMIT),
        name="peer_u",
    )(e, xn8, gates, u_packed, consts["collapse"], consts["mask8"])


def _peer_v_body(e_hbm, w_ref, x_ref, v_ref, exp_ref, mask_ref, y_ref, b0, b1, b2, b3, i0, i1, sems, eh_scr, el_scr,
                 y_scr, *, tile):
    mask = mask_ref[...]
    w = w_ref[...]
    wh = w.astype(BF16)
    eh_scr[...] = _dot(wh, exp_ref[...])
    el_scr[...] = _dot((w - wh.astype(F32)).astype(BF16), exp_ref[...])

    def compute(t, buf):
        row = pl.ds(t, 1)
        lhs = jnp.concatenate([jnp.broadcast_to(eh_scr[row, :], (SUBLANES, WIDE)) * mask,
                               jnp.broadcast_to(el_scr[row, :], (SUBLANES, WIDE)) * mask], axis=0).astype(BF16)
        o = _dot(lhs, pltpu.bitcast(buf[...], BF16))
        rows = _token_tile_rows(t)
        y_scr[rows, :] = x_ref[rows, :] + o[:SUBLANES] + o[SUBLANES:]

    _pipelined_tokens(e_hbm, v_ref, (b0, b1, b2, b3), (i0, i1), sems, compute, tile)
    for c in range(y_ref.shape[1] // LANES):
        y_ref[:, LANES * c:LANES * (c + 1)] = y_scr[pl.ds(c, tile, stride=SUBLANES), :]


def _peer_v_call(e, w, x8, v_packed, consts, tile):
    n = e.shape[0]
    full = lambda a: pl.BlockSpec(a.shape, lambda i: (0,) * a.ndim)
    return pl.pallas_call(
        functools.partial(_peer_v_body, tile=tile),
        grid=(n // tile,),
        in_specs=[pl.BlockSpec(memory_space=pl.ANY),
                  pl.BlockSpec((tile, PEER_SLOTS), lambda i: (i, 0)),
                  pl.BlockSpec((tile * SUBLANES, LANES), lambda i: (i, 0)),
                  pl.BlockSpec(memory_space=pltpu.VMEM),
                  full(consts["expand"]), full(consts["mask8"])],
        out_specs=pl.BlockSpec((tile, D_MODEL), lambda i: (i, 0)),
        out_shape=jax.ShapeDtypeStruct((n, D_MODEL), F32),
        scratch_shapes=_pipeline_scratch() + [pltpu.VMEM((tile, WIDE), F32), pltpu.VMEM((tile, WIDE), F32),
                                              pltpu.VMEM((tile * SUBLANES, LANES), F32)],
        compiler_params=pltpu.CompilerParams(dimension_semantics=("arbitrary",), vmem_limit_bytes=VMEM_LIMIT),
        name="peer_v",
    )(e, w, x8, v_packed, consts["expand"], consts["mask8"])


def _peer_constants():
    lane = np.arange(WIDE)
    expand = (lane[None, :] // SUBLANES == np.arange(PEER_SLOTS)[:, None]).astype(np.float32)
    mask8 = (lane[None, :] % SUBLANES == np.arange(SUBLANES)[:, None]).astype(np.float32)
    return {"expand": jnp.asarray(expand, BF16), "collapse": jnp.asarray(expand.T, BF16), "mask8": jnp.asarray(mask8)}


def _pack_table(tab):
    n, d = tab.shape
    rows = PACK_ROWS

    def body(t_ref, o_ref):
        for j in range(ROW_SUB):
            lo = pltpu.bitcast(t_ref[:, 2 * j * LANES:(2 * j + 1) * LANES].astype(BF16).astype(F32), I32)
            hi = pltpu.bitcast(t_ref[:, (2 * j + 1) * LANES:(2 * j + 2) * LANES].astype(BF16).astype(F32), I32)
            o_ref[pl.ds(j, rows, stride=ROW_SUB), :] = hi | lax.shift_right_logical(lo, jnp.int32(16))

    return pl.pallas_call(
        body,
        grid=(n // rows,),
        in_specs=[pl.BlockSpec((rows, d), lambda i: (i, 0))],
        out_specs=pl.BlockSpec((rows * ROW_SUB, LANES), lambda i: (i, 0)),
        out_shape=jax.ShapeDtypeStruct((n * ROW_SUB, LANES), I32),
        compiler_params=pltpu.CompilerParams(dimension_semantics=("arbitrary",), vmem_limit_bytes=VMEM_LIMIT),
        name="pack_table",
    )(tab)


def _block_mean(width):
    blk = np.arange(width) // HEAD_DIM
    return jnp.asarray((blk[:, None] == blk[None, :]) / HEAD_DIM, BF16)


def _forget_placement():
    pfq = np.zeros((3 * LANES, A_HEADS * LANES), np.float32)
    pfk = np.zeros((3 * LANES, A_HEADS * LANES), np.float32)
    cq = np.zeros((1, A_HEADS * LANES), np.float32)
    ck = np.zeros((1, A_HEADS * LANES), np.float32)
    for hd in range(A_HEADS):
        base = hd * LANES + HEAD_DIM
        for piece in range(3):
            pfq[piece * LANES + hd, base + piece] = 1.0
            pfk[piece * LANES + hd, base + 3 + piece] = -1.0
            cq[0, base + 3 + piece] = 1.0
            ck[0, base + piece] = 1.0
    return jnp.asarray(pfq, BF16), jnp.asarray(pfk, BF16), jnp.asarray(cq), jnp.asarray(ck)


def _layer_weights(l, g_attn, w_in, b_f, g_q_a, g_k_a, conv_w, conv_b, g_q_m, w_out, g_ffn, w_peer_q, peer_keys):
    scale = HEAD_DIM ** -0.5
    splits = np.cumsum([A_WIDTH, A_WIDTH, A_WIDTH, A_HEADS, B_WIDTH, B_WIDTH, B_WIDTH, M_WIDTH])
    wi = w_in[l]
    qa, ka, va, fl, bg, cg, hv, qm = [wi[:, a:b] for a, b in zip(np.r_[0, splits[:-1]], splits)]
    w_packed = jnp.concatenate([qa, ka, va, bg, cg, hv, qm, fl, jnp.zeros((wi.shape[0], LANES - A_HEADS), wi.dtype)],
                               axis=1).astype(BF16)
    pfq, pfk, cq, ck = _forget_placement()
    keys = peer_keys[l]
    zeros = jnp.zeros_like(keys[:, 0])
    keys2 = jnp.concatenate([jnp.concatenate([keys[:, 0], zeros], axis=-1),
                             jnp.concatenate([zeros, keys[:, 1]], axis=-1)], axis=1).astype(BF16)
    return {
        "g_attn": g_attn[l][None, :], "w_in": w_packed,
        "b_f": jnp.pad(b_f[l], (0, LANES - A_HEADS))[None, :],
        "g_q": (jnp.tile(g_q_a[l], A_HEADS) * scale)[None, :], "g_k": jnp.tile(g_k_a[l], A_HEADS)[None, :],
        "g_qm": (jnp.tile(g_q_m[l], M_HEADS) * scale)[None, :],
        "bd512": _block_mean(A_WIDTH), "bd256": _block_mean(M_WIDTH),
        "pfq": pfq, "pfk": pfk, "cq": cq, "ck": ck,
        "conv_w": conv_w[l], "conv_b": conv_b[l][None, :],
        "w_out": w_out[l].astype(BF16), "g_ffn": g_ffn[l][None, :], "w_pq": w_peer_q[l].astype(BF16), "keys": keys2,
    }


def _peer_and_merge(x, oa, ob, om, wts, u_packed, v_packed, route_tile, peer_tile):
    b, s, d = x.shape
    n = b * s
    x1, xn, e, gates = _route_call(x.reshape(n, d), oa.reshape(n, -1), ob.reshape(n, -1), om.reshape(n, -1), wts, route_tile)
    consts = _peer_constants()
    w = _peer_u_call(e, xn.reshape(n * SUBLANES, LANES), gates, u_packed, consts, peer_tile)
    y8 = _peer_v_call(e, w, x1.reshape(n * SUBLANES, LANES), v_packed, consts, peer_tile)
    return y8.reshape(b, s, d)


def kernel(x_prompt, x_sample, cache_a_k, cache_a_v, cache_a_logf, cache_b_conv, cache_m_k, cache_m_v, mem_prompt, g_attn, w_in, b_f, g_q_a, g_k_a, conv_w, conv_b, g_mem, w_mem_k, w_mem_v, g_k_m, g_q_m, w_out, g_ffn, w_peer_q, peer_keys, peer_u, peer_v):
    depth = w_in.shape[0]
    xp, xs = x_prompt, x_sample
    bp, sp, _ = xp.shape
    bs, ts, _ = xs.shape
    outs = [[] for _ in range(10)]
    for l in range(depth):
        wts = _layer_weights(l, g_attn, w_in, b_f, g_q_a, g_k_a, conv_w, conv_b, g_q_m, w_out, g_ffn, w_peer_q, peer_keys)
        u_packed, v_packed = _pack_table(peer_u[l]), _pack_table(peer_v[l])
        fox_blk = min(512, sp)
        route_tile = 256
        peer_tile = 256

        mk, mv = _memkv_call(mem_prompt, g_mem[l][None, :], w_mem_k[l].astype(BF16), w_mem_v[l].astype(BF16),
                             jnp.tile(g_k_m[l], M_HEADS)[None, :], wts["bd256"])
        ka, va, logf, qaug, kaug, vb, ob, om, cst = _proj_call(
            xp, jnp.zeros((bp, 2, B_WIDTH), F32), mk, mv, wts, min(512, sp))
        oa = _fox_prompt_call(qaug, kaug, vb, min(2 * fox_blk, sp), fox_blk)
        xp = _peer_and_merge(xp, oa, ob, om, wts, u_packed, v_packed, route_tile, peer_tile)
        n_mem = mk.shape[1]
        for dst, val in zip(outs[:6], [ka.reshape(bp, sp, A_HEADS, HEAD_DIM), va.reshape(bp, sp, A_HEADS, HEAD_DIM), logf, cst,
                                       mk.reshape(bp, n_mem, M_HEADS, HEAD_DIM), mv.reshape(bp, n_mem, M_HEADS, HEAD_DIM)]):
            dst.append(val)

        past = cache_a_k.shape[2]
        ka, va, logf, qaug, kaug, vb, ob, om, cst = _proj_call(
            xs, cache_b_conv[l], cache_m_k[l].reshape(bs, -1, M_WIDTH), cache_m_v[l].reshape(bs, -1, M_WIDTH), wts, ts)
        clf = jnp.pad(cache_a_logf[l], ((0, 0), (0, 0), (0, LANES - A_HEADS)))
        oa = _fox_sample_call(cache_a_k[l].reshape(bs, past, A_WIDTH), cache_a_v[l].reshape(bs, past, A_WIDTH), clf,
                              qaug, kaug, vb, wts["pfk"], wts["ck"])
        xs = _peer_and_merge(xs, oa, ob, om, wts, u_packed, v_packed, route_tile, peer_tile)
        for dst, val in zip(outs[6:], [ka.reshape(bs, ts, A_HEADS, HEAD_DIM), va.reshape(bs, ts, A_HEADS, HEAD_DIM), logf, cst]):
            dst.append(val)

    return (xp, xs) + tuple(jnp.stack(o) for o in outs)
```
